```python
import jax, jax.numpy as jnp
from jax import lax
import numpy as np

D_MODEL = 2048
BATCH = 4
SEQ = 2048
DEPTH = 2
DEC_BATCH = 128
DEC_SEQ = 8
PAST_LEN = 16384
PAGE_SIZE = 128

D_MIX = D_MODEL
N_GROUPS = 4
W_GROUP = D_MIX // N_GROUPS
A_HEADS = 4
A_HD = W_GROUP // A_HEADS
A_CHUNK = 128
R_HEADS = 4
R_HD = W_GROUP // R_HEADS
R_CHUNK = 128
ROPE_BASE = 10000.0
C_WIDTH = 3
K_HD = 64
K_HEADS = W_GROUP // K_HD
W_LORA = D_MODEL // 32
A_LORA = D_MODEL // 32
G_LORA = D_MODEL // 16
D_FF = 256 * ((8 * D_MODEL // 3 + 255) // 256)
EPS = 1e-6
GN_EPS = 64e-5
A_COLS = 2 * W_GROUP
R_COLS = 4 * W_GROUP
C_COLS = 3 * W_GROUP
K_COLS = 3 * W_GROUP + W_LORA + A_LORA + G_LORA
N_COLS = A_COLS + R_COLS + C_COLS + K_COLS

kernel_name = 'hybrid_parallel_groups_decode_step'

F32 = jnp.float32


def rmsnorm(x, g):
    xf = x.astype(F32)
    y = xf * lax.rsqrt(jnp.mean(xf * xf, axis=-1, keepdims=True) + EPS) * g.astype(F32)
    return y.astype(x.dtype)


def swiglu(x, w_gate, w_up, w_down):
    return (jax.nn.silu(x @ w_gate) * (x @ w_up)) @ w_down


def rope(x, pos):
    half = x.shape[-1] // 2
    inv = ROPE_BASE ** (-jnp.arange(half, dtype=F32) / half)
    ang = pos.astype(F32)[:, None] * inv[None, :]
    cos = jnp.cos(ang)[None, :, None, :]
    sin = jnp.sin(ang)[None, :, None, :]
    x1, x2 = x[..., :half], x[..., half:]
    return jnp.concatenate([x1 * cos - x2 * sin, x1 * sin + x2 * cos], axis=-1)


def chunk_spatial_gating(pa, w_s, b_s, ln_g, ln_b):
    bsz, L, _ = pa.shape
    u, v = jnp.split(jax.nn.gelu(pa.astype(F32)), 2, axis=-1)
    mu = jnp.mean(v, axis=-1, keepdims=True)
    var = jnp.mean(jnp.square(v - mu), axis=-1, keepdims=True)
    v = (v - mu) * lax.rsqrt(var + EPS) * ln_g.astype(F32) + ln_b.astype(F32)
    cl = min(A_CHUNK, L)
    n = L // cl
    wm = jnp.tril(w_s.astype(F32)[:, :cl, :cl])
    vh = v.reshape(bsz, n, cl, A_HEADS, A_HD)
    z = jnp.einsum('hts,bnshd->bnthd', wm, vh) + b_s.astype(F32)[:, :cl].T[None, None, :, :, None]
    out = u * z.reshape(bsz, L, W_GROUP)
    return out.astype(pa.dtype), v.astype(pa.dtype)


def retention(pr, pos, s0):
    q, k, v, g = jnp.split(pr.astype(F32), 4, axis=-1)
    bsz, L, _ = q.shape
    q = rope(q.reshape(bsz, L, R_HEADS, R_HD), pos)
    k = rope(k.reshape(bsz, L, R_HEADS, R_HD), pos) * (R_HD ** -0.5)
    v = v.reshape(bsz, L, R_HEADS, R_HD)
    log_gamma = jnp.log(1.0 - 2.0 ** (-5.0 - jnp.arange(R_HEADS, dtype=F32)))
    cl = min(R_CHUNK, L)
    n = L // cl
    qc = q.reshape(bsz, n, cl, R_HEADS, R_HD)
    kc = k.reshape(bsz, n, cl, R_HEADS, R_HD)
    vc = v.reshape(bsz, n, cl, R_HEADS, R_HD)
    idx = jnp.arange(cl, dtype=F32)
    diff = idx[:, None] - idx[None, :]
    dmat = jnp.where(diff >= 0, jnp.exp(jnp.maximum(diff, 0.0)[None] * log_gamma[:, None, None]), 0.0)
    scores = jnp.einsum('bnihd,bnjhd->bnhij', qc, kc) * dmat
    intra = jnp.einsum('bnhij,bnjhe->bnihe', scores, vc)
    kdec = jnp.exp((cl - 1.0 - idx)[:, None] * log_gamma[None, :])
    kv = jnp.einsum('bnjhd,bnjhe,jh->nbhde', kc, vc, kdec)
    chunk_decay = jnp.exp(cl * log_gamma)[None, :, None, None]

    def step(s, kv_n):
        return chunk_decay * s + kv_n, s

    s_last, s_prev = lax.scan(step, s0.astype(F32), kv)
    qdec = jnp.exp((idx + 1.0)[:, None] * log_gamma[None, :])
    cross = jnp.einsum('bnihd,nbhde,ih->bnihe', qc, s_prev, qdec)
    o = intra + cross
    o = o * lax.rsqrt(jnp.mean(o * o, axis=-1, keepdims=True) + EPS)
    o = o.reshape(bsz, L, W_GROUP) * jax.nn.silu(g)
    return o.astype(pr.dtype), s_last


def short_conv(pc, buf, conv_w):
    bg, cg, h = jnp.split(pc, 3, axis=-1)
    z = cg * h
    L = z.shape[1]
    zp = jnp.concatenate([buf.astype(z.dtype), z], axis=1)
    w = conv_w.astype(z.dtype)
    y = w[0] * zp[:, 0:L]
    for j in range(1, C_WIDTH):
        y = y + w[j] * zp[:, j:j + L]
    return bg * y, zp[:, -(C_WIDTH - 1):]


def rwkv7(pk, shift, s0, mu, w0, w2, a0, a2, g2, k_k, k_a, r_k, ln_w, ln_b):
    bsz, L, _ = pk.shape
    pkf = pk.astype(F32)
    prev = jnp.concatenate([shift[:, None].astype(F32), pkf[:, :-1]], axis=1)
    xs = pkf + (prev - pkf) * mu.astype(F32)
    W = W_GROUP
    r, k, v, wl, al, gl = jnp.split(xs, [W, 2 * W, 3 * W, 3 * W + W_LORA, 3 * W + W_LORA + A_LORA], axis=-1)
    wlog = -jax.nn.softplus(-(w0.astype(F32) + jnp.tanh(wl) @ w2.astype(F32))) - 0.5
    decay = jnp.exp(-jnp.exp(wlog))
    a = jax.nn.sigmoid(a0.astype(F32) + al @ a2.astype(F32))
    g = jax.nn.sigmoid(gl) @ g2.astype(F32)
    hs = (bsz, L, K_HEADS, K_HD)
    kk = (k * k_k.astype(F32)).reshape(hs)
    kk = kk / jnp.maximum(jnp.sqrt(jnp.sum(kk * kk, axis=-1, keepdims=True)), 1e-12)
    k = k * (1.0 + (a - 1.0) * k_a.astype(F32))
    r_h, k_h, v_h = r.reshape(hs), k.reshape(hs), v.reshape(hs)
    w_h, a_h = decay.reshape(hs), a.reshape(hs)

    def step(s, inp):
        r_t, k_t, v_t, w_t, kk_t, a_t = inp
        sk = jnp.einsum('bhvk,bhk->bhv', s, kk_t)
        s = s * w_t[:, :, None, :] - sk[..., None] * (kk_t * a_t)[:, :, None, :] + v_t[..., None] * k_t[:, :, None, :]
        y_t = jnp.einsum('bhvk,bhk->bhv', s, r_t)
        return s, y_t

    seq_in = (jnp.moveaxis(r_h, 1, 0), jnp.moveaxis(k_h, 1, 0), jnp.moveaxis(v_h, 1, 0),
              jnp.moveaxis(w_h, 1, 0), jnp.moveaxis(kk, 1, 0), jnp.moveaxis(a_h, 1, 0))
    s_last, y = lax.scan(step, s0.astype(F32), seq_in)
    y = jnp.moveaxis(y, 0, 1)
    mu_y = jnp.mean(y, axis=-1, keepdims=True)
    var_y = jnp.mean(jnp.square(y - mu_y), axis=-1, keepdims=True)
    y = ((y - mu_y) * lax.rsqrt(var_y + GN_EPS)).reshape(bsz, L, W_GROUP) * ln_w.astype(F32) + ln_b.astype(F32)
    bonus = jnp.sum(r_h * k_h * r_k.astype(F32), axis=-1, keepdims=True) * v_h
    y = (y + bonus.reshape(bsz, L, W_GROUP)) * g
    return y.astype(pk.dtype), pk[:, -1], s_last


def trunk_layer(x, pos, ret_s0, conv_buf, rw_shift, rw_s0, p):
    x = x + 0.5 * swiglu(rmsnorm(x, p['ffn1_norm']), p['ffn1_w_gate'], p['ffn1_w_up'], p['ffn1_w_down'])
    h = rmsnorm(x, p['mix_norm'])
    proj = h @ p['w_in']
    pa, pr, pc, pk = jnp.split(proj, [A_COLS, A_COLS + R_COLS, A_COLS + R_COLS + C_COLS], axis=-1)
    ya, v_rows = chunk_spatial_gating(pa, p['a_w_s'], p['a_b_s'], p['a_ln_g'], p['a_ln_b'])
    yb, ret_s = retention(pr, pos, ret_s0)
    yc, conv_new = short_conv(pc, conv_buf, p['c_conv_w'])
    yd, shift_new, rw_s = rwkv7(pk, rw_shift, rw_s0, p['k_mu'], p['k_w0'], p['k_w2'], p['k_a0'], p['k_a2'],
                                p['k_g2'], p['k_k_k'], p['k_k_a'], p['k_r_k'], p['k_ln_w'], p['k_ln_b'])
    mix = jnp.concatenate([ya.astype(x.dtype), yb.astype(x.dtype), yc.astype(x.dtype), yd.astype(x.dtype)], axis=-1)
    x = x + mix @ p['w_out']
    x = x + 0.5 * swiglu(rmsnorm(x, p['ffn2_norm']), p['ffn2_w_gate'], p['ffn2_w_up'], p['ffn2_w_down'])
    return x, ret_s, conv_new, shift_new, rw_s, v_rows


def setup_inputs(seed: int = 0) -> dict:
    key = jax.random.key(seed)
    ks = jax.random.split(key, 40)
    nrm = lambda i, shape, s: jax.random.normal(ks[i], shape, F32) * s
    return {
        'x_prompt': nrm(0, (BATCH, SEQ, D_MODEL), 1.0),
        'x_sample': nrm(1, (DEC_BATCH, DEC_SEQ, D_MODEL), 1.0),
        'state_ret': nrm(2, (DEPTH, DEC_BATCH, R_HEADS, R_HD, R_HD), 0.1),
        'state_conv': nrm(3, (DEPTH, DEC_BATCH, C_WIDTH - 1, W_GROUP), 1.0),
        'state_rwkv_shift': nrm(4, (DEPTH, DEC_BATCH, K_COLS), 1.0),
        'state_rwkv': nrm(5, (DEPTH, DEC_BATCH, K_HEADS, K_HD, K_HD), 0.1),
        'ffn1_norm': 1.0 + nrm(6, (DEPTH, D_MODEL), 0.01),
        'ffn1_w_gate': nrm(7, (DEPTH, D_MODEL, D_FF), D_MODEL ** -0.5),
        'ffn1_w_up': nrm(8, (DEPTH, D_MODEL, D_FF), D_MODEL ** -0.5),
        'ffn1_w_down': nrm(9, (DEPTH, D_FF, D_MODEL), D_FF ** -0.5),
        'mix_norm': 1.0 + nrm(10, (DEPTH, D_MODEL), 0.01),
        'w_in': nrm(11, (DEPTH, D_MODEL, N_COLS), D_MODEL ** -0.5),
        'w_out': nrm(12, (DEPTH, D_MIX, D_MODEL), D_MIX ** -0.5),
        'a_w_s': nrm(13, (DEPTH, A_HEADS, A_CHUNK, A_CHUNK), A_CHUNK ** -0.5),
        'a_b_s': 1.0 + nrm(14, (DEPTH, A_HEADS, A_CHUNK), 0.01),
        'a_ln_g': 1.0 + nrm(15, (DEPTH, W_GROUP), 0.01),
        'a_ln_b': nrm(16, (DEPTH, W_GROUP), 0.01),
        'c_conv_w': nrm(17, (DEPTH, C_WIDTH, W_GROUP), C_WIDTH ** -0.5),
        'k_mu': jax.random.uniform(ks[18], (DEPTH, K_COLS), F32),
        'k_w0': jax.random.uniform(ks[19], (DEPTH, W_GROUP), F32, -6.0, -1.0),
        'k_w2': nrm(20, (DEPTH, W_LORA, W_GROUP), 0.1),
        'k_a0': nrm(21, (DEPTH, W_GROUP), 0.1),
        'k_a2': nrm(22, (DEPTH, A_LORA, W_GROUP), 0.1),
        'k_g2': nrm(23, (DEPTH, G_LORA, W_GROUP), G_LORA ** -0.5),
        'k_k_k': 0.85 + nrm(24, (DEPTH, W_GROUP), 0.02),
        'k_k_a': 1.0 + nrm(25, (DEPTH, W_GROUP), 0.02),
        'k_r_k': nrm(26, (DEPTH, K_HEADS, K_HD), 0.1),
        'k_ln_w': 1.0 + nrm(27, (DEPTH, W_GROUP), 0.01),
        'k_ln_b': nrm(28, (DEPTH, W_GROUP), 0.01),
        'ffn2_norm': 1.0 + nrm(29, (DEPTH, D_MODEL), 0.01),
        'ffn2_w_gate': nrm(30, (DEPTH, D_MODEL, D_FF), D_MODEL ** -0.5),
        'ffn2_w_up': nrm(31, (DEPTH, D_MODEL, D_FF), D_MODEL ** -0.5),
        'ffn2_w_down': nrm(32, (DEPTH, D_FF, D_MODEL), D_FF ** -0.5),
        'final_norm': 1.0 + nrm(33, (D_MODEL,), 0.01),
    }


def reference(x_prompt, x_sample, state_ret, state_conv, state_rwkv_shift, state_rwkv,
              ffn1_norm, ffn1_w_gate, ffn1_w_up, ffn1_w_down, mix_norm, w_in, w_out,
              a_w_s, a_b_s, a_ln_g, a_ln_b, c_conv_w,
              k_mu, k_w0, k_w2, k_a0, k_a2, k_g2, k_k_k, k_k_a, k_r_k, k_ln_w, k_ln_b,
              ffn2_norm, ffn2_w_gate, ffn2_w_up, ffn2_w_down, final_norm):
    bp = x_prompt.shape[0]
    pos_p = jnp.arange(x_prompt.shape[1], dtype=F32)
    pos_s = PAST_LEN + jnp.arange(x_sample.shape[1], dtype=F32)
    zero_ret = jnp.zeros((bp, R_HEADS, R_HD, R_HD), F32)
    zero_conv = jnp.zeros((bp, C_WIDTH - 1, W_GROUP), x_prompt.dtype)
    zero_shift = jnp.zeros((bp, K_COLS), x_prompt.dtype)
    zero_rw = jnp.zeros((bp, K_HEADS, K_HD, K_HD), F32)

    xp, xs = x_prompt, x_sample
    ret_p, ret_s, conv_p, conv_s, shift_p, shift_s, rw_p, rw_s, v_s = [], [], [], [], [], [], [], [], []
    for l in range(DEPTH):
        p = dict(ffn1_norm=ffn1_norm[l], ffn1_w_gate=ffn1_w_gate[l], ffn1_w_up=ffn1_w_up[l],
                 ffn1_w_down=ffn1_w_down[l], mix_norm=mix_norm[l], w_in=w_in[l], w_out=w_out[l],
                 a_w_s=a_w_s[l], a_b_s=a_b_s[l], a_ln_g=a_ln_g[l], a_ln_b=a_ln_b[l], c_conv_w=c_conv_w[l],
                 k_mu=k_mu[l], k_w0=k_w0[l], k_w2=k_w2[l], k_a0=k_a0[l], k_a2=k_a2[l], k_g2=k_g2[l],
                 k_k_k=k_k_k[l], k_k_a=k_k_a[l], k_r_k=k_r_k[l], k_ln_w=k_ln_w[l], k_ln_b=k_ln_b[l],
                 ffn2_norm=ffn2_norm[l], ffn2_w_gate=ffn2_w_gate[l], ffn2_w_up=ffn2_w_up[l],
                 ffn2_w_down=ffn2_w_down[l])
        xp, rp, cp, sp, wp, _ = trunk_layer(xp, pos_p, zero_ret, zero_conv, zero_shift, zero_rw, p)
        xs, rs, cs, ss, ws, vs = trunk_layer(xs, pos_s, state_ret[l], state_conv[l], state_rwkv_shift[l],
                                             state_rwkv[l], p)
        ret_p.append(rp.astype(x_prompt.dtype)); ret_s.append(rs.astype(state_ret.dtype))
        conv_p.append(cp.astype(x_prompt.dtype)); conv_s.append(cs.astype(state_conv.dtype))
        shift_p.append(sp.astype(x_prompt.dtype)); shift_s.append(ss.astype(state_rwkv_shift.dtype))
        rw_p.append(wp.astype(x_prompt.dtype)); rw_s.append(ws.astype(state_rwkv.dtype))
        v_s.append(vs.astype(x_sample.dtype))

    y_prompt = rmsnorm(xp, final_norm)
    y_sample = rmsnorm(xs, final_norm)
    return (y_prompt, y_sample,
            jnp.stack(ret_p), jnp.stack(ret_s),
            jnp.stack(conv_p), jnp.stack(conv_s),
            jnp.stack(shift_p), jnp.stack(shift_s),
            jnp.stack(rw_p), jnp.stack(rw_s),
            jnp.stack(v_s))
```

```python
import functools
import math

import numpy as np
import jax
import jax.numpy as jnp
from jax import lax
from jax.experimental import pallas as pl
from jax.experimental.pallas import tpu as pltpu

F32 = jnp.float32
BF16 = jnp.bfloat16

W_GROUP = 512
A_CHUNK = 128
R_HEADS = 4
R_HD = 128
R_CHUNK = 128
ROPE_BASE = 10000.0
C_WIDTH = 3
K_HD = 64
K_HEADS = 8
W_LORA = 64
A_LORA = 64
G_LORA = 128
LORA_COLS = W_LORA + A_LORA + G_LORA
K_COLS = 3 * W_GROUP + LORA_COLS
EPS = 1e-6
GN_EPS = 64e-5
PAST_LEN = 16384

COL_A_U, COL_A_V = 0, 1
COL_R_Q, COL_R_K, COL_R_V, COL_R_G = 2, 3, 4, 5
COL_C_B, COL_C_C, COL_C_H = 6, 7, 8
COL_K_R, COL_K_K, COL_K_V = 9, 10, 11
COL_K_LORA = (12 * W_GROUP) // LORA_COLS
K_COL0 = 9 * W_GROUP

V7X_VMEM_LIMIT_BYTES = 56 * 1024 * 1024
RWKV_GROUP = 4


def _cparams(sem, vmem=V7X_VMEM_LIMIT_BYTES):
    return pltpu.CompilerParams(dimension_semantics=sem, vmem_limit_bytes=vmem)


def _rms(x, w):
    return x * lax.rsqrt(jnp.mean(x * x, axis=-1, keepdims=True) + EPS) * w


def _ffn_body(x_ref, nw_ref, wg_ref, wu_ref, wd_ref, fn_ref, o_ref, hn_ref, *, n_f, final):
    j = pl.program_id(1)

    @pl.when(j == 0)
    def _():
        hn_ref[...] = _rms(x_ref[...], nw_ref[...]).astype(BF16)
        o_ref[...] = jnp.zeros_like(o_ref)

    h = hn_ref[...]
    g = jnp.dot(h, wg_ref[...], preferred_element_type=F32)
    u = jnp.dot(h, wu_ref[...], preferred_element_type=F32)
    a = (g * jax.nn.sigmoid(g) * u).astype(BF16)
    o_ref[...] += jnp.dot(a, wd_ref[...], preferred_element_type=F32)

    @pl.when(j == n_f - 1)
    def _():
        y = x_ref[...] + 0.5 * o_ref[...]
        if final:
            y = _rms(y, fn_ref[...])
        o_ref[...] = y


def _ffn(x, nw, wg, wu, wd, fn, *, final, tm, tf):
    m, d = x.shape
    f = wg.shape[1]
    n_f = f // tf
    return pl.pallas_call(
        functools.partial(_ffn_body, n_f=n_f, final=final),
        grid=(m // tm, n_f),
        in_specs=[
            pl.BlockSpec((tm, d), lambda i, j: (i, 0)),
            pl.BlockSpec((1, d), lambda i, j: (0, 0)),
            pl.BlockSpec((d, tf), lambda i, j: (0, j)),
            pl.BlockSpec((d, tf), lambda i, j: (0, j)),
            pl.BlockSpec((tf, d), lambda i, j: (j, 0)),
            pl.BlockSpec((1, d), lambda i, j: (0, 0)),
        ],
        out_specs=pl.BlockSpec((tm, d), lambda i, j: (i, 0)),
        out_shape=jax.ShapeDtypeStruct((m, d), F32),
        scratch_shapes=[pltpu.VMEM((tm, d), BF16)],
        compiler_params=_cparams(("parallel", "arbitrary")),
        name="ffn_final" if final else "ffn",
    )(x, nw, wg, wu, wd, fn)


def _proj_body(x_ref, nw_ref, w_ref, o_ref, hn_ref):
    @pl.when(pl.program_id(1) == 0)
    def _():
        hn_ref[...] = _rms(x_ref[...], nw_ref[...]).astype(BF16)

    o_ref[...] = jnp.dot(hn_ref[...], w_ref[...], preferred_element_type=F32)


def _proj(x, nw, w, *, tm, tn):
    m, d = x.shape
    n = w.shape[1]
    return pl.pallas_call(
        _proj_body,
        grid=(m // tm, n // tn),
        in_specs=[
            pl.BlockSpec((tm, d), lambda i, j: (i, 0)),
            pl.BlockSpec((1, d), lambda i, j: (0, 0)),
            pl.BlockSpec((d, tn), lambda i, j: (0, j)),
        ],
        out_specs=pl.BlockSpec((tm, tn), lambda i, j: (i, j)),
        out_shape=jax.ShapeDtypeStruct((m, n), F32),
        scratch_shapes=[pltpu.VMEM((tm, d), BF16)],
        compiler_params=_cparams(("parallel", "arbitrary")),
        name="in_proj",
    )(x, nw, w)


def _outproj_body(x_ref, ya_ref, yb_ref, yc_ref, yd_ref, w_ref, o_ref):
    acc = x_ref[...]
    for gi, y_ref in enumerate((ya_ref, yb_ref, yc_ref, yd_ref)):
        acc = acc + jnp.dot(y_ref[...], w_ref[gi * W_GROUP:(gi + 1) * W_GROUP, :],
                            preferred_element_type=F32)
    o_ref[...] = acc


def _outproj(x, ya, yb, yc, yd, w, *, tm):
    m, d = x.shape
    yspec = pl.BlockSpec((tm, W_GROUP), lambda i: (i, 0))
    return pl.pallas_call(
        _outproj_body,
        grid=(m // tm,),
        in_specs=[pl.BlockSpec((tm, d), lambda i: (i, 0)), yspec, yspec, yspec, yspec,
                  pl.BlockSpec(w.shape, lambda i: (0, 0))],
        out_specs=pl.BlockSpec((tm, d), lambda i: (i, 0)),
        out_shape=jax.ShapeDtypeStruct((m, d), F32),
        compiler_params=_cparams(("parallel",)),
        name="out_proj",
    )(x, ya, yb, yc, yd, w)


def _gate_body(u_ref, v_ref, wm_ref, bias_ref, g_ref, b_ref, y_ref, *vr_ref, tm):
    gu = jax.nn.gelu(u_ref[...], approximate=True)
    gv = jax.nn.gelu(v_ref[...], approximate=True)
    mu = jnp.mean(gv, axis=-1, keepdims=True)
    var = jnp.mean(jnp.square(gv - mu), axis=-1, keepdims=True)
    vn = (gv - mu) * lax.rsqrt(var + EPS) * g_ref[...] + b_ref[...]
    if vr_ref:
        vr_ref[0][...] = vn
    vnb = vn.astype(BF16)
    for c in range(tm // A_CHUNK):
        rows = slice(c * A_CHUNK, (c + 1) * A_CHUNK)
        for h in range(W_GROUP // A_CHUNK):
            cols = slice(h * A_CHUNK, (h + 1) * A_CHUNK)
            z = jnp.dot(wm_ref[h], vnb[rows, cols], preferred_element_type=F32) + bias_ref[:, cols]
            y_ref[rows, cols] = (gu[rows, cols] * z).astype(BF16)


def _gate(p2d, wm, bias, ln_g, ln_b, *, tm, with_rows):
    m = p2d.shape[0]
    row_spec = pl.BlockSpec((tm, W_GROUP), lambda i: (i, 0))
    out_shape = [jax.ShapeDtypeStruct((m, W_GROUP), BF16)]
    out_specs = [row_spec]
    if with_rows:
        out_shape.append(jax.ShapeDtypeStruct((m, W_GROUP), F32))
        out_specs.append(row_spec)
    res = pl.pallas_call(
        functools.partial(_gate_body, tm=tm),
        grid=(m // tm,),
        in_specs=[
            pl.BlockSpec((tm, W_GROUP), lambda i: (i, COL_A_U)),
            pl.BlockSpec((tm, W_GROUP), lambda i: (i, COL_A_V)),
            pl.BlockSpec(wm.shape, lambda i: (0, 0, 0)),
            pl.BlockSpec(bias.shape, lambda i: (0, 0)),
            pl.BlockSpec((1, W_GROUP), lambda i: (0, 0)),
            pl.BlockSpec((1, W_GROUP), lambda i: (0, 0)),
        ],
        out_specs=out_specs,
        out_shape=out_shape,
        compiler_params=_cparams(("parallel",)),
        name="spatial_gate",
    )(p2d, p2d, wm, bias, ln_g, ln_b)
    return res if with_rows else (res[0], None)


def _ret_body(q_ref, k_ref, v_ref, g_ref, cos_ref, sin_ref, dm_ref, qd_ref, kd_ref, s0_ref,
              y_ref, so_ref, s_ref, *, bb, n_c, chunk_decay):
    c = pl.program_id(1)

    @pl.when(c == 0)
    def _():
        s_ref[...] = s0_ref[...]

    cos = cos_ref[...]
    sin = sin_ref[...]
    nt = (((1,), (1,)), ((), ()))
    tn = (((0,), (0,)), ((), ()))
    for b in range(bb):
        for h in range(R_HEADS):
            cols = slice(h * R_HD, (h + 1) * R_HD)
            q = q_ref[b, :, cols]
            k = k_ref[b, :, cols]
            v = v_ref[b, :, cols].astype(BF16)
            qr = q * cos + pltpu.roll(q, R_HD // 2, axis=1) * sin
            kr = (k * cos + pltpu.roll(k, R_HD // 2, axis=1) * sin) * (R_HD ** -0.5)
            sc = lax.dot_general(qr.astype(BF16), kr.astype(BF16), nt,
                                 preferred_element_type=F32) * dm_ref[h]
            s = s_ref[b, h]
            o = jnp.dot(sc.astype(BF16), v, preferred_element_type=F32)
            o = o + jnp.dot((qr * qd_ref[:, cols]).astype(BF16), s.astype(BF16),
                            preferred_element_type=F32)
            kv = lax.dot_general((kr * kd_ref[:, cols]).astype(BF16), v, tn,
                                 preferred_element_type=F32)
            s_ref[b, h] = chunk_decay[h] * s + kv
            o = o * lax.rsqrt(jnp.mean(o * o, axis=-1, keepdims=True) + EPS)
            g = g_ref[b, :, cols]
            y_ref[b, :, cols] = (o * (g * jax.nn.sigmoid(g))).astype(BF16)

    @pl.when(c == n_c - 1)
    def _():
        so_ref[...] = s_ref[...]


def _ret_tables(cl, pos0, length):
    half = R_HD // 2
    inv = ROPE_BASE ** (-jnp.arange(half, dtype=F32) / half)
    pos = pos0 + jnp.arange(length, dtype=F32)
    ang = pos[:, None] * inv[None, :]
    cos = jnp.cos(ang)
    sin = jnp.sin(ang)
    cos_t = jnp.concatenate([cos, cos], axis=-1)
    sin_t = jnp.concatenate([-sin, sin], axis=-1)
    log_gamma = np.log(1.0 - 2.0 ** (-5.0 - np.arange(R_HEADS, dtype=np.float64)))
    idx = np.arange(cl, dtype=np.float64)
    diff = idx[:, None] - idx[None, :]
    dmat = np.where(diff >= 0, np.exp(np.maximum(diff, 0.0)[None] * log_gamma[:, None, None]), 0.0)
    kdec = np.exp((cl - 1.0 - idx)[:, None] * log_gamma[None, :])
    qdec = np.exp((idx + 1.0)[:, None] * log_gamma[None, :])
    chunk_decay = tuple(float(x) for x in np.exp(cl * log_gamma))
    rep = lambda a: jnp.asarray(np.repeat(a, R_HD, axis=1), F32)
    return cos_t, sin_t, jnp.asarray(dmat, F32), rep(qdec), rep(kdec), chunk_decay


def _retention(p3d, s0, pos0, *, bb):
    bsz, length, _ = p3d.shape
    cl = min(R_CHUNK, length)
    n_c = length // cl
    cos_t, sin_t, dmat, qdec, kdec, chunk_decay = _ret_tables(cl, pos0, length)

    def col(j):
        return pl.BlockSpec((bb, cl, W_GROUP), lambda b, c: (b, c, j))

    tab = pl.BlockSpec((cl, R_HD), lambda b, c: (c, 0))
    state = pl.BlockSpec((bb, R_HEADS, R_HD, R_HD), lambda b, c: (b, 0, 0, 0))
    return pl.pallas_call(
        functools.partial(_ret_body, bb=bb, n_c=n_c, chunk_decay=chunk_decay),
        grid=(bsz // bb, n_c),
        in_specs=[col(COL_R_Q), col(COL_R_K), col(COL_R_V), col(COL_R_G), tab, tab,
                  pl.BlockSpec(dmat.shape, lambda b, c: (0, 0, 0)),
                  pl.BlockSpec(qdec.shape, lambda b, c: (0, 0)),
                  pl.BlockSpec(kdec.shape, lambda b, c: (0, 0)),
                  state],
        out_specs=[pl.BlockSpec((bb, cl, W_GROUP), lambda b, c: (b, c, 0)), state],
        out_shape=[jax.ShapeDtypeStruct((bsz, length, W_GROUP), BF16),
                   jax.ShapeDtypeStruct(s0.shape, F32)],
        scratch_shapes=[pltpu.VMEM((bb, R_HEADS, R_HD, R_HD), F32)],
        compiler_params=_cparams(("parallel", "arbitrary")),
        name="retention",
    )(p3d, p3d, p3d, p3d, cos_t, sin_t, dmat, qdec, kdec, s0)


def _conv_body(bg_ref, cg_ref, h_ref, buf_ref, w_ref, y_ref, st_ref, carry_ref, *, bb, tt):
    @pl.when(pl.program_id(1) == 0)
    def _():
        carry_ref[...] = buf_ref[...]

    shape = (bb, tt, W_GROUP)
    z = cg_ref[...] * h_ref[...]
    z2 = z.reshape(bb * tt, W_GROUP)
    r1 = pltpu.roll(z2, 1, axis=0).reshape(shape)
    r2 = pltpu.roll(z2, 2, axis=0).reshape(shape)
    tpos = lax.broadcasted_iota(jnp.int32, shape, 1)
    c0 = carry_ref[:, 0:1, :]
    c1 = carry_ref[:, 1:2, :]
    zm1 = jnp.where(tpos == 0, c1, r1)
    zm2 = jnp.where(tpos == 0, c0, jnp.where(tpos == 1, c1, r2))
    y = w_ref[0:1, :] * zm2 + w_ref[1:2, :] * zm1 + w_ref[2:3, :] * z
    y_ref[...] = (bg_ref[...] * y).astype(BF16)
    new = cg_ref[:, tt - 2:tt, :] * h_ref[:, tt - 2:tt, :]
    carry_ref[...] = new
    st_ref[...] = new


def _conv(p3d, buf, w, *, bb, tt):
    bsz, length, _ = p3d.shape

    def col(j):
        return pl.BlockSpec((bb, tt, W_GROUP), lambda b, t: (b, t, j))

    state = pl.BlockSpec((bb, C_WIDTH - 1, W_GROUP), lambda b, t: (b, 0, 0))
    return pl.pallas_call(
        functools.partial(_conv_body, bb=bb, tt=tt),
        grid=(bsz // bb, length // tt),
        in_specs=[col(COL_C_B), col(COL_C_C), col(COL_C_H), state,
                  pl.BlockSpec(w.shape, lambda b, t: (0, 0))],
        out_specs=[pl.BlockSpec((bb, tt, W_GROUP), lambda b, t: (b, t, 0)), state],
        out_shape=[jax.ShapeDtypeStruct((bsz, length, W_GROUP), BF16),
                   jax.ShapeDtypeStruct(buf.shape, F32)],
        scratch_shapes=[pltpu.VMEM((bb, C_WIDTH - 1, W_GROUP), F32)],
        compiler_params=_cparams(("parallel", "arbitrary")),
        name="short_conv",
    )(p3d, p3d, p3d, buf, w)


def _head_sum(x, ones_bd):
    hi = x.astype(BF16)
    lo = (x - hi.astype(F32)).astype(BF16)
    return (jnp.dot(hi, ones_bd, preferred_element_type=F32)
            + jnp.dot(lo, ones_bd, preferred_element_type=F32))


def _rwkv_body(r_ref, k_ref, v_ref, lo_ref, sh_ref, s0_ref,
               mu_ref, mulo_ref, w0_ref, w2_ref, a0_ref, a2_ref, g2_ref, kk_ref, ka_ref, rk_ref,
               lnw_ref, lnb_ref, ones_ref,
               y_ref, so_ref,
               s_ref, carry_ref, carrylo_ref, rs, ws, ks, vs, kks, kas, ys, *, bb, tt, n_t):
    tb = pl.program_id(1)
    n = bb * tt
    w3 = (bb, tt, W_GROUP)
    ones_bd = ones_ref[...]

    @pl.when(tb == 0)
    def _():
        s_ref[...] = s0_ref[...]
        carry_ref[...] = sh_ref[:, :, 0:3 * W_GROUP]
        carrylo_ref[...] = sh_ref[:, :, 3 * W_GROUP:K_COLS]

    tpos = lax.broadcasted_iota(jnp.int32, w3, 1)
    tpos_lo = lax.broadcasted_iota(jnp.int32, (bb, tt, LORA_COLS), 1)

    def shifted(x, carry, mu, mask):
        prev = pltpu.roll(x.reshape(n, x.shape[-1]), 1, axis=0).reshape(x.shape)
        prev = jnp.where(mask == 0, carry, prev)
        return (x + (prev - x) * mu).reshape(n, x.shape[-1])

    r_in, k_in, v_in, lo_in = r_ref[...], k_ref[...], v_ref[...], lo_ref[...]
    r = shifted(r_in, carry_ref[:, :, 0:W_GROUP], mu_ref[:, 0:W_GROUP], tpos)
    k = shifted(k_in, carry_ref[:, :, W_GROUP:2 * W_GROUP], mu_ref[:, W_GROUP:2 * W_GROUP], tpos)
    v = shifted(v_in, carry_ref[:, :, 2 * W_GROUP:3 * W_GROUP], mu_ref[:, 2 * W_GROUP:3 * W_GROUP], tpos)
    lo = shifted(lo_in, carrylo_ref[...], mulo_ref[...], tpos_lo)
    carry_ref[:, :, 0:W_GROUP] = r_ref[:, tt - 1:tt, :]
    carry_ref[:, :, W_GROUP:2 * W_GROUP] = k_ref[:, tt - 1:tt, :]
    carry_ref[:, :, 2 * W_GROUP:3 * W_GROUP] = v_ref[:, tt - 1:tt, :]
    carrylo_ref[...] = lo_ref[:, tt - 1:tt, :]

    zw = w0_ref[...] + jnp.dot(jnp.tanh(lo).astype(BF16), w2_ref[...], preferred_element_type=F32)
    nz = -zw
    softplus = jnp.maximum(nz, 0.0) + jnp.log1p(jnp.exp(-jnp.abs(nz)))
    decay = jnp.exp(-jnp.exp(-softplus - 0.5))
    a = jax.nn.sigmoid(a0_ref[...] + jnp.dot(lo.astype(BF16), a2_ref[...], preferred_element_type=F32))
    gate = jnp.dot(jax.nn.sigmoid(lo).astype(BF16), g2_ref[...], preferred_element_type=F32)
    kk = k * kk_ref[...]
    kk = kk / jnp.maximum(jnp.sqrt(_head_sum(kk * kk, ones_bd)), 1e-12)
    k2 = k * (1.0 + (a - 1.0) * ka_ref[...])
    bonus = _head_sum(r * k2 * rk_ref[...], ones_bd) * v

    rs[...] = r.reshape(w3)
    ws[...] = decay.reshape(w3)
    ks[...] = k2.reshape(w3)
    vs[...] = v.reshape(w3)
    kks[...] = kk.reshape(w3)
    kas[...] = (kk * a).reshape(w3)

    gsz = RWKV_GROUP
    g3 = (gsz, K_HD, W_GROUP)
    diag = (lax.broadcasted_iota(jnp.int32, g3, 1)
            == lax.broadcasted_iota(jnp.int32, g3, 2) % K_HD)

    for g0 in range(0, bb, gsz):
        rows = pl.ds(g0, gsz)

        def step(t, carry):
            at = pl.ds(t, 1)
            s = s_ref[rows]
            sk = _head_sum((s * kks[rows, at, :]).reshape(gsz * K_HD, W_GROUP), ones_bd).reshape(g3)
            vdiag = jnp.where(diag, vs[rows, at, :], 0.0).astype(BF16).reshape(gsz * K_HD, W_GROUP)
            vcol = jnp.dot(vdiag, ones_bd, preferred_element_type=F32).reshape(g3)
            s = s * ws[rows, at, :] - sk * kas[rows, at, :] + vcol * ks[rows, at, :]
            s_ref[rows] = s
            sr = (s * rs[rows, at, :]).astype(BF16).reshape(gsz * K_HD, W_GROUP)
            ybc = jnp.dot(sr, ones_bd, preferred_element_type=F32).reshape(g3)
            ys[rows, at, :] = jnp.sum(jnp.where(diag, ybc, 0.0), axis=1, keepdims=True)
            return carry

        lax.fori_loop(0, tt, step, 0)

    y = ys[...].reshape(n, W_GROUP)
    mean = _head_sum(y, ones_bd) * (1.0 / K_HD)
    yc = y - mean
    var = _head_sum(yc * yc, ones_bd) * (1.0 / K_HD)
    out = (yc * lax.rsqrt(var + GN_EPS) * lnw_ref[...] + lnb_ref[...] + bonus) * gate
    y_ref[...] = out.reshape(w3).astype(BF16)

    @pl.when(tb == n_t - 1)
    def _():
        so_ref[...] = s_ref[...]


def _rwkv(p3d, shift, s0, prm, *, bb, tt):
    bsz, length, _ = p3d.shape
    n_t = length // tt

    def col(j):
        return pl.BlockSpec((bb, tt, W_GROUP), lambda b, t: (b, t, j))

    def whole(a):
        nd = a.ndim
        return pl.BlockSpec(a.shape, lambda b, t: (0,) * nd)

    state = pl.BlockSpec((bb, K_HD, W_GROUP), lambda b, t: (b, 0, 0))
    params = [prm[nm] for nm in ("mu", "mu_lo", "w0", "w2", "a0", "a2", "g2", "k_k", "k_a", "r_k",
                                 "ln_w", "ln_b", "ones_bd")]
    blk = pltpu.VMEM((bb, tt, W_GROUP), F32)
    return pl.pallas_call(
        functools.partial(_rwkv_body, bb=bb, tt=tt, n_t=n_t),
        grid=(bsz // bb, n_t),
        in_specs=[col(COL_K_R), col(COL_K_K), col(COL_K_V),
                  pl.BlockSpec((bb, tt, LORA_COLS), lambda b, t: (b, t, COL_K_LORA)),
                  pl.BlockSpec((bb, 1, K_COLS), lambda b, t: (b, 0, 0)),
                  state] + [whole(a) for a in params],
        out_specs=[pl.BlockSpec((bb, tt, W_GROUP), lambda b, t: (b, t, 0)), state],
        out_shape=[jax.ShapeDtypeStruct((bsz, length, W_GROUP), BF16),
                   jax.ShapeDtypeStruct(s0.shape, F32)],
        scratch_shapes=[pltpu.VMEM((bb, K_HD, W_GROUP), F32),
                        pltpu.VMEM((bb, 1, 3 * W_GROUP), F32),
                        pltpu.VMEM((bb, 1, LORA_COLS), F32),
                        blk, blk, blk, blk, blk, blk, blk],
        compiler_params=_cparams(("parallel", "arbitrary")),
        name="rwkv7",
    )(p3d, p3d, p3d, p3d, shift, s0, *params)


def _rwkv_state_in(s):
    b = s.shape[0]
    return jnp.transpose(s, (0, 2, 1, 3)).reshape(b, K_HD, W_GROUP)


def _rwkv_state_out(s):
    b = s.shape[0]
    return jnp.transpose(s.reshape(b, K_HD, K_HEADS, K_HD), (0, 2, 1, 3))


def _gate_mixing(w_s, b_s, seq):
    cl = min(A_CHUNK, seq)
    wm = jnp.tril(w_s[:, :cl, :cl])
    bias = b_s[:, :cl]
    rep = A_CHUNK // cl
    if rep > 1:
        eye = jnp.eye(rep, dtype=w_s.dtype)
        wm = jnp.einsum("ab,hts->hatbs", eye, wm).reshape(w_s.shape[0], A_CHUNK, A_CHUNK)
        bias = jnp.tile(bias, (1, rep))
    bias = jnp.repeat(bias.T, A_CHUNK, axis=1)
    return wm.astype(BF16), bias


def _pad_rows(w, row0):
    return jnp.zeros((LORA_COLS, W_GROUP), F32).at[row0:row0 + w.shape[0]].set(w).astype(BF16)


def _layer_params(l, ffn1_norm, ffn1_w_gate, ffn1_w_up, ffn1_w_down, mix_norm, w_in, w_out,
                  a_w_s, a_b_s, a_ln_g, a_ln_b, c_conv_w,
                  k_mu, k_w0, k_w2, k_a0, k_a2, k_g2, k_k_k, k_k_a, k_r_k, k_ln_w, k_ln_b,
                  ffn2_norm, ffn2_w_gate, ffn2_w_up, ffn2_w_down):
    row = lambda a: a[l].reshape(1, -1)
    head_of = np.arange(W_GROUP) // K_HD
    ones_bd = jnp.asarray(head_of[:, None] == head_of[None, :], BF16)
    rwkv = dict(
        mu=k_mu[l][None, :3 * W_GROUP], mu_lo=k_mu[l][None, 3 * W_GROUP:],
        w0=row(k_w0), w2=_pad_rows(k_w2[l], 0),
        a0=row(k_a0), a2=_pad_rows(k_a2[l], W_LORA),
        g2=_pad_rows(k_g2[l], W_LORA + A_LORA),
        k_k=row(k_k_k), k_a=row(k_k_a), r_k=row(k_r_k), ln_w=row(k_ln_w), ln_b=row(k_ln_b),
        ones_bd=ones_bd)
    return dict(
        ffn1=(row(ffn1_norm), ffn1_w_gate[l].astype(BF16), ffn1_w_up[l].astype(BF16),
              ffn1_w_down[l].astype(BF16)),
        ffn2=(row(ffn2_norm), ffn2_w_gate[l].astype(BF16), ffn2_w_up[l].astype(BF16),
              ffn2_w_down[l].astype(BF16)),
        mix_norm=row(mix_norm), w_in=w_in[l].astype(BF16), w_out=w_out[l].astype(BF16),
        a_w_s=a_w_s[l], a_b_s=a_b_s[l], a_ln_g=row(a_ln_g), a_ln_b=row(a_ln_b),
        conv_w=c_conv_w[l], rwkv=rwkv)


def _stream_layer(x2d, bsz, length, pos0, ret_s0, conv_buf, rw_shift, rw_s0, p, fn, *, final, cfg):
    d = x2d.shape[1]
    x1 = _ffn(x2d, *p["ffn1"], fn, final=False, tm=cfg["tm"], tf=cfg["tf"])
    proj = _proj(x1, p["mix_norm"], p["w_in"], tm=cfg["tm"], tn=cfg["tn"])
    p3d = proj.reshape(bsz, length, proj.shape[1])

    wm, bias = _gate_mixing(p["a_w_s"], p["a_b_s"], length)
    ya, v_rows = _gate(proj, wm, bias, p["a_ln_g"], p["a_ln_b"], tm=cfg["gate_tm"],
                       with_rows=cfg["with_rows"])
    yb, ret_s = _retention(p3d, ret_s0, pos0, bb=cfg["ret_bb"])
    yc, conv_new = _conv(p3d, conv_buf, p["conv_w"], bb=cfg["conv_bb"], tt=cfg["conv_tt"])
    yd, rw_s = _rwkv(p3d, rw_shift[:, None, :], _rwkv_state_in(rw_s0), p["rwkv"],
                     bb=cfg["rwkv_bb"], tt=cfg["rwkv_tt"])
    shift_new = p3d[:, length - 1, K_COL0:K_COL0 + K_COLS]

    flat = lambda y: y.reshape(bsz * length, W_GROUP)
    x2 = _outproj(x1, ya, flat(yb), flat(yc), flat(yd), p["w_out"], tm=cfg["tm"])
    x3 = _ffn(x2, *p["ffn2"], fn, final=final, tm=cfg["tm"], tf=cfg["tf"])
    return x3, ret_s, conv_new, shift_new, _rwkv_state_out(rw_s), v_rows


def _stream_cfg(bsz, length, sample):
    m = bsz * length
    tm = min(512, m)
    if sample:
        return dict(tm=tm, tf=512, tn=640, gate_tm=min(512, m), with_rows=True,
                    ret_bb=8, conv_bb=32, conv_tt=length, rwkv_bb=16, rwkv_tt=length)
    return dict(tm=tm, tf=512, tn=640, gate_tm=min(512, m), with_rows=False,
                ret_bb=1, conv_bb=1, conv_tt=min(512, length), rwkv_bb=bsz, rwkv_tt=min(128, length))


def kernel(x_prompt, x_sample, state_ret, state_conv, state_rwkv_shift, state_rwkv, ffn1_norm, ffn1_w_gate, ffn1_w_up, ffn1_w_down, mix_norm, w_in, w_out, a_w_s, a_b_s, a_ln_g, a_ln_b, c_conv_w, k_mu, k_w0, k_w2, k_a0, k_a2, k_g2, k_k_k, k_k_a, k_r_k, k_ln_w, k_ln_b, ffn2_norm, ffn2_w_gate, ffn2_w_up, ffn2_w_down, final_norm):
    bp, lp, d = x_prompt.shape
    bs, ls, _ = x_sample.shape
    depth = ffn1_norm.shape[0]
    cfg_p = _stream_cfg(bp, lp, sample=False)
    cfg_s = _stream_cfg(bs, ls, sample=True)
    fn = final_norm.reshape(1, d)

    zero_ret = jnp.zeros((bp, R_HEADS, R_HD, R_HD), F32)
    zero_conv = jnp.zeros((bp, C_WIDTH - 1, W_GROUP), F32)
    zero_shift = jnp.zeros((bp, K_COLS), F32)
    zero_rw = jnp.zeros((bp, K_HEADS, K_HD, K_HD), F32)

    xp = x_prompt.reshape(bp * lp, d)
    xs = x_sample.reshape(bs * ls, d)
    outs = [[] for _ in range(9)]
    for l in range(depth):
        p = _layer_params(l, ffn1_norm, ffn1_w_gate, ffn1_w_up, ffn1_w_down, mix_norm, w_in, w_out,
                          a_w_s, a_b_s, a_ln_g, a_ln_b, c_conv_w,
                          k_mu, k_w0, k_w2, k_a0, k_a2, k_g2, k_k_k, k_k_a, k_r_k, k_ln_w, k_ln_b,
                          ffn2_norm, ffn2_w_gate, ffn2_w_up, ffn2_w_down)
        final = l == depth - 1
        xp, rp, cp, sp, wp, _ = _stream_layer(xp, bp, lp, 0.0, zero_ret, zero_conv, zero_shift, zero_rw,
                                              p, fn, final=final, cfg=cfg_p)
        xs, rs, cs, ss, ws, vs = _stream_layer(xs, bs, ls, float(PAST_LEN), state_ret[l], state_conv[l],
                                               state_rwkv_shift[l], state_rwkv[l], p, fn,
                                               final=final, cfg=cfg_s)
        for acc, val in zip(outs, (rp, rs, cp, cs, sp, ss, wp, ws, vs.reshape(bs, ls, W_GROUP))):
            acc.append(val)

    return (xp.reshape(bp, lp, d), xs.reshape(bs, ls, d)) + tuple(jnp.stack(o) for o in outs)
```

```python
import functools
import math

import numpy as np
import jax
import jax.numpy as jnp
from jax import lax
from jax.experimental import pallas as pl
from jax.experimental.pallas import tpu as pltpu

F32 = jnp.float32
BF16 = jnp.bfloat16

W_GROUP = 512
A_CHUNK = 128
R_HEADS = 4
R_HD = 128
R_CHUNK = 128
ROPE_BASE = 10000.0
C_WIDTH = 3
K_HD = 64
K_HEADS = 8
W_LORA = 64
A_LORA = 64
G_LORA = 128
LORA_COLS = W_LORA + A_LORA + G_LORA
K_COLS = 3 * W_GROUP + LORA_COLS
EPS = 1e-6
GN_EPS = 64e-5
PAST_LEN = 16384

COL_A_U, COL_A_V = 0, 1
COL_R_Q, COL_R_K, COL_R_V, COL_R_G = 2, 3, 4, 5
COL_C_B, COL_C_C, COL_C_H = 6, 7, 8
COL_K_R, COL_K_K, COL_K_V = 9, 10, 11
COL_K_LORA = (12 * W_GROUP) // LORA_COLS
K_COL0 = 9 * W_GROUP

V7X_VMEM_LIMIT_BYTES = 56 * 1024 * 1024
V7X_MXU_DIM = 256
RWKV_GROUP = 4


def _cparams(sem, vmem=V7X_VMEM_LIMIT_BYTES):
    return pltpu.CompilerParams(dimension_semantics=sem, vmem_limit_bytes=vmem)


def _rms(x, w):
    return x * lax.rsqrt(jnp.mean(x * x, axis=-1, keepdims=True) + EPS) * w


def _ffn_body(x_ref, nw_ref, wg_ref, wu_ref, wd_ref, fn_ref, o_ref, hn_ref, *, n_f, final):
    j = pl.program_id(1)

    @pl.when(j == 0)
    def _():
        hn_ref[...] = _rms(x_ref[...], nw_ref[...]).astype(BF16)
        o_ref[...] = jnp.zeros_like(o_ref)

    h = hn_ref[...]
    g = jnp.dot(h, wg_ref[...], preferred_element_type=F32)
    u = jnp.dot(h, wu_ref[...], preferred_element_type=F32)
    a = (g * jax.nn.sigmoid(g) * u).astype(BF16)
    o_ref[...] += jnp.dot(a, wd_ref[...], preferred_element_type=F32)

    @pl.when(j == n_f - 1)
    def _():
        y = x_ref[...] + 0.5 * o_ref[...]
        if final:
            y = _rms(y, fn_ref[...])
        o_ref[...] = y


def _ffn(x, nw, wg, wu, wd, fn, *, layer, final, tm, tf):
    m, d = x.shape
    f = wg.shape[2]
    n_f = f // tf
    return pl.pallas_call(
        functools.partial(_ffn_body, n_f=n_f, final=final),
        grid=(m // tm, n_f),
        in_specs=[
            pl.BlockSpec((tm, d), lambda i, j: (i, 0)),
            pl.BlockSpec((1, d), lambda i, j: (0, 0)),
            pl.BlockSpec((None, d, tf), lambda i, j: (layer, 0, j)),
            pl.BlockSpec((None, d, tf), lambda i, j: (layer, 0, j)),
            pl.BlockSpec((None, tf, d), lambda i, j: (layer, j, 0)),
            pl.BlockSpec((1, d), lambda i, j: (0, 0)),
        ],
        out_specs=pl.BlockSpec((tm, d), lambda i, j: (i, 0)),
        out_shape=jax.ShapeDtypeStruct((m, d), F32),
        scratch_shapes=[pltpu.VMEM((tm, d), BF16)],
        compiler_params=_cparams(("parallel", "arbitrary")),
        name="ffn_final" if final else "ffn",
    )(x, nw, wg, wu, wd, fn)


def _proj_body(x_ref, nw_ref, w_ref, o_ref, hn_ref):
    @pl.when(pl.program_id(1) == 0)
    def _():
        hn_ref[...] = _rms(x_ref[...], nw_ref[...]).astype(BF16)

    o_ref[...] = jnp.dot(hn_ref[...], w_ref[...], preferred_element_type=F32)


def _proj(x, nw, w, *, layer, tm, tn):
    m, d = x.shape
    n = w.shape[2]
    return pl.pallas_call(
        _proj_body,
        grid=(m // tm, n // tn),
        in_specs=[
            pl.BlockSpec((tm, d), lambda i, j: (i, 0)),
            pl.BlockSpec((1, d), lambda i, j: (0, 0)),
            pl.BlockSpec((None, d, tn), lambda i, j: (layer, 0, j)),
        ],
        out_specs=pl.BlockSpec((tm, tn), lambda i, j: (i, j)),
        out_shape=jax.ShapeDtypeStruct((m, n), F32),
        scratch_shapes=[pltpu.VMEM((tm, d), BF16)],
        compiler_params=_cparams(("parallel", "arbitrary")),
        name="in_proj",
    )(x, nw, w)


def _outproj_body(x_ref, ya_ref, yb_ref, yc_ref, yd_ref, w_ref, o_ref):
    acc = x_ref[...]
    for gi, y_ref in enumerate((ya_ref, yb_ref, yc_ref, yd_ref)):
        acc = acc + jnp.dot(y_ref[...], w_ref[gi * W_GROUP:(gi + 1) * W_GROUP, :],
                            preferred_element_type=F32)
    o_ref[...] = acc


def _outproj(x, ya, yb, yc, yd, w, *, layer, tm):
    m, d = x.shape
    yspec = pl.BlockSpec((tm, W_GROUP), lambda i: (i, 0))
    return pl.pallas_call(
        _outproj_body,
        grid=(m // tm,),
        in_specs=[pl.BlockSpec((tm, d), lambda i: (i, 0)), yspec, yspec, yspec, yspec,
                  pl.BlockSpec((None,) + w.shape[1:], lambda i: (layer, 0, 0))],
        out_specs=pl.BlockSpec((tm, d), lambda i: (i, 0)),
        out_shape=jax.ShapeDtypeStruct((m, d), F32),
        compiler_params=_cparams(("parallel",)),
        name="out_proj",
    )(x, ya, yb, yc, yd, w)


def _gate_body(u_ref, v_ref, wm_ref, bias_ref, g_ref, b_ref, y_ref, *vr_ref, tm):
    gu = jax.nn.gelu(u_ref[...], approximate=True)
    gv = jax.nn.gelu(v_ref[...], approximate=True)
    mu = jnp.mean(gv, axis=-1, keepdims=True)
    var = jnp.mean(jnp.square(gv - mu), axis=-1, keepdims=True)
    vn = (gv - mu) * lax.rsqrt(var + EPS) * g_ref[...] + b_ref[...]
    if vr_ref:
        vr_ref[0][...] = vn
    vnb = vn.astype(BF16)
    for c in range(tm // A_CHUNK):
        rows = slice(c * A_CHUNK, (c + 1) * A_CHUNK)
        for h in range(W_GROUP // A_CHUNK):
            cols = slice(h * A_CHUNK, (h + 1) * A_CHUNK)
            z = jnp.dot(wm_ref[h], vnb[rows, cols], preferred_element_type=F32) + bias_ref[:, cols]
            y_ref[rows, cols] = (gu[rows, cols] * z).astype(BF16)


def _gate(p2d, wm, bias, ln_g, ln_b, *, tm, with_rows):
    m = p2d.shape[0]
    row_spec = pl.BlockSpec((tm, W_GROUP), lambda i: (i, 0))
    out_shape = [jax.ShapeDtypeStruct((m, W_GROUP), BF16)]
    out_specs = [row_spec]
    if with_rows:
        out_shape.append(jax.ShapeDtypeStruct((m, W_GROUP), F32))
        out_specs.append(row_spec)
    res = pl.pallas_call(
        functools.partial(_gate_body, tm=tm),
        grid=(m // tm,),
        in_specs=[
            pl.BlockSpec((tm, W_GROUP), lambda i: (i, COL_A_U)),
            pl.BlockSpec((tm, W_GROUP), lambda i: (i, COL_A_V)),
            pl.BlockSpec(wm.shape, lambda i: (0, 0, 0)),
            pl.BlockSpec(bias.shape, lambda i: (0, 0)),
            pl.BlockSpec((1, W_GROUP), lambda i: (0, 0)),
            pl.BlockSpec((1, W_GROUP), lambda i: (0, 0)),
        ],
        out_specs=out_specs,
        out_shape=out_shape,
        compiler_params=_cparams(("parallel",)),
        name="spatial_gate",
    )(p2d, p2d, wm, bias, ln_g, ln_b)
    return res if with_rows else (res[0], None)


def _ret_body(q_ref, k_ref, v_ref, g_ref, cos_ref, sin_ref, dm_ref, qd_ref, kd_ref, s0_ref,
              y_ref, so_ref, s_ref, *, bb, n_c, chunk_decay):
    c = pl.program_id(1)

    @pl.when(c == 0)
    def _():
        s_ref[...] = s0_ref[...]

    cos = cos_ref[...]
    sin = sin_ref[...]
    nt = (((1,), (1,)), ((), ()))
    tn = (((0,), (0,)), ((), ()))
    for b in range(bb):
        for h in range(R_HEADS):
            cols = slice(h * R_HD, (h + 1) * R_HD)
            q = q_ref[b, :, cols]
            k = k_ref[b, :, cols]
            v = v_ref[b, :, cols].astype(BF16)
            qr = q * cos + pltpu.roll(q, R_HD // 2, axis=1) * sin
            kr = (k * cos + pltpu.roll(k, R_HD // 2, axis=1) * sin) * (R_HD ** -0.5)
            sc = lax.dot_general(qr.astype(BF16), kr.astype(BF16), nt,
                                 preferred_element_type=F32) * dm_ref[h]
            s = s_ref[b, h]
            o = jnp.dot(sc.astype(BF16), v, preferred_element_type=F32)
            o = o + jnp.dot((qr * qd_ref[:, cols]).astype(BF16), s.astype(BF16),
                            preferred_element_type=F32)
            kv = lax.dot_general((kr * kd_ref[:, cols]).astype(BF16), v, tn,
                                 preferred_element_type=F32)
            s_ref[b, h] = chunk_decay[h] * s + kv
            o = o * lax.rsqrt(jnp.mean(o * o, axis=-1, keepdims=True) + EPS)
            g = g_ref[b, :, cols]
            y_ref[b, :, cols] = (o * (g * jax.nn.sigmoid(g))).astype(BF16)

    @pl.when(c == n_c - 1)
    def _():
        so_ref[...] = s_ref[...]


def _ret_tables(cl, pos0, length):
    half = R_HD // 2
    inv = ROPE_BASE ** (-jnp.arange(half, dtype=F32) / half)
    pos = pos0 + jnp.arange(length, dtype=F32)
    ang = pos[:, None] * inv[None, :]
    cos = jnp.cos(ang)
    sin = jnp.sin(ang)
    cos_t = jnp.concatenate([cos, cos], axis=-1)
    sin_t = jnp.concatenate([-sin, sin], axis=-1)
    log_gamma = np.log(1.0 - 2.0 ** (-5.0 - np.arange(R_HEADS, dtype=np.float64)))
    idx = np.arange(cl, dtype=np.float64)
    diff = idx[:, None] - idx[None, :]
    dmat = np.where(diff >= 0, np.exp(np.maximum(diff, 0.0)[None] * log_gamma[:, None, None]), 0.0)
    kdec = np.exp((cl - 1.0 - idx)[:, None] * log_gamma[None, :])
    qdec = np.exp((idx + 1.0)[:, None] * log_gamma[None, :])
    chunk_decay = tuple(float(x) for x in np.exp(cl * log_gamma))
    rep = lambda a: jnp.asarray(np.repeat(a, R_HD, axis=1), F32)
    return cos_t, sin_t, jnp.asarray(dmat, F32), rep(qdec), rep(kdec), chunk_decay


def _retention(p3d, s0_all, layer, pos0, *, bb):
    bsz, length, _ = p3d.shape
    cl = min(R_CHUNK, length)
    n_c = length // cl
    cos_t, sin_t, dmat, qdec, kdec, chunk_decay = _ret_tables(cl, pos0, length)

    def col(j):
        return pl.BlockSpec((bb, cl, W_GROUP), lambda b, c: (b, c, j))

    tab = pl.BlockSpec((cl, R_HD), lambda b, c: (c, 0))
    state = pl.BlockSpec((bb, R_HEADS, R_HD, R_HD), lambda b, c: (b, 0, 0, 0))
    return pl.pallas_call(
        functools.partial(_ret_body, bb=bb, n_c=n_c, chunk_decay=chunk_decay),
        grid=(bsz // bb, n_c),
        in_specs=[col(COL_R_Q), col(COL_R_K), col(COL_R_V), col(COL_R_G), tab, tab,
                  pl.BlockSpec(dmat.shape, lambda b, c: (0, 0, 0)),
                  pl.BlockSpec(qdec.shape, lambda b, c: (0, 0)),
                  pl.BlockSpec(kdec.shape, lambda b, c: (0, 0)),
                  pl.BlockSpec((None, bb, R_HEADS, R_HD, R_HD), lambda b, c: (layer, b, 0, 0, 0))],
        out_specs=[pl.BlockSpec((bb, cl, W_GROUP), lambda b, c: (b, c, 0)), state],
        out_shape=[jax.ShapeDtypeStruct((bsz, length, W_GROUP), BF16),
                   jax.ShapeDtypeStruct(s0_all.shape[1:], F32)],
        scratch_shapes=[pltpu.VMEM((bb, R_HEADS, R_HD, R_HD), F32)],
        compiler_params=_cparams(("parallel", "arbitrary")),
        name="retention",
    )(p3d, p3d, p3d, p3d, cos_t, sin_t, dmat, qdec, kdec, s0_all)


def _conv_body(bg_ref, cg_ref, h_ref, buf_ref, w_ref, y_ref, st_ref, carry_ref, *, bb, tt):
    @pl.when(pl.program_id(1) == 0)
    def _():
        carry_ref[...] = buf_ref[...]

    shape = (bb, tt, W_GROUP)
    z = cg_ref[...] * h_ref[...]
    z2 = z.reshape(bb * tt, W_GROUP)
    r1 = pltpu.roll(z2, 1, axis=0).reshape(shape)
    r2 = pltpu.roll(z2, 2, axis=0).reshape(shape)
    tpos = lax.broadcasted_iota(jnp.int32, shape, 1)
    c0 = carry_ref[:, 0:1, :]
    c1 = carry_ref[:, 1:2, :]
    zm1 = jnp.where(tpos == 0, c1, r1)
    zm2 = jnp.where(tpos == 0, c0, jnp.where(tpos == 1, c1, r2))
    y = w_ref[0:1, :] * zm2 + w_ref[1:2, :] * zm1 + w_ref[2:3, :] * z
    y_ref[...] = (bg_ref[...] * y).astype(BF16)
    new = cg_ref[:, tt - 2:tt, :] * h_ref[:, tt - 2:tt, :]
    carry_ref[...] = new
    st_ref[...] = new


def _conv(p3d, buf, w, *, bb, tt):
    bsz, length, _ = p3d.shape

    def col(j):
        return pl.BlockSpec((bb, tt, W_GROUP), lambda b, t: (b, t, j))

    state = pl.BlockSpec((bb, C_WIDTH - 1, W_GROUP), lambda b, t: (b, 0, 0))
    return pl.pallas_call(
        functools.partial(_conv_body, bb=bb, tt=tt),
        grid=(bsz // bb, length // tt),
        in_specs=[col(COL_C_B), col(COL_C_C), col(COL_C_H), state,
                  pl.BlockSpec(w.shape, lambda b, t: (0, 0))],
        out_specs=[pl.BlockSpec((bb, tt, W_GROUP), lambda b, t: (b, t, 0)), state],
        out_shape=[jax.ShapeDtypeStruct((bsz, length, W_GROUP), BF16),
                   jax.ShapeDtypeStruct(buf.shape, F32)],
        scratch_shapes=[pltpu.VMEM((bb, C_WIDTH - 1, W_GROUP), F32)],
        compiler_params=_cparams(("parallel", "arbitrary")),
        name="short_conv",
    )(p3d, p3d, p3d, buf, w)


def _group_dot(xb, ones_bd):
    wb = ones_bd.shape[0]
    return jnp.concatenate(
        [jnp.dot(xb[:, i * wb:(i + 1) * wb], ones_bd, preferred_element_type=F32)
         for i in range(W_GROUP // wb)], axis=1)


def _head_sum(x, ones_bd):
    hi = x.astype(BF16)
    lo = (x - hi.astype(F32)).astype(BF16)
    return _group_dot(hi, ones_bd) + _group_dot(lo, ones_bd)


def _rwkv_body(r_ref, k_ref, v_ref, lo_ref, sh_ref, s0_ref,
               mu_ref, mulo_ref, w0_ref, w2_ref, a0_ref, a2_ref, g2_ref, kk_ref, ka_ref, rk_ref,
               lnw_ref, lnb_ref, ones_ref,
               y_ref, so_ref,
               s_ref, carry_ref, carrylo_ref, rs, ws, ks, vs, kks, kas, ys, *, bb, tt, n_t):
    tb = pl.program_id(1)
    n = bb * tt
    w3 = (bb, tt, W_GROUP)
    ones_bd = ones_ref[...]

    @pl.when(tb == 0)
    def _():
        s_ref[...] = s0_ref[...]
        carry_ref[...] = sh_ref[:, :, 0:3 * W_GROUP]
        carrylo_ref[...] = sh_ref[:, :, 3 * W_GROUP:K_COLS]

    tpos = lax.broadcasted_iota(jnp.int32, w3, 1)
    tpos_lo = lax.broadcasted_iota(jnp.int32, (bb, tt, LORA_COLS), 1)

    def shifted(x, carry, mu, mask):
        prev = pltpu.roll(x.reshape(n, x.shape[-1]), 1, axis=0).reshape(x.shape)
        prev = jnp.where(mask == 0, carry, prev)
        return (x + (prev - x) * mu).reshape(n, x.shape[-1])

    r_in, k_in, v_in, lo_in = r_ref[...], k_ref[...], v_ref[...], lo_ref[...]
    r = shifted(r_in, carry_ref[:, :, 0:W_GROUP], mu_ref[:, 0:W_GROUP], tpos)
    k = shifted(k_in, carry_ref[:, :, W_GROUP:2 * W_GROUP], mu_ref[:, W_GROUP:2 * W_GROUP], tpos)
    v = shifted(v_in, carry_ref[:, :, 2 * W_GROUP:3 * W_GROUP], mu_ref[:, 2 * W_GROUP:3 * W_GROUP], tpos)
    lo = shifted(lo_in, carrylo_ref[...], mulo_ref[...], tpos_lo)
    carry_ref[:, :, 0:W_GROUP] = r_ref[:, tt - 1:tt, :]
    carry_ref[:, :, W_GROUP:2 * W_GROUP] = k_ref[:, tt - 1:tt, :]
    carry_ref[:, :, 2 * W_GROUP:3 * W_GROUP] = v_ref[:, tt - 1:tt, :]
    carrylo_ref[...] = lo_ref[:, tt - 1:tt, :]

    zw = w0_ref[...] + jnp.dot(jnp.tanh(lo).astype(BF16), w2_ref[...], preferred_element_type=F32)
    nz = -zw
    softplus = jnp.maximum(nz, 0.0) + jnp.log1p(jnp.exp(-jnp.abs(nz)))
    decay = jnp.exp(-jnp.exp(-softplus - 0.5))
    a = jax.nn.sigmoid(a0_ref[...] + jnp.dot(lo.astype(BF16), a2_ref[...], preferred_element_type=F32))
    gate = jnp.dot(jax.nn.sigmoid(lo).astype(BF16), g2_ref[...], preferred_element_type=F32)
    kk = k * kk_ref[...]
    kk = kk / jnp.maximum(jnp.sqrt(_head_sum(kk * kk, ones_bd)), 1e-12)
    k2 = k * (1.0 + (a - 1.0) * ka_ref[...])
    bonus = _head_sum(r * k2 * rk_ref[...], ones_bd) * v

    rs[...] = r.reshape(w3)
    ws[...] = decay.reshape(w3)
    ks[...] = k2.reshape(w3)
    vs[...] = v.reshape(w3)
    kks[...] = kk.reshape(w3)
    kas[...] = (kk * a).reshape(w3)

    gsz = RWKV_GROUP
    g3 = (gsz, K_HD, W_GROUP)
    diag = (lax.broadcasted_iota(jnp.int32, g3, 1)
            == lax.broadcasted_iota(jnp.int32, g3, 2) % K_HD)

    for g0 in range(0, bb, gsz):
        rows = pl.ds(g0, gsz)

        def step(t, carry):
            at = pl.ds(t, 1)
            s = s_ref[rows]
            sk = _head_sum((s * kks[rows, at, :]).reshape(gsz * K_HD, W_GROUP), ones_bd).reshape(g3)
            vdiag = jnp.where(diag, vs[rows, at, :], 0.0).astype(BF16).reshape(gsz * K_HD, W_GROUP)
            vcol = _group_dot(vdiag, ones_bd).reshape(g3)
            s = s * ws[rows, at, :] - sk * kas[rows, at, :] + vcol * ks[rows, at, :]
            s_ref[rows] = s
            sr = (s * rs[rows, at, :]).astype(BF16).reshape(gsz * K_HD, W_GROUP)
            ybc = _group_dot(sr, ones_bd).reshape(g3)
            ys[rows, at, :] = jnp.sum(jnp.where(diag, ybc, 0.0), axis=1, keepdims=True)
            return carry

        lax.fori_loop(0, tt, step, 0, unroll=2)

    y = ys[...].reshape(n, W_GROUP)
    mean = _head_sum(y, ones_bd) * (1.0 / K_HD)
    yc = y - mean
    var = _head_sum(yc * yc, ones_bd) * (1.0 / K_HD)
    out = (yc * lax.rsqrt(var + GN_EPS) * lnw_ref[...] + lnb_ref[...] + bonus) * gate
    y_ref[...] = out.reshape(w3).astype(BF16)

    @pl.when(tb == n_t - 1)
    def _():
        so_ref[...] = s_ref[...]


def _rwkv(p3d, shift, s0, prm, *, bb, tt):
    bsz, length, _ = p3d.shape
    n_t = length // tt

    def col(j):
        return pl.BlockSpec((bb, tt, W_GROUP), lambda b, t: (b, t, j))

    def whole(a):
        nd = a.ndim
        return pl.BlockSpec(a.shape, lambda b, t: (0,) * nd)

    state = pl.BlockSpec((bb, K_HD, W_GROUP), lambda b, t: (b, 0, 0))
    params = [prm[nm] for nm in ("mu", "mu_lo", "w0", "w2", "a0", "a2", "g2", "k_k", "k_a", "r_k",
                                 "ln_w", "ln_b", "ones_bd")]
    blk = pltpu.VMEM((bb, tt, W_GROUP), F32)
    return pl.pallas_call(
        functools.partial(_rwkv_body, bb=bb, tt=tt, n_t=n_t),
        grid=(bsz // bb, n_t),
        in_specs=[col(COL_K_R), col(COL_K_K), col(COL_K_V),
                  pl.BlockSpec((bb, tt, LORA_COLS), lambda b, t: (b, t, COL_K_LORA)),
                  pl.BlockSpec((bb, 1, K_COLS), lambda b, t: (b, 0, 0)),
                  state] + [whole(a) for a in params],
        out_specs=[pl.BlockSpec((bb, tt, W_GROUP), lambda b, t: (b, t, 0)), state],
        out_shape=[jax.ShapeDtypeStruct((bsz, length, W_GROUP), BF16),
                   jax.ShapeDtypeStruct(s0.shape, F32)],
        scratch_shapes=[pltpu.VMEM((bb, K_HD, W_GROUP), F32),
                        pltpu.VMEM((bb, 1, 3 * W_GROUP), F32),
                        pltpu.VMEM((bb, 1, LORA_COLS), F32),
                        blk, blk, blk, blk, blk, blk, blk],
        compiler_params=_cparams(("parallel", "arbitrary")),
        name="rwkv7",
    )(p3d, p3d, p3d, p3d, shift, s0, *params)


def _rwkv_state_in(s):
    b = s.shape[0]
    return jnp.transpose(s, (0, 2, 1, 3)).reshape(b, K_HD, W_GROUP)


def _rwkv_state_out(s):
    b = s.shape[0]
    return jnp.transpose(s.reshape(b, K_HD, K_HEADS, K_HD), (0, 2, 1, 3))


def _gate_mixing(w_s, b_s, seq):
    cl = min(A_CHUNK, seq)
    wm = jnp.tril(w_s[:, :cl, :cl])
    bias = b_s[:, :cl]
    rep = A_CHUNK // cl
    if rep > 1:
        eye = jnp.eye(rep, dtype=w_s.dtype)
        wm = jnp.einsum("ab,hts->hatbs", eye, wm).reshape(w_s.shape[0], A_CHUNK, A_CHUNK)
        bias = jnp.tile(bias, (1, rep))
    bias = jnp.repeat(bias.T, A_CHUNK, axis=1)
    return wm.astype(BF16), bias


def _pad_rows(w, row0):
    return jnp.zeros((LORA_COLS, W_GROUP), F32).at[row0:row0 + w.shape[0]].set(w).astype(BF16)


def _layer_params(l, ffn1_norm, ffn1_w_gate, ffn1_w_up, ffn1_w_down, mix_norm, w_in, w_out,
                  a_w_s, a_b_s, a_ln_g, a_ln_b, c_conv_w,
                  k_mu, k_w0, k_w2, k_a0, k_a2, k_g2, k_k_k, k_k_a, k_r_k, k_ln_w, k_ln_b,
                  ffn2_norm, ffn2_w_gate, ffn2_w_up, ffn2_w_down):
    row = lambda a: a[l].reshape(1, -1)
    head_of = np.arange(V7X_MXU_DIM) // K_HD
    ones_bd = jnp.asarray(head_of[:, None] == head_of[None, :], BF16)
    rwkv = dict(
        mu=k_mu[l][None, :3 * W_GROUP], mu_lo=k_mu[l][None, 3 * W_GROUP:],
        w0=row(k_w0), w2=_pad_rows(k_w2[l], 0),
        a0=row(k_a0), a2=_pad_rows(k_a2[l], W_LORA),
        g2=_pad_rows(k_g2[l], W_LORA + A_LORA),
        k_k=row(k_k_k), k_a=row(k_k_a), r_k=row(k_r_k), ln_w=row(k_ln_w), ln_b=row(k_ln_b),
        ones_bd=ones_bd)
    return dict(
        layer=l,
        ffn1=(row(ffn1_norm), ffn1_w_gate, ffn1_w_up, ffn1_w_down),
        ffn2=(row(ffn2_norm), ffn2_w_gate, ffn2_w_up, ffn2_w_down),
        mix_norm=row(mix_norm), w_in=w_in, w_out=w_out,
        a_w_s=a_w_s[l], a_b_s=a_b_s[l], a_ln_g=row(a_ln_g), a_ln_b=row(a_ln_b),
        conv_w=c_conv_w[l], rwkv=rwkv)


def _stream_layer(x2d, bsz, length, pos0, ret_s0_all, ret_layer, conv_buf, rw_shift, rw_s0, p, fn, *, final, cfg):
    layer = p["layer"]
    x1 = _ffn(x2d, *p["ffn1"], fn, layer=layer, final=False, tm=cfg["tm"], tf=cfg["tf"])
    proj = _proj(x1, p["mix_norm"], p["w_in"], layer=layer, tm=cfg["tm"], tn=cfg["tn"])
    p3d = proj.reshape(bsz, length, proj.shape[1])

    wm, bias = _gate_mixing(p["a_w_s"], p["a_b_s"], length)
    ya, v_rows = _gate(proj, wm, bias, p["a_ln_g"], p["a_ln_b"], tm=cfg["gate_tm"],
                       with_rows=cfg["with_rows"])
    yb, ret_s = _retention(p3d, ret_s0_all, ret_layer, pos0, bb=cfg["ret_bb"])
    yc, conv_new = _conv(p3d, conv_buf, p["conv_w"], bb=cfg["conv_bb"], tt=cfg["conv_tt"])
    yd, rw_s = _rwkv(p3d, rw_shift[:, None, :], _rwkv_state_in(rw_s0), p["rwkv"],
                     bb=cfg["rwkv_bb"], tt=cfg["rwkv_tt"])
    shift_new = p3d[:, length - 1, K_COL0:K_COL0 + K_COLS]

    flat = lambda y: y.reshape(bsz * length, W_GROUP)
    x2 = _outproj(x1, ya, flat(yb), flat(yc), flat(yd), p["w_out"], layer=layer, tm=cfg["tm"])
    x3 = _ffn(x2, *p["ffn2"], fn, layer=layer, final=final, tm=cfg["tm"], tf=cfg["tf"])
    return x3, ret_s, conv_new, shift_new, _rwkv_state_out(rw_s), v_rows


def _stream_cfg(bsz, length, sample):
    m = bsz * length
    tm = min(512, m)
    if sample:
        return dict(tm=tm, tf=512, tn=640, gate_tm=min(512, m), with_rows=True,
                    ret_bb=8, conv_bb=32, conv_tt=length, rwkv_bb=16, rwkv_tt=length)
    return dict(tm=tm, tf=512, tn=640, gate_tm=min(512, m), with_rows=False,
                ret_bb=1, conv_bb=1, conv_tt=min(512, length), rwkv_bb=bsz, rwkv_tt=min(128, length))


def kernel(x_prompt, x_sample, state_ret, state_conv, state_rwkv_shift, state_rwkv, ffn1_norm, ffn1_w_gate, ffn1_w_up, ffn1_w_down, mix_norm, w_in, w_out, a_w_s, a_b_s, a_ln_g, a_ln_b, c_conv_w, k_mu, k_w0, k_w2, k_a0, k_a2, k_g2, k_k_k, k_k_a, k_r_k, k_ln_w, k_ln_b, ffn2_norm, ffn2_w_gate, ffn2_w_up, ffn2_w_down, final_norm):
    bp, lp, d = x_prompt.shape
    bs, ls, _ = x_sample.shape
    depth = ffn1_norm.shape[0]
    cfg_p = _stream_cfg(bp, lp, sample=False)
    cfg_s = _stream_cfg(bs, ls, sample=True)
    fn = final_norm.reshape(1, d)

    zero_ret = jnp.zeros((1, bp, R_HEADS, R_HD, R_HD), F32)
    zero_conv = jnp.zeros((bp, C_WIDTH - 1, W_GROUP), F32)
    zero_shift = jnp.zeros((bp, K_COLS), F32)
    zero_rw = jnp.zeros((bp, K_HEADS, K_HD, K_HD), F32)

    ffn1_w_gate, ffn1_w_up, ffn1_w_down, w_in, w_out, ffn2_w_gate, ffn2_w_up, ffn2_w_down = (
        w.astype(BF16) for w in (ffn1_w_gate, ffn1_w_up, ffn1_w_down, w_in, w_out,
                                 ffn2_w_gate, ffn2_w_up, ffn2_w_down))
    xp = x_prompt.reshape(bp * lp, d)
    xs = x_sample.reshape(bs * ls, d)
    outs = [[] for _ in range(9)]
    for l in range(depth):
        p = _layer_params(l, ffn1_norm, ffn1_w_gate, ffn1_w_up, ffn1_w_down, mix_norm, w_in, w_out,
                          a_w_s, a_b_s, a_ln_g, a_ln_b, c_conv_w,
                          k_mu, k_w0, k_w2, k_a0, k_a2, k_g2, k_k_k, k_k_a, k_r_k, k_ln_w, k_ln_b,
                          ffn2_norm, ffn2_w_gate, ffn2_w_up, ffn2_w_down)
        final = l == depth - 1
        xp, rp, cp, sp, wp, _ = _stream_layer(xp, bp, lp, 0.0, zero_ret, 0, zero_conv, zero_shift, zero_rw,
                                              p, fn, final=final, cfg=cfg_p)
        xs, rs, cs, ss, ws, vs = _stream_layer(xs, bs, ls, float(PAST_LEN), state_ret, l, state_conv[l],
                                               state_rwkv_shift[l], state_rwkv[l], p, fn,
                                               final=final, cfg=cfg_s)
        for acc, val in zip(outs, (rp, rs, cp, cs, sp, ss, wp, ws, vs.reshape(bs, ls, W_GROUP))):
            acc.append(val)

    return (xp.reshape(bp, lp, d), xs.reshape(bs, ls, d)) + tuple(jnp.stack(o) for o in outs)
```

```python
import functools
import math

import numpy as np
import jax
import jax.numpy as jnp
from jax import lax
from jax.experimental import pallas as pl
from jax.experimental.pallas import tpu as pltpu

F32 = jnp.float32
BF16 = jnp.bfloat16

W_GROUP = 512
A_CHUNK = 128
R_HEADS = 4
R_HD = 128
R_CHUNK = 128
ROPE_BASE = 10000.0
C_WIDTH = 3
K_HD = 64
K_HEADS = 8
W_LORA = 64
A_LORA = 64
G_LORA = 128
LORA_COLS = W_LORA + A_LORA + G_LORA
K_COLS = 3 * W_GROUP + LORA_COLS
EPS = 1e-6
GN_EPS = 64e-5
PAST_LEN = 16384

COL_A_U, COL_A_V = 0, 1
COL_R_Q, COL_R_K, COL_R_V, COL_R_G = 2, 3, 4, 5
COL_C_B, COL_C_C, COL_C_H = 6, 7, 8
COL_K_R, COL_K_K, COL_K_V = 9, 10, 11
COL_K_LORA = (12 * W_GROUP) // LORA_COLS
K_COL0 = 9 * W_GROUP

V7X_VMEM_LIMIT_BYTES = 56 * 1024 * 1024
V7X_MXU_DIM = 256


def _cparams(sem, vmem=V7X_VMEM_LIMIT_BYTES):
    return pltpu.CompilerParams(dimension_semantics=sem, vmem_limit_bytes=vmem)


def _rms(x, w):
    return x * lax.rsqrt(jnp.mean(x * x, axis=-1, keepdims=True) + EPS) * w


def _ffn_body(x_ref, nw_ref, wg_ref, wu_ref, wd_ref, fn_ref, o_ref, hn_ref, *, n_f, final):
    j = pl.program_id(1)

    @pl.when(j == 0)
    def _():
        hn_ref[...] = _rms(x_ref[...], nw_ref[...]).astype(BF16)
        o_ref[...] = jnp.zeros_like(o_ref)

    h = hn_ref[...]
    g = jnp.dot(h, wg_ref[...], preferred_element_type=F32)
    u = jnp.dot(h, wu_ref[...], preferred_element_type=F32)
    a = (g * jax.nn.sigmoid(g) * u).astype(BF16)
    o_ref[...] += jnp.dot(a, wd_ref[...], preferred_element_type=F32)

    @pl.when(j == n_f - 1)
    def _():
        y = x_ref[...] + 0.5 * o_ref[...]
        if final:
            y = _rms(y, fn_ref[...])
        o_ref[...] = y


def _ffn(x, nw, wg, wu, wd, fn, *, layer, final, tm, tf):
    m, d = x.shape
    f = wg.shape[2]
    n_f = f // tf
    return pl.pallas_call(
        functools.partial(_ffn_body, n_f=n_f, final=final),
        grid=(m // tm, n_f),
        in_specs=[
            pl.BlockSpec((tm, d), lambda i, j: (i, 0)),
            pl.BlockSpec((1, d), lambda i, j: (0, 0)),
            pl.BlockSpec((None, d, tf), lambda i, j: (layer, 0, j)),
            pl.BlockSpec((None, d, tf), lambda i, j: (layer, 0, j)),
            pl.BlockSpec((None, tf, d), lambda i, j: (layer, j, 0)),
            pl.BlockSpec((1, d), lambda i, j: (0, 0)),
        ],
        out_specs=pl.BlockSpec((tm, d), lambda i, j: (i, 0)),
        out_shape=jax.ShapeDtypeStruct((m, d), F32),
        scratch_shapes=[pltpu.VMEM((tm, d), BF16)],
        compiler_params=_cparams(("parallel", "arbitrary")),
        name="ffn_final" if final else "ffn",
    )(x, nw, wg, wu, wd, fn)


def _proj_body(x_ref, nw_ref, w_ref, o_ref, hn_ref):
    @pl.when(pl.program_id(1) == 0)
    def _():
        hn_ref[...] = _rms(x_ref[...], nw_ref[...]).astype(BF16)

    o_ref[...] = jnp.dot(hn_ref[...], w_ref[...], preferred_element_type=F32)


def _proj(x, nw, w, *, layer, tm, tn):
    m, d = x.shape
    n = w.shape[2]
    return pl.pallas_call(
        _proj_body,
        grid=(m // tm, n // tn),
        in_specs=[
            pl.BlockSpec((tm, d), lambda i, j: (i, 0)),
            pl.BlockSpec((1, d), lambda i, j: (0, 0)),
            pl.BlockSpec((None, d, tn), lambda i, j: (layer, 0, j)),
        ],
        out_specs=pl.BlockSpec((tm, tn), lambda i, j: (i, j)),
        out_shape=jax.ShapeDtypeStruct((m, n), F32),
        scratch_shapes=[pltpu.VMEM((tm, d), BF16)],
        compiler_params=_cparams(("parallel", "arbitrary")),
        name="in_proj",
    )(x, nw, w)


def _outproj_body(x_ref, ya_ref, yb_ref, yc_ref, yd_ref, w_ref, o_ref):
    acc = x_ref[...]
    for gi, y_ref in enumerate((ya_ref, yb_ref, yc_ref, yd_ref)):
        acc = acc + jnp.dot(y_ref[...], w_ref[gi * W_GROUP:(gi + 1) * W_GROUP, :],
                            preferred_element_type=F32)
    o_ref[...] = acc


def _outproj(x, ya, yb, yc, yd, w, *, layer, tm):
    m, d = x.shape
    yspec = pl.BlockSpec((tm, W_GROUP), lambda i: (i, 0))
    return pl.pallas_call(
        _outproj_body,
        grid=(m // tm,),
        in_specs=[pl.BlockSpec((tm, d), lambda i: (i, 0)), yspec, yspec, yspec, yspec,
                  pl.BlockSpec((None,) + w.shape[1:], lambda i: (layer, 0, 0))],
        out_specs=pl.BlockSpec((tm, d), lambda i: (i, 0)),
        out_shape=jax.ShapeDtypeStruct((m, d), F32),
        compiler_params=_cparams(("parallel",)),
        name="out_proj",
    )(x, ya, yb, yc, yd, w)


def _gate_body(u_ref, v_ref, wm_ref, bias_ref, g_ref, b_ref, y_ref, *vr_ref, tm):
    gu = jax.nn.gelu(u_ref[...], approximate=True)
    gv = jax.nn.gelu(v_ref[...], approximate=True)
    mu = jnp.mean(gv, axis=-1, keepdims=True)
    var = jnp.mean(jnp.square(gv - mu), axis=-1, keepdims=True)
    vn = (gv - mu) * lax.rsqrt(var + EPS) * g_ref[...] + b_ref[...]
    if vr_ref:
        vr_ref[0][...] = vn
    vnb = vn.astype(BF16)
    for c in range(tm // A_CHUNK):
        rows = slice(c * A_CHUNK, (c + 1) * A_CHUNK)
        for h in range(W_GROUP // A_CHUNK):
            cols = slice(h * A_CHUNK, (h + 1) * A_CHUNK)
            z = jnp.dot(wm_ref[h], vnb[rows, cols], preferred_element_type=F32) + bias_ref[:, cols]
            y_ref[rows, cols] = (gu[rows, cols] * z).astype(BF16)


def _gate(p2d, wm, bias, ln_g, ln_b, *, tm, with_rows):
    m = p2d.shape[0]
    row_spec = pl.BlockSpec((tm, W_GROUP), lambda i: (i, 0))
    out_shape = [jax.ShapeDtypeStruct((m, W_GROUP), BF16)]
    out_specs = [row_spec]
    if with_rows:
        out_shape.append(jax.ShapeDtypeStruct((m, W_GROUP), F32))
        out_specs.append(row_spec)
    res = pl.pallas_call(
        functools.partial(_gate_body, tm=tm),
        grid=(m // tm,),
        in_specs=[
            pl.BlockSpec((tm, W_GROUP), lambda i: (i, COL_A_U)),
            pl.BlockSpec((tm, W_GROUP), lambda i: (i, COL_A_V)),
            pl.BlockSpec(wm.shape, lambda i: (0, 0, 0)),
            pl.BlockSpec(bias.shape, lambda i: (0, 0)),
            pl.BlockSpec((1, W_GROUP), lambda i: (0, 0)),
            pl.BlockSpec((1, W_GROUP), lambda i: (0, 0)),
        ],
        out_specs=out_specs,
        out_shape=out_shape,
        compiler_params=_cparams(("parallel",)),
        name="spatial_gate",
    )(p2d, p2d, wm, bias, ln_g, ln_b)
    return res if with_rows else (res[0], None)


def _ret_body(q_ref, k_ref, v_ref, g_ref, cos_ref, sin_ref, dm_ref, qd_ref, kd_ref, s0_ref,
              y_ref, so_ref, s_ref, *, bb, n_c, chunk_decay):
    c = pl.program_id(1)

    @pl.when(c == 0)
    def _():
        s_ref[...] = s0_ref[...]

    cos = cos_ref[...]
    sin = sin_ref[...]
    nt = (((1,), (1,)), ((), ()))
    tn = (((0,), (0,)), ((), ()))
    for b in range(bb):
        for h in range(R_HEADS):
            cols = slice(h * R_HD, (h + 1) * R_HD)
            q = q_ref[b, :, cols]
            k = k_ref[b, :, cols]
            v = v_ref[b, :, cols].astype(BF16)
            qr = q * cos + pltpu.roll(q, R_HD // 2, axis=1) * sin
            kr = (k * cos + pltpu.roll(k, R_HD // 2, axis=1) * sin) * (R_HD ** -0.5)
            sc = lax.dot_general(qr.astype(BF16), kr.astype(BF16), nt,
                                 preferred_element_type=F32) * dm_ref[h]
            s = s_ref[b, h]
            o = jnp.dot(sc.astype(BF16), v, preferred_element_type=F32)
            o = o + jnp.dot((qr * qd_ref[:, cols]).astype(BF16), s.astype(BF16),
                            preferred_element_type=F32)
            kv = lax.dot_general((kr * kd_ref[:, cols]).astype(BF16), v, tn,
                                 preferred_element_type=F32)
            s_ref[b, h] = chunk_decay[h] * s + kv
            o = o * lax.rsqrt(jnp.mean(o * o, axis=-1, keepdims=True) + EPS)
            g = g_ref[b, :, cols]
            y_ref[b, :, cols] = (o * (g * jax.nn.sigmoid(g))).astype(BF16)

    @pl.when(c == n_c - 1)
    def _():
        so_ref[...] = s_ref[...]


def _ret_tables(cl, pos0, length):
    half = R_HD // 2
    inv = ROPE_BASE ** (-jnp.arange(half, dtype=F32) / half)
    pos = pos0 + jnp.arange(length, dtype=F32)
    ang = pos[:, None] * inv[None, :]
    cos = jnp.cos(ang)
    sin = jnp.sin(ang)
    cos_t = jnp.concatenate([cos, cos], axis=-1)
    sin_t = jnp.concatenate([-sin, sin], axis=-1)
    log_gamma = np.log(1.0 - 2.0 ** (-5.0 - np.arange(R_HEADS, dtype=np.float64)))
    idx = np.arange(cl, dtype=np.float64)
    diff = idx[:, None] - idx[None, :]
    dmat = np.where(diff >= 0, np.exp(np.maximum(diff, 0.0)[None] * log_gamma[:, None, None]), 0.0)
    kdec = np.exp((cl - 1.0 - idx)[:, None] * log_gamma[None, :])
    qdec = np.exp((idx + 1.0)[:, None] * log_gamma[None, :])
    chunk_decay = tuple(float(x) for x in np.exp(cl * log_gamma))
    rep = lambda a: jnp.asarray(np.repeat(a, R_HD, axis=1), F32)
    return cos_t, sin_t, jnp.asarray(dmat, F32), rep(qdec), rep(kdec), chunk_decay


def _retention(p3d, s0_all, layer, pos0, *, bb):
    bsz, length, _ = p3d.shape
    cl = min(R_CHUNK, length)
    n_c = length // cl
    cos_t, sin_t, dmat, qdec, kdec, chunk_decay = _ret_tables(cl, pos0, length)

    def col(j):
        return pl.BlockSpec((bb, cl, W_GROUP), lambda b, c: (b, c, j))

    tab = pl.BlockSpec((cl, R_HD), lambda b, c: (c, 0))
    state = pl.BlockSpec((bb, R_HEADS, R_HD, R_HD), lambda b, c: (b, 0, 0, 0))
    return pl.pallas_call(
        functools.partial(_ret_body, bb=bb, n_c=n_c, chunk_decay=chunk_decay),
        grid=(bsz // bb, n_c),
        in_specs=[col(COL_R_Q), col(COL_R_K), col(COL_R_V), col(COL_R_G), tab, tab,
                  pl.BlockSpec(dmat.shape, lambda b, c: (0, 0, 0)),
                  pl.BlockSpec(qdec.shape, lambda b, c: (0, 0)),
                  pl.BlockSpec(kdec.shape, lambda b, c: (0, 0)),
                  pl.BlockSpec((None, bb, R_HEADS, R_HD, R_HD), lambda b, c: (layer, b, 0, 0, 0))],
        out_specs=[pl.BlockSpec((bb, cl, W_GROUP), lambda b, c: (b, c, 0)), state],
        out_shape=[jax.ShapeDtypeStruct((bsz, length, W_GROUP), BF16),
                   jax.ShapeDtypeStruct(s0_all.shape[1:], F32)],
        scratch_shapes=[pltpu.VMEM((bb, R_HEADS, R_HD, R_HD), F32)],
        compiler_params=_cparams(("parallel", "arbitrary")),
        name="retention",
    )(p3d, p3d, p3d, p3d, cos_t, sin_t, dmat, qdec, kdec, s0_all)


def _conv_body(bg_ref, cg_ref, h_ref, buf_ref, w_ref, y_ref, st_ref, carry_ref, *, bb, tt):
    @pl.when(pl.program_id(1) == 0)
    def _():
        carry_ref[...] = buf_ref[...]

    shape = (bb, tt, W_GROUP)
    z = cg_ref[...] * h_ref[...]
    z2 = z.reshape(bb * tt, W_GROUP)
    r1 = pltpu.roll(z2, 1, axis=0).reshape(shape)
    r2 = pltpu.roll(z2, 2, axis=0).reshape(shape)
    tpos = lax.broadcasted_iota(jnp.int32, shape, 1)
    c0 = carry_ref[:, 0:1, :]
    c1 = carry_ref[:, 1:2, :]
    zm1 = jnp.where(tpos == 0, c1, r1)
    zm2 = jnp.where(tpos == 0, c0, jnp.where(tpos == 1, c1, r2))
    y = w_ref[0:1, :] * zm2 + w_ref[1:2, :] * zm1 + w_ref[2:3, :] * z
    y_ref[...] = (bg_ref[...] * y).astype(BF16)
    new = cg_ref[:, tt - 2:tt, :] * h_ref[:, tt - 2:tt, :]
    carry_ref[...] = new
    st_ref[...] = new


def _conv(p3d, buf, w, *, bb, tt):
    bsz, length, _ = p3d.shape

    def col(j):
        return pl.BlockSpec((bb, tt, W_GROUP), lambda b, t: (b, t, j))

    state = pl.BlockSpec((bb, C_WIDTH - 1, W_GROUP), lambda b, t: (b, 0, 0))
    return pl.pallas_call(
        functools.partial(_conv_body, bb=bb, tt=tt),
        grid=(bsz // bb, length // tt),
        in_specs=[col(COL_C_B), col(COL_C_C), col(COL_C_H), state,
                  pl.BlockSpec(w.shape, lambda b, t: (0, 0))],
        out_specs=[pl.BlockSpec((bb, tt, W_GROUP), lambda b, t: (b, t, 0)), state],
        out_shape=[jax.ShapeDtypeStruct((bsz, length, W_GROUP), BF16),
                   jax.ShapeDtypeStruct(buf.shape, F32)],
        scratch_shapes=[pltpu.VMEM((bb, C_WIDTH - 1, W_GROUP), F32)],
        compiler_params=_cparams(("parallel", "arbitrary")),
        name="short_conv",
    )(p3d, p3d, p3d, buf, w)


def _group_dot(xb, ones_bd):
    wb = ones_bd.shape[0]
    return jnp.concatenate(
        [jnp.dot(xb[:, i * wb:(i + 1) * wb], ones_bd, preferred_element_type=F32)
         for i in range(W_GROUP // wb)], axis=1)


def _head_sum(x, ones_bd):
    hi = x.astype(BF16)
    lo = (x - hi.astype(F32)).astype(BF16)
    return _group_dot(hi, ones_bd) + _group_dot(lo, ones_bd)


def _rwkv_body(r_ref, k_ref, v_ref, lo_ref, sh_ref, s0_ref,
               mu_ref, mulo_ref, w0_ref, w2_ref, a0_ref, a2_ref, g2_ref, kk_ref, ka_ref, rk_ref,
               lnw_ref, lnb_ref, ones_ref,
               y_ref, so_ref,
               s_ref, carry_ref, carrylo_ref, rs, ws, ks, vs, kks, kas, ys, *, bb, tt, n_t, gsz, unroll):
    tb = pl.program_id(1)
    n = bb * tt
    w3 = (bb, tt, W_GROUP)
    ones_bd = ones_ref[...]

    @pl.when(tb == 0)
    def _():
        s_ref[...] = s0_ref[...]
        carry_ref[...] = sh_ref[:, :, 0:3 * W_GROUP]
        carrylo_ref[...] = sh_ref[:, :, 3 * W_GROUP:K_COLS]

    tpos = lax.broadcasted_iota(jnp.int32, w3, 1)
    tpos_lo = lax.broadcasted_iota(jnp.int32, (bb, tt, LORA_COLS), 1)

    def shifted(x, carry, mu, mask):
        prev = pltpu.roll(x.reshape(n, x.shape[-1]), 1, axis=0).reshape(x.shape)
        prev = jnp.where(mask == 0, carry, prev)
        return (x + (prev - x) * mu).reshape(n, x.shape[-1])

    r_in, k_in, v_in, lo_in = r_ref[...], k_ref[...], v_ref[...], lo_ref[...]
    r = shifted(r_in, carry_ref[:, :, 0:W_GROUP], mu_ref[:, 0:W_GROUP], tpos)
    k = shifted(k_in, carry_ref[:, :, W_GROUP:2 * W_GROUP], mu_ref[:, W_GROUP:2 * W_GROUP], tpos)
    v = shifted(v_in, carry_ref[:, :, 2 * W_GROUP:3 * W_GROUP], mu_ref[:, 2 * W_GROUP:3 * W_GROUP], tpos)
    lo = shifted(lo_in, carrylo_ref[...], mulo_ref[...], tpos_lo)
    carry_ref[:, :, 0:W_GROUP] = r_ref[:, tt - 1:tt, :]
    carry_ref[:, :, W_GROUP:2 * W_GROUP] = k_ref[:, tt - 1:tt, :]
    carry_ref[:, :, 2 * W_GROUP:3 * W_GROUP] = v_ref[:, tt - 1:tt, :]
    carrylo_ref[...] = lo_ref[:, tt - 1:tt, :]

    zw = w0_ref[...] + jnp.dot(jnp.tanh(lo).astype(BF16), w2_ref[...], preferred_element_type=F32)
    nz = -zw
    softplus = jnp.maximum(nz, 0.0) + jnp.log1p(jnp.exp(-jnp.abs(nz)))
    decay = jnp.exp(-jnp.exp(-softplus - 0.5))
    a = jax.nn.sigmoid(a0_ref[...] + jnp.dot(lo.astype(BF16), a2_ref[...], preferred_element_type=F32))
    gate = jnp.dot(jax.nn.sigmoid(lo).astype(BF16), g2_ref[...], preferred_element_type=F32)
    kk = k * kk_ref[...]
    kk = kk / jnp.maximum(jnp.sqrt(_head_sum(kk * kk, ones_bd)), 1e-12)
    k2 = k * (1.0 + (a - 1.0) * ka_ref[...])
    bonus = _head_sum(r * k2 * rk_ref[...], ones_bd) * v

    rs[...] = r.reshape(w3)
    ws[...] = decay.reshape(w3)
    ks[...] = k2.reshape(w3)
    vs[...] = v.reshape(w3)
    kks[...] = kk.reshape(w3)
    kas[...] = (kk * a).reshape(w3)

    s2 = (K_HD, W_GROUP)
    diag = (lax.broadcasted_iota(jnp.int32, s2, 0)
            == lax.broadcasted_iota(jnp.int32, s2, 1) % K_HD)

    def read_out(ybc):
        return jnp.sum(jnp.where(diag, ybc, 0.0), axis=0, keepdims=True)

    def step(t, carry):
        at = pl.ds(t, 1)
        prev = pl.ds(jnp.maximum(t - 1, 0), 1)
        for g0 in range(0, bb, gsz):
            parts = []
            for b in range(g0, g0 + gsz):
                s = s_ref[b]
                parts += [(s * kks[b, at, :]).astype(BF16),
                          jnp.where(diag, vs[b, at, :], 0.0).astype(BF16),
                          (s * rs[b, prev, :]).astype(BF16)]
            res = _group_dot(jnp.concatenate(parts, axis=0), ones_bd)
            for i, b in enumerate(range(g0, g0 + gsz)):
                r0 = 3 * i * K_HD
                sk = res[r0:r0 + K_HD]
                vcol = res[r0 + K_HD:r0 + 2 * K_HD]
                ys[b, prev, :] = read_out(res[r0 + 2 * K_HD:r0 + 3 * K_HD])
                s_ref[b] = s_ref[b] * ws[b, at, :] - sk * kas[b, at, :] + vcol * ks[b, at, :]
        return carry

    lax.fori_loop(0, tt, step, 0, unroll=unroll)
    last = pl.ds(tt - 1, 1)
    for b in range(bb):
        sr = (s_ref[b] * rs[b, last, :]).astype(BF16)
        ys[b, last, :] = read_out(_group_dot(sr, ones_bd))

    y = ys[...].reshape(n, W_GROUP)
    mean = _head_sum(y, ones_bd) * (1.0 / K_HD)
    yc = y - mean
    var = _head_sum(yc * yc, ones_bd) * (1.0 / K_HD)
    out = (yc * lax.rsqrt(var + GN_EPS) * lnw_ref[...] + lnb_ref[...] + bonus) * gate
    y_ref[...] = out.reshape(w3).astype(BF16)

    @pl.when(tb == n_t - 1)
    def _():
        so_ref[...] = s_ref[...]


def _rwkv(p3d, shift, s0, prm, *, bb, tt, gsz, unroll):
    bsz, length, _ = p3d.shape
    n_t = length // tt

    def col(j):
        return pl.BlockSpec((bb, tt, W_GROUP), lambda b, t: (b, t, j))

    def whole(a):
        nd = a.ndim
        return pl.BlockSpec(a.shape, lambda b, t: (0,) * nd)

    state = pl.BlockSpec((bb, K_HD, W_GROUP), lambda b, t: (b, 0, 0))
    params = [prm[nm] for nm in ("mu", "mu_lo", "w0", "w2", "a0", "a2", "g2", "k_k", "k_a", "r_k",
                                 "ln_w", "ln_b", "ones_bd")]
    blk = pltpu.VMEM((bb, tt, W_GROUP), F32)
    return pl.pallas_call(
        functools.partial(_rwkv_body, bb=bb, tt=tt, n_t=n_t, gsz=gsz, unroll=unroll),
        grid=(bsz // bb, n_t),
        in_specs=[col(COL_K_R), col(COL_K_K), col(COL_K_V),
                  pl.BlockSpec((bb, tt, LORA_COLS), lambda b, t: (b, t, COL_K_LORA)),
                  pl.BlockSpec((bb, 1, K_COLS), lambda b, t: (b, 0, 0)),
                  state] + [whole(a) for a in params],
        out_specs=[pl.BlockSpec((bb, tt, W_GROUP), lambda b, t: (b, t, 0)), state],
        out_shape=[jax.ShapeDtypeStruct((bsz, length, W_GROUP), BF16),
                   jax.ShapeDtypeStruct(s0.shape, F32)],
        scratch_shapes=[pltpu.VMEM((bb, K_HD, W_GROUP), F32),
                        pltpu.VMEM((bb, 1, 3 * W_GROUP), F32),
                        pltpu.VMEM((bb, 1, LORA_COLS), F32),
                        blk, blk, blk, blk, blk, blk, blk],
        compiler_params=_cparams(("parallel", "arbitrary")),
        name="rwkv7",
    )(p3d, p3d, p3d, p3d, shift, s0, *params)


def _rwkv_state_in(s):
    b = s.shape[0]
    return jnp.transpose(s, (0, 2, 1, 3)).reshape(b, K_HD, W_GROUP)


def _rwkv_state_out(s):
    b = s.shape[0]
    return jnp.transpose(s.reshape(b, K_HD, K_HEADS, K_HD), (0, 2, 1, 3))


def _gate_mixing(w_s, b_s, seq):
    cl = min(A_CHUNK, seq)
    wm = jnp.tril(w_s[:, :cl, :cl])
    bias = b_s[:, :cl]
    rep = A_CHUNK // cl
    if rep > 1:
        eye = jnp.eye(rep, dtype=w_s.dtype)
        wm = jnp.einsum("ab,hts->hatbs", eye, wm).reshape(w_s.shape[0], A_CHUNK, A_CHUNK)
        bias = jnp.tile(bias, (1, rep))
    bias = jnp.repeat(bias.T, A_CHUNK, axis=1)
    return wm.astype(BF16), bias


def _pad_rows(w, row0):
    return jnp.zeros((LORA_COLS, W_GROUP), F32).at[row0:row0 + w.shape[0]].set(w).astype(BF16)


def _layer_params(l, ffn1_norm, ffn1_w_gate, ffn1_w_up, ffn1_w_down, mix_norm, w_in, w_out,
                  a_w_s, a_b_s, a_ln_g, a_ln_b, c_conv_w,
                  k_mu, k_w0, k_w2, k_a0, k_a2, k_g2, k_k_k, k_k_a, k_r_k, k_ln_w, k_ln_b,
                  ffn2_norm, ffn2_w_gate, ffn2_w_up, ffn2_w_down):
    row = lambda a: a[l].reshape(1, -1)
    head_of = np.arange(V7X_MXU_DIM) // K_HD
    ones_bd = jnp.asarray(head_of[:, None] == head_of[None, :], BF16)
    rwkv = dict(
        mu=k_mu[l][None, :3 * W_GROUP], mu_lo=k_mu[l][None, 3 * W_GROUP:],
        w0=row(k_w0), w2=_pad_rows(k_w2[l], 0),
        a0=row(k_a0), a2=_pad_rows(k_a2[l], W_LORA),
        g2=_pad_rows(k_g2[l], W_LORA + A_LORA),
        k_k=row(k_k_k), k_a=row(k_k_a), r_k=row(k_r_k), ln_w=row(k_ln_w), ln_b=row(k_ln_b),
        ones_bd=ones_bd)
    return dict(
        layer=l,
        ffn1=(row(ffn1_norm), ffn1_w_gate, ffn1_w_up, ffn1_w_down),
        ffn2=(row(ffn2_norm), ffn2_w_gate, ffn2_w_up, ffn2_w_down),
        mix_norm=row(mix_norm), w_in=w_in, w_out=w_out,
        a_w_s=a_w_s[l], a_b_s=a_b_s[l], a_ln_g=row(a_ln_g), a_ln_b=row(a_ln_b),
        conv_w=c_conv_w[l], rwkv=rwkv)


def _stream_layer(x2d, bsz, length, pos0, ret_s0_all, ret_layer, conv_buf, rw_shift, rw_s0, p, fn, *, final, cfg):
    layer = p["layer"]
    x1 = _ffn(x2d, *p["ffn1"], fn, layer=layer, final=False, tm=cfg["tm"], tf=cfg["tf"])
    proj = _proj(x1, p["mix_norm"], p["w_in"], layer=layer, tm=cfg["tm"], tn=cfg["tn"])
    p3d = proj.reshape(bsz, length, proj.shape[1])

    wm, bias = _gate_mixing(p["a_w_s"], p["a_b_s"], length)
    ya, v_rows = _gate(proj, wm, bias, p["a_ln_g"], p["a_ln_b"], tm=cfg["gate_tm"],
                       with_rows=cfg["with_rows"])
    yb, ret_s = _retention(p3d, ret_s0_all, ret_layer, pos0, bb=cfg["ret_bb"])
    yc, conv_new = _conv(p3d, conv_buf, p["conv_w"], bb=cfg["conv_bb"], tt=cfg["conv_tt"])
    yd, rw_s = _rwkv(p3d, rw_shift[:, None, :], _rwkv_state_in(rw_s0), p["rwkv"],
                     bb=cfg["rwkv_bb"], tt=cfg["rwkv_tt"], gsz=cfg["rwkv_gsz"], unroll=cfg["rwkv_unroll"])
    shift_new = p3d[:, length - 1, K_COL0:K_COL0 + K_COLS]

    flat = lambda y: y.reshape(bsz * length, W_GROUP)
    x2 = _outproj(x1, ya, flat(yb), flat(yc), flat(yd), p["w_out"], layer=layer, tm=cfg["tm"])
    x3 = _ffn(x2, *p["ffn2"], fn, layer=layer, final=final, tm=cfg["tm"], tf=cfg["tf"])
    return x3, ret_s, conv_new, shift_new, _rwkv_state_out(rw_s), v_rows


def _stream_cfg(bsz, length, sample):
    m = bsz * length
    tm = min(512, m)
    if sample:
        return dict(tm=tm, tf=512, tn=1280, gate_tm=min(512, m), with_rows=True,
                    ret_bb=8, conv_bb=32, conv_tt=length, rwkv_bb=16, rwkv_tt=length, rwkv_gsz=4, rwkv_unroll=1)
    return dict(tm=tm, tf=512, tn=1280, gate_tm=min(512, m), with_rows=False,
                ret_bb=1, conv_bb=1, conv_tt=min(512, length), rwkv_bb=bsz, rwkv_tt=min(128, length), rwkv_gsz=4, rwkv_unroll=2)


def kernel(x_prompt, x_sample, state_ret, state_conv, state_rwkv_shift, state_rwkv, ffn1_norm, ffn1_w_gate, ffn1_w_up, ffn1_w_down, mix_norm, w_in, w_out, a_w_s, a_b_s, a_ln_g, a_ln_b, c_conv_w, k_mu, k_w0, k_w2, k_a0, k_a2, k_g2, k_k_k, k_k_a, k_r_k, k_ln_w, k_ln_b, ffn2_norm, ffn2_w_gate, ffn2_w_up, ffn2_w_down, final_norm):
    bp, lp, d = x_prompt.shape
    bs, ls, _ = x_sample.shape
    depth = ffn1_norm.shape[0]
    cfg_p = _stream_cfg(bp, lp, sample=False)
    cfg_s = _stream_cfg(bs, ls, sample=True)
    fn = final_norm.reshape(1, d)

    zero_ret = jnp.zeros((1, bp, R_HEADS, R_HD, R_HD), F32)
    zero_conv = jnp.zeros((bp, C_WIDTH - 1, W_GROUP), F32)
    zero_shift = jnp.zeros((bp, K_COLS), F32)
    zero_rw = jnp.zeros((bp, K_HEADS, K_HD, K_HD), F32)

    ffn1_w_gate, ffn1_w_up, ffn1_w_down, w_in, w_out, ffn2_w_gate, ffn2_w_up, ffn2_w_down = (
        w.astype(BF16) for w in (ffn1_w_gate, ffn1_w_up, ffn1_w_down, w_in, w_out,
                                 ffn2_w_gate, ffn2_w_up, ffn2_w_down))
    xp = x_prompt.reshape(bp * lp, d)
    xs = x_sample.reshape(bs * ls, d)
    outs = [[] for _ in range(9)]
    for l in range(depth):
        p = _layer_params(l, ffn1_norm, ffn1_w_gate, ffn1_w_up, ffn1_w_down, mix_norm, w_in, w_out,
                          a_w_s, a_b_s, a_ln_g, a_ln_b, c_conv_w,
                          k_mu, k_w0, k_w2, k_a0, k_a2, k_g2, k_k_k, k_k_a, k_r_k, k_ln_w, k_ln_b,
                          ffn2_norm, ffn2_w_gate, ffn2_w_up, ffn2_w_down)
        final = l == depth - 1
        xp, rp, cp, sp, wp, _ = _stream_layer(xp, bp, lp, 0.0, zero_ret, 0, zero_conv, zero_shift, zero_rw,
                                              p, fn, final=final, cfg=cfg_p)
        xs, rs, cs, ss, ws, vs = _stream_layer(xs, bs, ls, float(PAST_LEN), state_ret, l, state_conv[l],
                                               state_rwkv_shift[l], state_rwkv[l], p, fn,
                                               final=final, cfg=cfg_s)
        for acc, val in zip(outs, (rp, rs, cp, cs, sp, ss, wp, ws, vs.reshape(bs, ls, W_GROUP))):
            acc.append(val)

    return (xp.reshape(bp, lp, d), xs.reshape(bs, ls, d)) + tuple(jnp.stack(o) for o in outs)
```

```python
import functools
import math

import numpy as np
import jax
import jax.numpy as jnp
from jax import lax
from jax.experimental import pallas as pl
from jax.experimental.pallas import tpu as pltpu

F32 = jnp.float32
BF16 = jnp.bfloat16

W_GROUP = 512
A_CHUNK = 128
R_HEADS = 4
R_HD = 128
R_CHUNK = 128
ROPE_BASE = 10000.0
C_WIDTH = 3
K_HD = 64
K_HEADS = 8
W_LORA = 64
A_LORA = 64
G_LORA = 128
LORA_COLS = W_LORA + A_LORA + G_LORA
K_COLS = 3 * W_GROUP + LORA_COLS
EPS = 1e-6
GN_EPS = 64e-5
PAST_LEN = 16384

COL_A_U, COL_A_V = 0, 1
COL_R_Q, COL_R_K, COL_R_V, COL_R_G = 2, 3, 4, 5
COL_C_B, COL_C_C, COL_C_H = 6, 7, 8
COL_K_R, COL_K_K, COL_K_V = 9, 10, 11
COL_K_LORA = (12 * W_GROUP) // LORA_COLS
K_COL0 = 9 * W_GROUP

V7X_VMEM_LIMIT_BYTES = 56 * 1024 * 1024
V7X_MXU_DIM = 256


def _cparams(sem, vmem=V7X_VMEM_LIMIT_BYTES):
    return pltpu.CompilerParams(dimension_semantics=sem, vmem_limit_bytes=vmem)


def _rms(x, w):
    return x * lax.rsqrt(jnp.mean(x * x, axis=-1, keepdims=True) + EPS) * w


def _ffn_body(x_ref, nw_ref, wg_ref, wu_ref, wd_ref, fn_ref, o_ref, hn_ref, *, n_f, final):
    j = pl.program_id(1)

    @pl.when(j == 0)
    def _():
        hn_ref[...] = _rms(x_ref[...], nw_ref[...]).astype(BF16)
        o_ref[...] = jnp.zeros_like(o_ref)

    h = hn_ref[...]
    g = jnp.dot(h, wg_ref[...], preferred_element_type=F32)
    u = jnp.dot(h, wu_ref[...], preferred_element_type=F32)
    a = (g * jax.nn.sigmoid(g) * u).astype(BF16)
    o_ref[...] += jnp.dot(a, wd_ref[...], preferred_element_type=F32)

    @pl.when(j == n_f - 1)
    def _():
        y = x_ref[...] + 0.5 * o_ref[...]
        if final:
            y = _rms(y, fn_ref[...])
        o_ref[...] = y


def _ffn(x, nw, wg, wu, wd, fn, *, layer, final, tm, tf):
    m, d = x.shape
    f = wg.shape[2]
    n_f = f // tf
    return pl.pallas_call(
        functools.partial(_ffn_body, n_f=n_f, final=final),
        grid=(m // tm, n_f),
        in_specs=[
            pl.BlockSpec((tm, d), lambda i, j: (i, 0)),
            pl.BlockSpec((1, d), lambda i, j: (0, 0)),
            pl.BlockSpec((None, d, tf), lambda i, j: (layer, 0, j)),
            pl.BlockSpec((None, d, tf), lambda i, j: (layer, 0, j)),
            pl.BlockSpec((None, tf, d), lambda i, j: (layer, j, 0)),
            pl.BlockSpec((1, d), lambda i, j: (0, 0)),
        ],
        out_specs=pl.BlockSpec((tm, d), lambda i, j: (i, 0)),
        out_shape=jax.ShapeDtypeStruct((m, d), F32),
        scratch_shapes=[pltpu.VMEM((tm, d), BF16)],
        compiler_params=_cparams(("parallel", "arbitrary")),
        name="ffn_final" if final else "ffn",
    )(x, nw, wg, wu, wd, fn)


def _proj_body(x_ref, nw_ref, w_ref, o_ref, hn_ref):
    @pl.when(pl.program_id(1) == 0)
    def _():
        hn_ref[...] = _rms(x_ref[...], nw_ref[...]).astype(BF16)

    o_ref[...] = jnp.dot(hn_ref[...], w_ref[...], preferred_element_type=F32)


def _proj(x, nw, w, *, layer, tm, tn):
    m, d = x.shape
    n = w.shape[2]
    return pl.pallas_call(
        _proj_body,
        grid=(m // tm, n // tn),
        in_specs=[
            pl.BlockSpec((tm, d), lambda i, j: (i, 0)),
            pl.BlockSpec((1, d), lambda i, j: (0, 0)),
            pl.BlockSpec((None, d, tn), lambda i, j: (layer, 0, j)),
        ],
        out_specs=pl.BlockSpec((tm, tn), lambda i, j: (i, j)),
        out_shape=jax.ShapeDtypeStruct((m, n), F32),
        scratch_shapes=[pltpu.VMEM((tm, d), BF16)],
        compiler_params=_cparams(("parallel", "arbitrary")),
        name="in_proj",
    )(x, nw, w)


def _outproj_body(x_ref, ya_ref, yb_ref, yc_ref, yd_ref, w_ref, o_ref):
    acc = x_ref[...]
    for gi, y_ref in enumerate((ya_ref, yb_ref, yc_ref, yd_ref)):
        acc = acc + jnp.dot(y_ref[...], w_ref[gi * W_GROUP:(gi + 1) * W_GROUP, :],
                            preferred_element_type=F32)
    o_ref[...] = acc


def _outproj(x, ya, yb, yc, yd, w, *, layer, tm):
    m, d = x.shape
    yspec = pl.BlockSpec((tm, W_GROUP), lambda i: (i, 0))
    return pl.pallas_call(
        _outproj_body,
        grid=(m // tm,),
        in_specs=[pl.BlockSpec((tm, d), lambda i: (i, 0)), yspec, yspec, yspec, yspec,
                  pl.BlockSpec((None,) + w.shape[1:], lambda i: (layer, 0, 0))],
        out_specs=pl.BlockSpec((tm, d), lambda i: (i, 0)),
        out_shape=jax.ShapeDtypeStruct((m, d), F32),
        compiler_params=_cparams(("parallel",)),
        name="out_proj",
    )(x, ya, yb, yc, yd, w)


def _gate_body(u_ref, v_ref, wm_ref, bias_ref, g_ref, b_ref, y_ref, *vr_ref, tm):
    gu = jax.nn.gelu(u_ref[...], approximate=True)
    gv = jax.nn.gelu(v_ref[...], approximate=True)
    mu = jnp.mean(gv, axis=-1, keepdims=True)
    var = jnp.mean(jnp.square(gv - mu), axis=-1, keepdims=True)
    vn = (gv - mu) * lax.rsqrt(var + EPS) * g_ref[...] + b_ref[...]
    if vr_ref:
        vr_ref[0][...] = vn
    vnb = vn.astype(BF16)
    for c in range(tm // A_CHUNK):
        rows = slice(c * A_CHUNK, (c + 1) * A_CHUNK)
        for h in range(W_GROUP // A_CHUNK):
            cols = slice(h * A_CHUNK, (h + 1) * A_CHUNK)
            z = jnp.dot(wm_ref[h], vnb[rows, cols], preferred_element_type=F32) + bias_ref[:, cols]
            y_ref[rows, cols] = (gu[rows, cols] * z).astype(BF16)


def _gate(p2d, wm, bias, ln_g, ln_b, *, tm, with_rows):
    m = p2d.shape[0]
    row_spec = pl.BlockSpec((tm, W_GROUP), lambda i: (i, 0))
    out_shape = [jax.ShapeDtypeStruct((m, W_GROUP), BF16)]
    out_specs = [row_spec]
    if with_rows:
        out_shape.append(jax.ShapeDtypeStruct((m, W_GROUP), F32))
        out_specs.append(row_spec)
    res = pl.pallas_call(
        functools.partial(_gate_body, tm=tm),
        grid=(m // tm,),
        in_specs=[
            pl.BlockSpec((tm, W_GROUP), lambda i: (i, COL_A_U)),
            pl.BlockSpec((tm, W_GROUP), lambda i: (i, COL_A_V)),
            pl.BlockSpec(wm.shape, lambda i: (0, 0, 0)),
            pl.BlockSpec(bias.shape, lambda i: (0, 0)),
            pl.BlockSpec((1, W_GROUP), lambda i: (0, 0)),
            pl.BlockSpec((1, W_GROUP), lambda i: (0, 0)),
        ],
        out_specs=out_specs,
        out_shape=out_shape,
        compiler_params=_cparams(("parallel",)),
        name="spatial_gate",
    )(p2d, p2d, wm, bias, ln_g, ln_b)
    return res if with_rows else (res[0], None)


def _ret_body(q_ref, k_ref, v_ref, g_ref, cos_ref, sin_ref, dm_ref, qd_ref, kd_ref, s0_ref,
              y_ref, so_ref, s_ref, *, bb, n_c, chunk_decay):
    c = pl.program_id(1)

    @pl.when(c == 0)
    def _():
        s_ref[...] = s0_ref[...]

    cos = cos_ref[...]
    sin = sin_ref[...]
    nt = (((1,), (1,)), ((), ()))
    tn = (((0,), (0,)), ((), ()))
    for b in range(bb):
        for h in range(R_HEADS):
            cols = slice(h * R_HD, (h + 1) * R_HD)
            q = q_ref[b, :, cols]
            k = k_ref[b, :, cols]
            v = v_ref[b, :, cols].astype(BF16)
            qr = q * cos + pltpu.roll(q, R_HD // 2, axis=1) * sin
            kr = (k * cos + pltpu.roll(k, R_HD // 2, axis=1) * sin) * (R_HD ** -0.5)
            sc = lax.dot_general(qr.astype(BF16), kr.astype(BF16), nt,
                                 preferred_element_type=F32) * dm_ref[h]
            s = s_ref[b, h]
            o = jnp.dot(sc.astype(BF16), v, preferred_element_type=F32)
            o = o + jnp.dot((qr * qd_ref[:, cols]).astype(BF16), s.astype(BF16),
                            preferred_element_type=F32)
            kv = lax.dot_general((kr * kd_ref[:, cols]).astype(BF16), v, tn,
                                 preferred_element_type=F32)
            s_ref[b, h] = chunk_decay[h] * s + kv
            o = o * lax.rsqrt(jnp.mean(o * o, axis=-1, keepdims=True) + EPS)
            g = g_ref[b, :, cols]
            y_ref[b, :, cols] = (o * (g * jax.nn.sigmoid(g))).astype(BF16)

    @pl.when(c == n_c - 1)
    def _():
        so_ref[...] = s_ref[...]


def _ret_tables(cl, pos0, length):
    half = R_HD // 2
    inv = ROPE_BASE ** (-jnp.arange(half, dtype=F32) / half)
    pos = pos0 + jnp.arange(length, dtype=F32)
    ang = pos[:, None] * inv[None, :]
    cos = jnp.cos(ang)
    sin = jnp.sin(ang)
    cos_t = jnp.concatenate([cos, cos], axis=-1)
    sin_t = jnp.concatenate([-sin, sin], axis=-1)
    log_gamma = np.log(1.0 - 2.0 ** (-5.0 - np.arange(R_HEADS, dtype=np.float64)))
    idx = np.arange(cl, dtype=np.float64)
    diff = idx[:, None] - idx[None, :]
    dmat = np.where(diff >= 0, np.exp(np.maximum(diff, 0.0)[None] * log_gamma[:, None, None]), 0.0)
    kdec = np.exp((cl - 1.0 - idx)[:, None] * log_gamma[None, :])
    qdec = np.exp((idx + 1.0)[:, None] * log_gamma[None, :])
    chunk_decay = tuple(float(x) for x in np.exp(cl * log_gamma))
    rep = lambda a: jnp.asarray(np.repeat(a, R_HD, axis=1), F32)
    return cos_t, sin_t, jnp.asarray(dmat, F32), rep(qdec), rep(kdec), chunk_decay


def _retention(p3d, s0_all, layer, pos0, *, bb):
    bsz, length, _ = p3d.shape
    cl = min(R_CHUNK, length)
    n_c = length // cl
    cos_t, sin_t, dmat, qdec, kdec, chunk_decay = _ret_tables(cl, pos0, length)

    def col(j):
        return pl.BlockSpec((bb, cl, W_GROUP), lambda b, c: (b, c, j))

    tab = pl.BlockSpec((cl, R_HD), lambda b, c: (c, 0))
    state = pl.BlockSpec((bb, R_HEADS, R_HD, R_HD), lambda b, c: (b, 0, 0, 0))
    return pl.pallas_call(
        functools.partial(_ret_body, bb=bb, n_c=n_c, chunk_decay=chunk_decay),
        grid=(bsz // bb, n_c),
        in_specs=[col(COL_R_Q), col(COL_R_K), col(COL_R_V), col(COL_R_G), tab, tab,
                  pl.BlockSpec(dmat.shape, lambda b, c: (0, 0, 0)),
                  pl.BlockSpec(qdec.shape, lambda b, c: (0, 0)),
                  pl.BlockSpec(kdec.shape, lambda b, c: (0, 0)),
                  pl.BlockSpec((None, bb, R_HEADS, R_HD, R_HD), lambda b, c: (layer, b, 0, 0, 0))],
        out_specs=[pl.BlockSpec((bb, cl, W_GROUP), lambda b, c: (b, c, 0)), state],
        out_shape=[jax.ShapeDtypeStruct((bsz, length, W_GROUP), BF16),
                   jax.ShapeDtypeStruct(s0_all.shape[1:], F32)],
        scratch_shapes=[pltpu.VMEM((bb, R_HEADS, R_HD, R_HD), F32)],
        compiler_params=_cparams(("parallel", "arbitrary")),
        name="retention",
    )(p3d, p3d, p3d, p3d, cos_t, sin_t, dmat, qdec, kdec, s0_all)


def _conv_body(bg_ref, cg_ref, h_ref, buf_ref, w_ref, y_ref, st_ref, carry_ref, *, bb, tt):
    @pl.when(pl.program_id(1) == 0)
    def _():
        carry_ref[...] = buf_ref[...]

    shape = (bb, tt, W_GROUP)
    z = cg_ref[...] * h_ref[...]
    z2 = z.reshape(bb * tt, W_GROUP)
    r1 = pltpu.roll(z2, 1, axis=0).reshape(shape)
    r2 = pltpu.roll(z2, 2, axis=0).reshape(shape)
    tpos = lax.broadcasted_iota(jnp.int32, shape, 1)
    c0 = carry_ref[:, 0:1, :]
    c1 = carry_ref[:, 1:2, :]
    zm1 = jnp.where(tpos == 0, c1, r1)
    zm2 = jnp.where(tpos == 0, c0, jnp.where(tpos == 1, c1, r2))
    y = w_ref[0:1, :] * zm2 + w_ref[1:2, :] * zm1 + w_ref[2:3, :] * z
    y_ref[...] = (bg_ref[...] * y).astype(BF16)
    new = cg_ref[:, tt - 2:tt, :] * h_ref[:, tt - 2:tt, :]
    carry_ref[...] = new
    st_ref[...] = new


def _conv(p3d, buf, w, *, bb, tt):
    bsz, length, _ = p3d.shape

    def col(j):
        return pl.BlockSpec((bb, tt, W_GROUP), lambda b, t: (b, t, j))

    state = pl.BlockSpec((bb, C_WIDTH - 1, W_GROUP), lambda b, t: (b, 0, 0))
    return pl.pallas_call(
        functools.partial(_conv_body, bb=bb, tt=tt),
        grid=(bsz // bb, length // tt),
        in_specs=[col(COL_C_B), col(COL_C_C), col(COL_C_H), state,
                  pl.BlockSpec(w.shape, lambda b, t: (0, 0))],
        out_specs=[pl.BlockSpec((bb, tt, W_GROUP), lambda b, t: (b, t, 0)), state],
        out_shape=[jax.ShapeDtypeStruct((bsz, length, W_GROUP), BF16),
                   jax.ShapeDtypeStruct(buf.shape, F32)],
        scratch_shapes=[pltpu.VMEM((bb, C_WIDTH - 1, W_GROUP), F32)],
        compiler_params=_cparams(("parallel", "arbitrary")),
        name="short_conv",
    )(p3d, p3d, p3d, buf, w)


def _group_dot(xb, ones_bd):
    wb = ones_bd.shape[0]
    return jnp.concatenate(
        [jnp.dot(xb[:, i * wb:(i + 1) * wb], ones_bd, preferred_element_type=F32)
         for i in range(W_GROUP // wb)], axis=1)


def _head_sum(x, ones_bd):
    hi = x.astype(BF16)
    lo = (x - hi.astype(F32)).astype(BF16)
    return _group_dot(hi, ones_bd) + _group_dot(lo, ones_bd)


def _rwkv_body(r_ref, k_ref, v_ref, lo_ref, sh_ref, s0_ref,
               mu_ref, mulo_ref, w0_ref, w2_ref, a0_ref, a2_ref, g2_ref, kk_ref, ka_ref, rk_ref,
               lnw_ref, lnb_ref, ones_ref,
               y_ref, so_ref,
               s_ref, carry_ref, carrylo_ref, rs, ws, ks, vs, kks, kas, ys, *, bb, tt, n_t, gsz, unroll):
    tb = pl.program_id(1)
    n = bb * tt
    w3 = (bb, tt, W_GROUP)
    ones_bd = ones_ref[...]

    @pl.when(tb == 0)
    def _():
        for b in range(bb):
            s_ref[b] = jnp.concatenate([s0_ref[b, h] for h in range(K_HEADS)], axis=1)
        carry_ref[...] = sh_ref[:, :, 0:3 * W_GROUP]
        carrylo_ref[...] = sh_ref[:, :, 3 * W_GROUP:K_COLS]

    tpos = lax.broadcasted_iota(jnp.int32, w3, 1)
    tpos_lo = lax.broadcasted_iota(jnp.int32, (bb, tt, LORA_COLS), 1)

    def shifted(x, carry, mu, mask):
        prev = pltpu.roll(x.reshape(n, x.shape[-1]), 1, axis=0).reshape(x.shape)
        prev = jnp.where(mask == 0, carry, prev)
        return (x + (prev - x) * mu).reshape(n, x.shape[-1])

    r_in, k_in, v_in, lo_in = r_ref[...], k_ref[...], v_ref[...], lo_ref[...]
    r = shifted(r_in, carry_ref[:, :, 0:W_GROUP], mu_ref[:, 0:W_GROUP], tpos)
    k = shifted(k_in, carry_ref[:, :, W_GROUP:2 * W_GROUP], mu_ref[:, W_GROUP:2 * W_GROUP], tpos)
    v = shifted(v_in, carry_ref[:, :, 2 * W_GROUP:3 * W_GROUP], mu_ref[:, 2 * W_GROUP:3 * W_GROUP], tpos)
    lo = shifted(lo_in, carrylo_ref[...], mulo_ref[...], tpos_lo)
    carry_ref[:, :, 0:W_GROUP] = r_ref[:, tt - 1:tt, :]
    carry_ref[:, :, W_GROUP:2 * W_GROUP] = k_ref[:, tt - 1:tt, :]
    carry_ref[:, :, 2 * W_GROUP:3 * W_GROUP] = v_ref[:, tt - 1:tt, :]
    carrylo_ref[...] = lo_ref[:, tt - 1:tt, :]

    zw = w0_ref[...] + jnp.dot(jnp.tanh(lo).astype(BF16), w2_ref[...], preferred_element_type=F32)
    nz = -zw
    softplus = jnp.maximum(nz, 0.0) + jnp.log1p(jnp.exp(-jnp.abs(nz)))
    decay = jnp.exp(-jnp.exp(-softplus - 0.5))
    a = jax.nn.sigmoid(a0_ref[...] + jnp.dot(lo.astype(BF16), a2_ref[...], preferred_element_type=F32))
    gate = jnp.dot(jax.nn.sigmoid(lo).astype(BF16), g2_ref[...], preferred_element_type=F32)
    kk = k * kk_ref[...]
    kk = kk / jnp.maximum(jnp.sqrt(_head_sum(kk * kk, ones_bd)), 1e-12)
    k2 = k * (1.0 + (a - 1.0) * ka_ref[...])
    bonus = _head_sum(r * k2 * rk_ref[...], ones_bd) * v

    rs[...] = r.reshape(w3)
    ws[...] = decay.reshape(w3)
    ks[...] = k2.reshape(w3)
    vs[...] = v.reshape(w3)
    kks[...] = kk.reshape(w3)
    kas[...] = (kk * a).reshape(w3)

    s2 = (K_HD, W_GROUP)
    diag = (lax.broadcasted_iota(jnp.int32, s2, 0)
            == lax.broadcasted_iota(jnp.int32, s2, 1) % K_HD)

    def read_out(ybc):
        return jnp.sum(jnp.where(diag, ybc, 0.0), axis=0, keepdims=True)

    def step(t, carry):
        at = pl.ds(t, 1)
        prev = pl.ds(jnp.maximum(t - 1, 0), 1)
        for g0 in range(0, bb, gsz):
            parts = []
            for b in range(g0, g0 + gsz):
                s = s_ref[b]
                parts += [(s * kks[b, at, :]).astype(BF16),
                          jnp.where(diag, vs[b, at, :], 0.0).astype(BF16),
                          (s * rs[b, prev, :]).astype(BF16)]
            res = _group_dot(jnp.concatenate(parts, axis=0), ones_bd)
            for i, b in enumerate(range(g0, g0 + gsz)):
                r0 = 3 * i * K_HD
                sk = res[r0:r0 + K_HD]
                vcol = res[r0 + K_HD:r0 + 2 * K_HD]
                ys[b, prev, :] = read_out(res[r0 + 2 * K_HD:r0 + 3 * K_HD])
                s_ref[b] = s_ref[b] * ws[b, at, :] - sk * kas[b, at, :] + vcol * ks[b, at, :]
        return carry

    lax.fori_loop(0, tt, step, 0, unroll=unroll)
    last = pl.ds(tt - 1, 1)
    for b in range(bb):
        sr = (s_ref[b] * rs[b, last, :]).astype(BF16)
        ys[b, last, :] = read_out(_group_dot(sr, ones_bd))

    y = ys[...].reshape(n, W_GROUP)
    mean = _head_sum(y, ones_bd) * (1.0 / K_HD)
    yc = y - mean
    var = _head_sum(yc * yc, ones_bd) * (1.0 / K_HD)
    out = (yc * lax.rsqrt(var + GN_EPS) * lnw_ref[...] + lnb_ref[...] + bonus) * gate
    y_ref[...] = out.reshape(w3).astype(BF16)

    @pl.when(tb == n_t - 1)
    def _():
        for b in range(bb):
            for h in range(K_HEADS):
                so_ref[b, h] = s_ref[b, :, h * K_HD:(h + 1) * K_HD]


def _rwkv(p3d, shift, s0_all, layer, prm, *, bb, tt, gsz, unroll):
    bsz, length, _ = p3d.shape
    n_t = length // tt

    def col(j):
        return pl.BlockSpec((bb, tt, W_GROUP), lambda b, t: (b, t, j))

    def whole(a):
        nd = a.ndim
        return pl.BlockSpec(a.shape, lambda b, t: (0,) * nd)

    state = pl.BlockSpec((bb, K_HEADS, K_HD, K_HD), lambda b, t: (b, 0, 0, 0))
    params = [prm[nm] for nm in ("mu", "mu_lo", "w0", "w2", "a0", "a2", "g2", "k_k", "k_a", "r_k",
                                 "ln_w", "ln_b", "ones_bd")]
    blk = pltpu.VMEM((bb, tt, W_GROUP), F32)
    return pl.pallas_call(
        functools.partial(_rwkv_body, bb=bb, tt=tt, n_t=n_t, gsz=gsz, unroll=unroll),
        grid=(bsz // bb, n_t),
        in_specs=[col(COL_K_R), col(COL_K_K), col(COL_K_V),
                  pl.BlockSpec((bb, tt, LORA_COLS), lambda b, t: (b, t, COL_K_LORA)),
                  pl.BlockSpec((bb, 1, K_COLS), lambda b, t: (b, 0, 0)),
                  pl.BlockSpec((None, bb, K_HEADS, K_HD, K_HD), lambda b, t: (layer, b, 0, 0, 0))]
                 + [whole(a) for a in params],
        out_specs=[pl.BlockSpec((bb, tt, W_GROUP), lambda b, t: (b, t, 0)), state],
        out_shape=[jax.ShapeDtypeStruct((bsz, length, W_GROUP), BF16),
                   jax.ShapeDtypeStruct(s0_all.shape[1:], F32)],
        scratch_shapes=[pltpu.VMEM((bb, K_HD, W_GROUP), F32),
                        pltpu.VMEM((bb, 1, 3 * W_GROUP), F32),
                        pltpu.VMEM((bb, 1, LORA_COLS), F32),
                        blk, blk, blk, blk, blk, blk, blk],
        compiler_params=_cparams(("parallel", "arbitrary")),
        name="rwkv7",
    )(p3d, p3d, p3d, p3d, shift, s0_all, *params)


def _gate_mixing(w_s, b_s, seq):
    cl = min(A_CHUNK, seq)
    wm = jnp.tril(w_s[:, :cl, :cl])
    bias = b_s[:, :cl]
    rep = A_CHUNK // cl
    if rep > 1:
        eye = jnp.eye(rep, dtype=w_s.dtype)
        wm = jnp.einsum("ab,hts->hatbs", eye, wm).reshape(w_s.shape[0], A_CHUNK, A_CHUNK)
        bias = jnp.tile(bias, (1, rep))
    bias = jnp.repeat(bias.T, A_CHUNK, axis=1)
    return wm.astype(BF16), bias


def _pad_rows(w, row0):
    return jnp.zeros((LORA_COLS, W_GROUP), F32).at[row0:row0 + w.shape[0]].set(w).astype(BF16)


def _layer_params(l, ffn1_norm, ffn1_w_gate, ffn1_w_up, ffn1_w_down, mix_norm, w_in, w_out,
                  a_w_s, a_b_s, a_ln_g, a_ln_b, c_conv_w,
                  k_mu, k_w0, k_w2, k_a0, k_a2, k_g2, k_k_k, k_k_a, k_r_k, k_ln_w, k_ln_b,
                  ffn2_norm, ffn2_w_gate, ffn2_w_up, ffn2_w_down):
    row = lambda a: a[l].reshape(1, -1)
    head_of = np.arange(V7X_MXU_DIM) // K_HD
    ones_bd = jnp.asarray(head_of[:, None] == head_of[None, :], BF16)
    rwkv = dict(
        mu=k_mu[l][None, :3 * W_GROUP], mu_lo=k_mu[l][None, 3 * W_GROUP:],
        w0=row(k_w0), w2=_pad_rows(k_w2[l], 0),
        a0=row(k_a0), a2=_pad_rows(k_a2[l], W_LORA),
        g2=_pad_rows(k_g2[l], W_LORA + A_LORA),
        k_k=row(k_k_k), k_a=row(k_k_a), r_k=row(k_r_k), ln_w=row(k_ln_w), ln_b=row(k_ln_b),
        ones_bd=ones_bd)
    return dict(
        layer=l,
        ffn1=(row(ffn1_norm), ffn1_w_gate, ffn1_w_up, ffn1_w_down),
        ffn2=(row(ffn2_norm), ffn2_w_gate, ffn2_w_up, ffn2_w_down),
        mix_norm=row(mix_norm), w_in=w_in, w_out=w_out,
        a_w_s=a_w_s[l], a_b_s=a_b_s[l], a_ln_g=row(a_ln_g), a_ln_b=row(a_ln_b),
        conv_w=c_conv_w[l], rwkv=rwkv)


def _stream_layer(x2d, bsz, length, pos0, ret_s0_all, ret_layer, conv_buf, rw_shift, rw_s0_all, p, fn, *,
                  final, cfg):
    layer = p["layer"]
    x1 = _ffn(x2d, *p["ffn1"], fn, layer=layer, final=False, tm=cfg["tm"], tf=cfg["tf"])
    proj = _proj(x1, p["mix_norm"], p["w_in"], layer=layer, tm=cfg["proj_tm"], tn=cfg["tn"])
    p3d = proj.reshape(bsz, length, proj.shape[1])

    wm, bias = _gate_mixing(p["a_w_s"], p["a_b_s"], length)
    ya, v_rows = _gate(proj, wm, bias, p["a_ln_g"], p["a_ln_b"], tm=cfg["gate_tm"],
                       with_rows=cfg["with_rows"])
    yb, ret_s = _retention(p3d, ret_s0_all, ret_layer, pos0, bb=cfg["ret_bb"])
    yc, conv_new = _conv(p3d, conv_buf, p["conv_w"], bb=cfg["conv_bb"], tt=cfg["conv_tt"])
    yd, rw_s = _rwkv(p3d, rw_shift[:, None, :], rw_s0_all, ret_layer, p["rwkv"],
                     bb=cfg["rwkv_bb"], tt=cfg["rwkv_tt"], gsz=cfg["rwkv_gsz"], unroll=cfg["rwkv_unroll"])
    shift_new = p3d[:, length - 1, K_COL0:K_COL0 + K_COLS]

    flat = lambda y: y.reshape(bsz * length, W_GROUP)
    x2 = _outproj(x1, ya, flat(yb), flat(yc), flat(yd), p["w_out"], layer=layer, tm=cfg["tm"])
    x3 = _ffn(x2, *p["ffn2"], fn, layer=layer, final=final, tm=cfg["tm"], tf=cfg["tf"])
    return x3, ret_s, conv_new, shift_new, rw_s, v_rows


def _stream_cfg(bsz, length, sample):
    m = bsz * length
    tm = min(512, m)
    if sample:
        return dict(tm=tm, proj_tm=min(1024, m), tf=512, tn=1280, gate_tm=min(512, m), with_rows=True,
                    ret_bb=8, conv_bb=32, conv_tt=length, rwkv_bb=16, rwkv_tt=length, rwkv_gsz=4, rwkv_unroll=1)
    return dict(tm=tm, proj_tm=min(1024, m), tf=512, tn=1280, gate_tm=min(512, m), with_rows=False,
                ret_bb=1, conv_bb=1, conv_tt=min(512, length), rwkv_bb=bsz, rwkv_tt=min(128, length), rwkv_gsz=4, rwkv_unroll=2)


def kernel(x_prompt, x_sample, state_ret, state_conv, state_rwkv_shift, state_rwkv, ffn1_norm, ffn1_w_gate, ffn1_w_up, ffn1_w_down, mix_norm, w_in, w_out, a_w_s, a_b_s, a_ln_g, a_ln_b, c_conv_w, k_mu, k_w0, k_w2, k_a0, k_a2, k_g2, k_k_k, k_k_a, k_r_k, k_ln_w, k_ln_b, ffn2_norm, ffn2_w_gate, ffn2_w_up, ffn2_w_down, final_norm):
    bp, lp, d = x_prompt.shape
    bs, ls, _ = x_sample.shape
    depth = ffn1_norm.shape[0]
    cfg_p = _stream_cfg(bp, lp, sample=False)
    cfg_s = _stream_cfg(bs, ls, sample=True)
    fn = final_norm.reshape(1, d)

    zero_ret = jnp.zeros((1, bp, R_HEADS, R_HD, R_HD), F32)
    zero_conv = jnp.zeros((bp, C_WIDTH - 1, W_GROUP), F32)
    zero_shift = jnp.zeros((bp, K_COLS), F32)
    zero_rw = jnp.zeros((1, bp, K_HEADS, K_HD, K_HD), F32)

    ffn1_w_gate, ffn1_w_up, ffn1_w_down, w_in, w_out, ffn2_w_gate, ffn2_w_up, ffn2_w_down = (
        w.astype(BF16) for w in (ffn1_w_gate, ffn1_w_up, ffn1_w_down, w_in, w_out,
                                 ffn2_w_gate, ffn2_w_up, ffn2_w_down))
    xp = x_prompt.reshape(bp * lp, d)
    xs = x_sample.reshape(bs * ls, d)
    outs = [[] for _ in range(9)]
    for l in range(depth):
        p = _layer_params(l, ffn1_norm, ffn1_w_gate, ffn1_w_up, ffn1_w_down, mix_norm, w_in, w_out,
                          a_w_s, a_b_s, a_ln_g, a_ln_b, c_conv_w,
                          k_mu, k_w0, k_w2, k_a0, k_a2, k_g2, k_k_k, k_k_a, k_r_k, k_ln_w, k_ln_b,
                          ffn2_norm, ffn2_w_gate, ffn2_w_up, ffn2_w_down)
        final = l == depth - 1
        xp, rp, cp, sp, wp, _ = _stream_layer(xp, bp, lp, 0.0, zero_ret, 0, zero_conv, zero_shift, zero_rw,
                                              p, fn, final=final, cfg=cfg_p)
        xs, rs, cs, ss, ws, vs = _stream_layer(xs, bs, ls, float(PAST_LEN), state_ret, l, state_conv[l],
                                               state_rwkv_shift[l], state_rwkv, p, fn,
                                               final=final, cfg=cfg_s)
        for acc, val in zip(outs, (rp, rs, cp, cs, sp, ss, wp, ws, vs.reshape(bs, ls, W_GROUP))):
            acc.append(val)

    return (xp.reshape(bp, lp, d), xs.reshape(bs, ls, d)) + tuple(jnp.stack(o) for o in outs)
```

```python
import functools
import math

import numpy as np
import jax
import jax.numpy as jnp
from jax import lax
from jax.experimental import pallas as pl
from jax.experimental.pallas import tpu as pltpu

F32 = jnp.float32
BF16 = jnp.bfloat16

W_GROUP = 512
A_CHUNK = 128
R_HEADS = 4
R_HD = 128
R_CHUNK = 128
ROPE_BASE = 10000.0
C_WIDTH = 3
K_HD = 64
K_HEADS = 8
W_LORA = 64
A_LORA = 64
G_LORA = 128
LORA_COLS = W_LORA + A_LORA + G_LORA
K_COLS = 3 * W_GROUP + LORA_COLS
EPS = 1e-6
GN_EPS = 64e-5
PAST_LEN = 16384

COL_A_U, COL_A_V = 0, 1
COL_R_Q, COL_R_K, COL_R_V, COL_R_G = 2, 3, 4, 5
COL_C_B, COL_C_C, COL_C_H = 6, 7, 8
COL_K_R, COL_K_K, COL_K_V = 9, 10, 11
COL_K_LORA = (12 * W_GROUP) // LORA_COLS
K_COL0 = 9 * W_GROUP

V7X_VMEM_LIMIT_BYTES = 56 * 1024 * 1024
V7X_MXU_DIM = 256


def _cparams(sem, vmem=V7X_VMEM_LIMIT_BYTES):
    return pltpu.CompilerParams(dimension_semantics=sem, vmem_limit_bytes=vmem)


def _rms(x, w):
    return x * lax.rsqrt(jnp.mean(x * x, axis=-1, keepdims=True) + EPS) * w


def _ffn_body(x_ref, nw_ref, wg_ref, wu_ref, wd_ref, fn_ref, o_ref, hn_ref, *, n_f, final):
    j = pl.program_id(1)

    @pl.when(j == 0)
    def _():
        hn_ref[...] = _rms(x_ref[...], nw_ref[...]).astype(BF16)
        o_ref[...] = jnp.zeros_like(o_ref)

    h = hn_ref[...]
    g = jnp.dot(h, wg_ref[...], preferred_element_type=F32)
    u = jnp.dot(h, wu_ref[...], preferred_element_type=F32)
    a = (g * jax.nn.sigmoid(g) * u).astype(BF16)
    o_ref[...] += jnp.dot(a, wd_ref[...], preferred_element_type=F32)

    @pl.when(j == n_f - 1)
    def _():
        y = x_ref[...] + 0.5 * o_ref[...]
        if final:
            y = _rms(y, fn_ref[...])
        o_ref[...] = y


def _ffn(x, nw, wg, wu, wd, fn, *, layer, final, tm, tf):
    m, d = x.shape
    f = wg.shape[2]
    n_f = f // tf
    return pl.pallas_call(
        functools.partial(_ffn_body, n_f=n_f, final=final),
        grid=(m // tm, n_f),
        in_specs=[
            pl.BlockSpec((tm, d), lambda i, j: (i, 0)),
            pl.BlockSpec((1, d), lambda i, j: (0, 0)),
            pl.BlockSpec((None, d, tf), lambda i, j: (layer, 0, j)),
            pl.BlockSpec((None, d, tf), lambda i, j: (layer, 0, j)),
            pl.BlockSpec((None, tf, d), lambda i, j: (layer, j, 0)),
            pl.BlockSpec((1, d), lambda i, j: (0, 0)),
        ],
        out_specs=pl.BlockSpec((tm, d), lambda i, j: (i, 0)),
        out_shape=jax.ShapeDtypeStruct((m, d), F32),
        scratch_shapes=[pltpu.VMEM((tm, d), BF16)],
        compiler_params=_cparams(("parallel", "arbitrary")),
        name="ffn_final" if final else "ffn",
    )(x, nw, wg, wu, wd, fn)


def _proj_body(x_ref, nw_ref, w_ref, o_ref, hn_ref):
    @pl.when(pl.program_id(1) == 0)
    def _():
        hn_ref[...] = _rms(x_ref[...], nw_ref[...]).astype(BF16)

    o_ref[...] = jnp.dot(hn_ref[...], w_ref[...], preferred_element_type=F32)


def _proj(x, nw, w, *, layer, tm, tn):
    m, d = x.shape
    n = w.shape[2]
    return pl.pallas_call(
        _proj_body,
        grid=(m // tm, n // tn),
        in_specs=[
            pl.BlockSpec((tm, d), lambda i, j: (i, 0)),
            pl.BlockSpec((1, d), lambda i, j: (0, 0)),
            pl.BlockSpec((None, d, tn), lambda i, j: (layer, 0, j)),
        ],
        out_specs=pl.BlockSpec((tm, tn), lambda i, j: (i, j)),
        out_shape=jax.ShapeDtypeStruct((m, n), F32),
        scratch_shapes=[pltpu.VMEM((tm, d), BF16)],
        compiler_params=_cparams(("parallel", "arbitrary")),
        name="in_proj",
    )(x, nw, w)


def _outproj_body(x_ref, ya_ref, yb_ref, yc_ref, yd_ref, w_ref, o_ref):
    acc = x_ref[...]
    for gi, y_ref in enumerate((ya_ref, yb_ref, yc_ref, yd_ref)):
        acc = acc + jnp.dot(y_ref[...], w_ref[gi * W_GROUP:(gi + 1) * W_GROUP, :],
                            preferred_element_type=F32)
    o_ref[...] = acc


def _outproj(x, ya, yb, yc, yd, w, *, layer, tm):
    m, d = x.shape
    yspec = pl.BlockSpec((tm, W_GROUP), lambda i: (i, 0))
    return pl.pallas_call(
        _outproj_body,
        grid=(m // tm,),
        in_specs=[pl.BlockSpec((tm, d), lambda i: (i, 0)), yspec, yspec, yspec, yspec,
                  pl.BlockSpec((None,) + w.shape[1:], lambda i: (layer, 0, 0))],
        out_specs=pl.BlockSpec((tm, d), lambda i: (i, 0)),
        out_shape=jax.ShapeDtypeStruct((m, d), F32),
        compiler_params=_cparams(("parallel",)),
        name="out_proj",
    )(x, ya, yb, yc, yd, w)


def _gate_body(u_ref, v_ref, wm_ref, bias_ref, g_ref, b_ref, y_ref, *vr_ref, tm):
    gu = jax.nn.gelu(u_ref[...], approximate=True)
    gv = jax.nn.gelu(v_ref[...], approximate=True)
    mu = jnp.mean(gv, axis=-1, keepdims=True)
    var = jnp.mean(jnp.square(gv - mu), axis=-1, keepdims=True)
    vn = (gv - mu) * lax.rsqrt(var + EPS) * g_ref[...] + b_ref[...]
    if vr_ref:
        vr_ref[0][...] = vn
    vnb = vn.astype(BF16)
    for c in range(tm // A_CHUNK):
        rows = slice(c * A_CHUNK, (c + 1) * A_CHUNK)
        for h in range(W_GROUP // A_CHUNK):
            cols = slice(h * A_CHUNK, (h + 1) * A_CHUNK)
            z = jnp.dot(wm_ref[h], vnb[rows, cols], preferred_element_type=F32) + bias_ref[:, cols]
            y_ref[rows, cols] = (gu[rows, cols] * z).astype(BF16)


def _gate(p2d, wm, bias, ln_g, ln_b, *, tm, with_rows):
    m = p2d.shape[0]
    row_spec = pl.BlockSpec((tm, W_GROUP), lambda i: (i, 0))
    out_shape = [jax.ShapeDtypeStruct((m, W_GROUP), BF16)]
    out_specs = [row_spec]
    if with_rows:
        out_shape.append(jax.ShapeDtypeStruct((m, W_GROUP), F32))
        out_specs.append(row_spec)
    res = pl.pallas_call(
        functools.partial(_gate_body, tm=tm),
        grid=(m // tm,),
        in_specs=[
            pl.BlockSpec((tm, W_GROUP), lambda i: (i, COL_A_U)),
            pl.BlockSpec((tm, W_GROUP), lambda i: (i, COL_A_V)),
            pl.BlockSpec(wm.shape, lambda i: (0, 0, 0)),
            pl.BlockSpec(bias.shape, lambda i: (0, 0)),
            pl.BlockSpec((1, W_GROUP), lambda i: (0, 0)),
            pl.BlockSpec((1, W_GROUP), lambda i: (0, 0)),
        ],
        out_specs=out_specs,
        out_shape=out_shape,
        compiler_params=_cparams(("parallel",)),
        name="spatial_gate",
    )(p2d, p2d, wm, bias, ln_g, ln_b)
    return res if with_rows else (res[0], None)


def _ret_body(q_ref, k_ref, v_ref, g_ref, cos_ref, sin_ref, dm_ref, qd_ref, kd_ref, s0_ref,
              y_ref, so_ref, s_ref, *, bb, n_c, chunk_decay):
    c = pl.program_id(1)

    @pl.when(c == 0)
    def _():
        s_ref[...] = s0_ref[...]

    cos = cos_ref[...]
    sin = sin_ref[...]
    nt = (((1,), (1,)), ((), ()))
    tn = (((0,), (0,)), ((), ()))
    for b in range(bb):
        for h in range(R_HEADS):
            cols = slice(h * R_HD, (h + 1) * R_HD)
            q = q_ref[b, :, cols]
            k = k_ref[b, :, cols]
            v = v_ref[b, :, cols].astype(BF16)
            qr = q * cos + pltpu.roll(q, R_HD // 2, axis=1) * sin
            kr = (k * cos + pltpu.roll(k, R_HD // 2, axis=1) * sin) * (R_HD ** -0.5)
            sc = lax.dot_general(qr.astype(BF16), kr.astype(BF16), nt,
                                 preferred_element_type=F32) * dm_ref[h]
            s = s_ref[b, h]
            o = jnp.dot(sc.astype(BF16), v, preferred_element_type=F32)
            o = o + jnp.dot((qr * qd_ref[:, cols]).astype(BF16), s.astype(BF16),
                            preferred_element_type=F32)
            kv = lax.dot_general((kr * kd_ref[:, cols]).astype(BF16), v, tn,
                                 preferred_element_type=F32)
            s_ref[b, h] = chunk_decay[h] * s + kv
            o = o * lax.rsqrt(jnp.mean(o * o, axis=-1, keepdims=True) + EPS)
            g = g_ref[b, :, cols]
            y_ref[b, :, cols] = (o * (g * jax.nn.sigmoid(g))).astype(BF16)

    @pl.when(c == n_c - 1)
    def _():
        so_ref[...] = s_ref[...]


def _ret_tables(cl, pos0, length):
    half = R_HD // 2
    inv = ROPE_BASE ** (-jnp.arange(half, dtype=F32) / half)
    pos = pos0 + jnp.arange(length, dtype=F32)
    ang = pos[:, None] * inv[None, :]
    cos = jnp.cos(ang)
    sin = jnp.sin(ang)
    cos_t = jnp.concatenate([cos, cos], axis=-1)
    sin_t = jnp.concatenate([-sin, sin], axis=-1)
    log_gamma = np.log(1.0 - 2.0 ** (-5.0 - np.arange(R_HEADS, dtype=np.float64)))
    idx = np.arange(cl, dtype=np.float64)
    diff = idx[:, None] - idx[None, :]
    dmat = np.where(diff >= 0, np.exp(np.maximum(diff, 0.0)[None] * log_gamma[:, None, None]), 0.0)
    kdec = np.exp((cl - 1.0 - idx)[:, None] * log_gamma[None, :])
    qdec = np.exp((idx + 1.0)[:, None] * log_gamma[None, :])
    chunk_decay = tuple(float(x) for x in np.exp(cl * log_gamma))
    rep = lambda a: jnp.asarray(np.repeat(a, R_HD, axis=1), F32)
    return cos_t, sin_t, jnp.asarray(dmat, F32), rep(qdec), rep(kdec), chunk_decay


def _retention(p3d, s0_all, layer, pos0, *, bb):
    bsz, length, _ = p3d.shape
    cl = min(R_CHUNK, length)
    n_c = length // cl
    cos_t, sin_t, dmat, qdec, kdec, chunk_decay = _ret_tables(cl, pos0, length)

    def col(j):
        return pl.BlockSpec((bb, cl, W_GROUP), lambda b, c: (b, c, j))

    tab = pl.BlockSpec((cl, R_HD), lambda b, c: (c, 0))
    state = pl.BlockSpec((bb, R_HEADS, R_HD, R_HD), lambda b, c: (b, 0, 0, 0))
    return pl.pallas_call(
        functools.partial(_ret_body, bb=bb, n_c=n_c, chunk_decay=chunk_decay),
        grid=(bsz // bb, n_c),
        in_specs=[col(COL_R_Q), col(COL_R_K), col(COL_R_V), col(COL_R_G), tab, tab,
                  pl.BlockSpec(dmat.shape, lambda b, c: (0, 0, 0)),
                  pl.BlockSpec(qdec.shape, lambda b, c: (0, 0)),
                  pl.BlockSpec(kdec.shape, lambda b, c: (0, 0)),
                  pl.BlockSpec((None, bb, R_HEADS, R_HD, R_HD), lambda b, c: (layer, b, 0, 0, 0))],
        out_specs=[pl.BlockSpec((bb, cl, W_GROUP), lambda b, c: (b, c, 0)), state],
        out_shape=[jax.ShapeDtypeStruct((bsz, length, W_GROUP), BF16),
                   jax.ShapeDtypeStruct(s0_all.shape[1:], F32)],
        scratch_shapes=[pltpu.VMEM((bb, R_HEADS, R_HD, R_HD), F32)],
        compiler_params=_cparams(("parallel", "arbitrary")),
        name="retention",
    )(p3d, p3d, p3d, p3d, cos_t, sin_t, dmat, qdec, kdec, s0_all)


def _conv_body(bg_ref, cg_ref, h_ref, buf_ref, w_ref, y_ref, st_ref, carry_ref, *, bb, tt):
    @pl.when(pl.program_id(1) == 0)
    def _():
        carry_ref[...] = buf_ref[...]

    shape = (bb, tt, W_GROUP)
    z = cg_ref[...] * h_ref[...]
    z2 = z.reshape(bb * tt, W_GROUP)
    r1 = pltpu.roll(z2, 1, axis=0).reshape(shape)
    r2 = pltpu.roll(z2, 2, axis=0).reshape(shape)
    tpos = lax.broadcasted_iota(jnp.int32, shape, 1)
    c0 = carry_ref[:, 0:1, :]
    c1 = carry_ref[:, 1:2, :]
    zm1 = jnp.where(tpos == 0, c1, r1)
    zm2 = jnp.where(tpos == 0, c0, jnp.where(tpos == 1, c1, r2))
    y = w_ref[0:1, :] * zm2 + w_ref[1:2, :] * zm1 + w_ref[2:3, :] * z
    y_ref[...] = (bg_ref[...] * y).astype(BF16)
    new = cg_ref[:, tt - 2:tt, :] * h_ref[:, tt - 2:tt, :]
    carry_ref[...] = new
    st_ref[...] = new


def _conv(p3d, buf, w, *, bb, tt):
    bsz, length, _ = p3d.shape

    def col(j):
        return pl.BlockSpec((bb, tt, W_GROUP), lambda b, t: (b, t, j))

    state = pl.BlockSpec((bb, C_WIDTH - 1, W_GROUP), lambda b, t: (b, 0, 0))
    return pl.pallas_call(
        functools.partial(_conv_body, bb=bb, tt=tt),
        grid=(bsz // bb, length // tt),
        in_specs=[col(COL_C_B), col(COL_C_C), col(COL_C_H), state,
                  pl.BlockSpec(w.shape, lambda b, t: (0, 0))],
        out_specs=[pl.BlockSpec((bb, tt, W_GROUP), lambda b, t: (b, t, 0)), state],
        out_shape=[jax.ShapeDtypeStruct((bsz, length, W_GROUP), BF16),
                   jax.ShapeDtypeStruct(buf.shape, F32)],
        scratch_shapes=[pltpu.VMEM((bb, C_WIDTH - 1, W_GROUP), F32)],
        compiler_params=_cparams(("parallel", "arbitrary")),
        name="short_conv",
    )(p3d, p3d, p3d, buf, w)


def _group_dot(xb, ones_bd):
    wb = ones_bd.shape[0]
    return jnp.concatenate(
        [jnp.dot(xb[:, i * wb:(i + 1) * wb], ones_bd, preferred_element_type=F32)
         for i in range(W_GROUP // wb)], axis=1)


def _head_sum(x, ones_bd):
    hi = x.astype(BF16)
    lo = (x - hi.astype(F32)).astype(BF16)
    return _group_dot(hi, ones_bd) + _group_dot(lo, ones_bd)


def _rwkv_body(r_ref, k_ref, v_ref, lo_ref, sh_ref, s0_ref,
               mu_ref, mulo_ref, w0_ref, w2_ref, a0_ref, a2_ref, g2_ref, kk_ref, ka_ref, rk_ref,
               lnw_ref, lnb_ref, ones_ref,
               y_ref, so_ref,
               s_ref, carry_ref, carrylo_ref, rs, ws, ks, vs, kks, kas, ys, *, bb, tt, n_t, gsz, unroll):
    tb = pl.program_id(1)
    n = bb * tt
    w3 = (bb, tt, W_GROUP)
    ones_bd = ones_ref[...]

    @pl.when(tb == 0)
    def _():
        for b in range(bb):
            s_ref[b] = jnp.concatenate([s0_ref[b, h] for h in range(K_HEADS)], axis=1)
        carry_ref[...] = sh_ref[:, :, 0:3 * W_GROUP]
        carrylo_ref[...] = sh_ref[:, :, 3 * W_GROUP:K_COLS]

    tpos = lax.broadcasted_iota(jnp.int32, w3, 1)
    tpos_lo = lax.broadcasted_iota(jnp.int32, (bb, tt, LORA_COLS), 1)

    def shifted(x, carry, mu, mask):
        prev = pltpu.roll(x.reshape(n, x.shape[-1]), 1, axis=0).reshape(x.shape)
        prev = jnp.where(mask == 0, carry, prev)
        return (x + (prev - x) * mu).reshape(n, x.shape[-1])

    r_in, k_in, v_in, lo_in = r_ref[...], k_ref[...], v_ref[...], lo_ref[...]
    r = shifted(r_in, carry_ref[:, :, 0:W_GROUP], mu_ref[:, 0:W_GROUP], tpos)
    k = shifted(k_in, carry_ref[:, :, W_GROUP:2 * W_GROUP], mu_ref[:, W_GROUP:2 * W_GROUP], tpos)
    v = shifted(v_in, carry_ref[:, :, 2 * W_GROUP:3 * W_GROUP], mu_ref[:, 2 * W_GROUP:3 * W_GROUP], tpos)
    lo = shifted(lo_in, carrylo_ref[...], mulo_ref[...], tpos_lo)
    carry_ref[:, :, 0:W_GROUP] = r_ref[:, tt - 1:tt, :]
    carry_ref[:, :, W_GROUP:2 * W_GROUP] = k_ref[:, tt - 1:tt, :]
    carry_ref[:, :, 2 * W_GROUP:3 * W_GROUP] = v_ref[:, tt - 1:tt, :]
    carrylo_ref[...] = lo_ref[:, tt - 1:tt, :]

    zw = w0_ref[...] + jnp.dot(jnp.tanh(lo).astype(BF16), w2_ref[...], preferred_element_type=F32)
    nz = -zw
    softplus = jnp.maximum(nz, 0.0) + jnp.log1p(jnp.exp(-jnp.abs(nz)))
    decay = jnp.exp(-jnp.exp(-softplus - 0.5))
    a = jax.nn.sigmoid(a0_ref[...] + jnp.dot(lo.astype(BF16), a2_ref[...], preferred_element_type=F32))
    gate = jnp.dot(jax.nn.sigmoid(lo).astype(BF16), g2_ref[...], preferred_element_type=F32)
    kk = k * kk_ref[...]
    kk = kk / jnp.maximum(jnp.sqrt(_head_sum(kk * kk, ones_bd)), 1e-12)
    k2 = k * (1.0 + (a - 1.0) * ka_ref[...])
    bonus = _head_sum(r * k2 * rk_ref[...], ones_bd) * v

    rs[...] = r.reshape(w3)
    ws[...] = decay.reshape(w3)
    ks[...] = k2.reshape(w3)
    vs[...] = v.reshape(w3)
    kks[...] = kk.reshape(w3)
    kas[...] = (kk * a).reshape(w3)

    s2 = (K_HD, W_GROUP)
    diag = (lax.broadcasted_iota(jnp.int32, s2, 0)
            == lax.broadcasted_iota(jnp.int32, s2, 1) % K_HD)

    def read_out(ybc):
        return jnp.sum(jnp.where(diag, ybc, 0.0), axis=0, keepdims=True)

    def step(t, carry):
        at = pl.ds(t, 1)
        prev = pl.ds(jnp.maximum(t - 1, 0), 1)
        for g0 in range(0, bb, gsz):
            parts = []
            for b in range(g0, g0 + gsz):
                s = s_ref[b]
                parts += [(s * kks[b, at, :]).astype(BF16),
                          jnp.where(diag, vs[b, at, :], 0.0).astype(BF16),
                          (s * rs[b, prev, :]).astype(BF16)]
            res = _group_dot(jnp.concatenate(parts, axis=0), ones_bd)
            for i, b in enumerate(range(g0, g0 + gsz)):
                r0 = 3 * i * K_HD
                sk = res[r0:r0 + K_HD]
                vcol = res[r0 + K_HD:r0 + 2 * K_HD]
                ys[b, prev, :] = read_out(res[r0 + 2 * K_HD:r0 + 3 * K_HD])
                s_ref[b] = s_ref[b] * ws[b, at, :] - sk * kas[b, at, :] + vcol * ks[b, at, :]
        return carry

    lax.fori_loop(0, tt, step, 0, unroll=unroll)
    last = pl.ds(tt - 1, 1)
    for b in range(bb):
        sr = (s_ref[b] * rs[b, last, :]).astype(BF16)
        ys[b, last, :] = read_out(_group_dot(sr, ones_bd))

    y = ys[...].reshape(n, W_GROUP)
    mean = _head_sum(y, ones_bd) * (1.0 / K_HD)
    yc = y - mean
    var = _head_sum(yc * yc, ones_bd) * (1.0 / K_HD)
    out = (yc * lax.rsqrt(var + GN_EPS) * lnw_ref[...] + lnb_ref[...] + bonus) * gate
    y_ref[...] = out.reshape(w3).astype(BF16)

    @pl.when(tb == n_t - 1)
    def _():
        for b in range(bb):
            for h in range(K_HEADS):
                so_ref[b, h] = s_ref[b, :, h * K_HD:(h + 1) * K_HD]


def _rwkv(p3d, shift, s0_all, layer, prm, *, bb, tt, gsz, unroll):
    bsz, length, _ = p3d.shape
    n_t = length // tt

    def col(j):
        return pl.BlockSpec((bb, tt, W_GROUP), lambda b, t: (b, t, j))

    def whole(a):
        nd = a.ndim
        return pl.BlockSpec(a.shape, lambda b, t: (0,) * nd)

    state = pl.BlockSpec((bb, K_HEADS, K_HD, K_HD), lambda b, t: (b, 0, 0, 0))
    params = [prm[nm] for nm in ("mu", "mu_lo", "w0", "w2", "a0", "a2", "g2", "k_k", "k_a", "r_k",
                                 "ln_w", "ln_b", "ones_bd")]
    blk = pltpu.VMEM((bb, tt, W_GROUP), F32)
    return pl.pallas_call(
        functools.partial(_rwkv_body, bb=bb, tt=tt, n_t=n_t, gsz=gsz, unroll=unroll),
        grid=(bsz // bb, n_t),
        in_specs=[col(COL_K_R), col(COL_K_K), col(COL_K_V),
                  pl.BlockSpec((bb, tt, LORA_COLS), lambda b, t: (b, t, COL_K_LORA)),
                  pl.BlockSpec((bb, 1, K_COLS), lambda b, t: (b, 0, 0)),
                  pl.BlockSpec((None, bb, K_HEADS, K_HD, K_HD), lambda b, t: (layer, b, 0, 0, 0))]
                 + [whole(a) for a in params],
        out_specs=[pl.BlockSpec((bb, tt, W_GROUP), lambda b, t: (b, t, 0)), state],
        out_shape=[jax.ShapeDtypeStruct((bsz, length, W_GROUP), BF16),
                   jax.ShapeDtypeStruct(s0_all.shape[1:], F32)],
        scratch_shapes=[pltpu.VMEM((bb, K_HD, W_GROUP), F32),
                        pltpu.VMEM((bb, 1, 3 * W_GROUP), F32),
                        pltpu.VMEM((bb, 1, LORA_COLS), F32),
                        blk, blk, blk, blk, blk, blk, blk],
        compiler_params=_cparams(("parallel", "arbitrary")),
        name="rwkv7",
    )(p3d, p3d, p3d, p3d, shift, s0_all, *params)


def _rwkvc_body(r_ref, k_ref, v_ref, lo_ref, sh_ref, s0_ref,
                mu_ref, mulo_ref, w0_ref, w2_ref, a0_ref, a2_ref, g2_ref, kk_ref, ka_ref, rk_ref,
                lnw_ref, lnb_ref, ones_ref,
                y_ref, so_ref,
                s2_ref, carry_ref, carrylo_ref, pt_s, rt_s, qh_s, kh_s, qb_s, kb_s, v_s, ec_s, ys,
                *, bb, tt, n_t, gsz, chunk):
    tb = pl.program_id(1)
    n = bb * tt
    w3 = (bb, tt, W_GROUP)
    ones_bd = ones_ref[...]
    n_pair = K_HEADS // 2
    pair_w = 2 * K_HD
    zero_blk = jnp.zeros((K_HD, K_HD), F32)

    @pl.when(tb == 0)
    def _():
        for b in range(bb):
            for p in range(n_pair):
                top = jnp.concatenate([s0_ref[b, 2 * p], zero_blk], axis=1)
                bot = jnp.concatenate([zero_blk, s0_ref[b, 2 * p + 1]], axis=1)
                s2_ref[b, p] = jnp.concatenate([top, bot], axis=0)
        carry_ref[...] = sh_ref[:, :, 0:3 * W_GROUP]
        carrylo_ref[...] = sh_ref[:, :, 3 * W_GROUP:K_COLS]

    tpos = lax.broadcasted_iota(jnp.int32, w3, 1)
    tpos_lo = lax.broadcasted_iota(jnp.int32, (bb, tt, LORA_COLS), 1)

    def shifted(x, carry, mu, mask):
        prev = pltpu.roll(x.reshape(n, x.shape[-1]), 1, axis=0).reshape(x.shape)
        prev = jnp.where(mask == 0, carry, prev)
        return (x + (prev - x) * mu).reshape(n, x.shape[-1])

    r_in, k_in, v_in, lo_in = r_ref[...], k_ref[...], v_ref[...], lo_ref[...]
    r = shifted(r_in, carry_ref[:, :, 0:W_GROUP], mu_ref[:, 0:W_GROUP], tpos)
    k = shifted(k_in, carry_ref[:, :, W_GROUP:2 * W_GROUP], mu_ref[:, W_GROUP:2 * W_GROUP], tpos)
    v = shifted(v_in, carry_ref[:, :, 2 * W_GROUP:3 * W_GROUP], mu_ref[:, 2 * W_GROUP:3 * W_GROUP], tpos)
    lo = shifted(lo_in, carrylo_ref[...], mulo_ref[...], tpos_lo)
    carry_ref[:, :, 0:W_GROUP] = r_ref[:, tt - 1:tt, :]
    carry_ref[:, :, W_GROUP:2 * W_GROUP] = k_ref[:, tt - 1:tt, :]
    carry_ref[:, :, 2 * W_GROUP:3 * W_GROUP] = v_ref[:, tt - 1:tt, :]
    carrylo_ref[...] = lo_ref[:, tt - 1:tt, :]

    zw = w0_ref[...] + jnp.dot(jnp.tanh(lo).astype(BF16), w2_ref[...], preferred_element_type=F32)
    nz = -zw
    softplus = jnp.maximum(nz, 0.0) + jnp.log1p(jnp.exp(-jnp.abs(nz)))
    logw = -jnp.exp(-softplus - 0.5)
    a = jax.nn.sigmoid(a0_ref[...] + jnp.dot(lo.astype(BF16), a2_ref[...], preferred_element_type=F32))
    gate = jnp.dot(jax.nn.sigmoid(lo).astype(BF16), g2_ref[...], preferred_element_type=F32)
    kk = k * kk_ref[...]
    kk = kk / jnp.maximum(jnp.sqrt(_head_sum(kk * kk, ones_bd)), 1e-12)
    k2 = k * (1.0 + (a - 1.0) * ka_ref[...])
    bonus = _head_sum(r * k2 * rk_ref[...], ones_bd) * v
    q = -(kk * a)

    cpos = lax.broadcasted_iota(jnp.int32, (n, W_GROUP), 0) % chunk
    g = logw
    step = 1
    while step < chunk:
        g = g + jnp.where(cpos >= step, pltpu.roll(g, step, axis=0), 0.0)
        step *= 2
    g3 = g.reshape(n // chunk, chunk, W_GROUP)
    gtot = jnp.broadcast_to(g3[:, chunk - 1:chunk, :], g3.shape).reshape(n, W_GROUP)
    e_neg = jnp.exp(-g)
    e_rem = jnp.exp(gtot - g)
    pt_s[...] = (kk * jnp.exp(g - logw)).reshape(w3)
    rt_s[...] = (r * jnp.exp(g)).reshape(w3)
    qh_s[...] = (q * e_neg).reshape(w3)
    kh_s[...] = (k2 * e_neg).reshape(w3)
    qb_s[...] = (q * e_rem).reshape(w3)
    kb_s[...] = (k2 * e_rem).reshape(w3)
    v_s[...] = v.reshape(w3)
    ec_s[...] = jnp.exp(gtot).reshape(w3)

    c2, c4 = 2 * chunk, 4 * chunk
    even = lax.broadcasted_iota(jnp.int32, (chunk, pair_w), 1) < K_HD
    ri = lax.broadcasted_iota(jnp.int32, (c4, c4), 0)
    ci = lax.broadcasted_iota(jnp.int32, (c4, c4), 1)
    keep = ci % chunk < ri % chunk + ri // c2
    right = lax.broadcasted_iota(jnp.int32, (c2, c4), 1) >= c2
    eye = (lax.broadcasted_iota(jnp.int32, (c2, c2), 0)
           == lax.broadcasted_iota(jnp.int32, (c2, c2), 1)).astype(F32)
    nt = (((1,), (1,)), ((), ()))
    tn = (((0,), (0,)), ((), ()))
    n_double = chunk.bit_length() - 2

    def two(x):
        return jnp.concatenate([jnp.where(even, x, 0.0), jnp.where(even, 0.0, x)], axis=0)

    def mm(x, y):
        return jnp.dot(x.astype(BF16), y.astype(BF16), preferred_element_type=F32)

    def units(args):
        v2, pr, qk, qkb, s2, ec = zip(*args)
        idx = range(len(args))
        apr = [jnp.where(keep, lax.dot_general(pr[i], qk[i], nt, preferred_element_type=F32), 0.0) for i in idx]
        prs = [lax.dot_general(pr[i], s2[i].astype(BF16), nt, preferred_element_type=F32) for i in idx]
        apk = [mm(jnp.where(right, apr[i][0:c2], 0.0), jnp.concatenate([v2[i], v2[i]], axis=0)) for i in idx]
        power = [apr[i][0:c2, 0:c2] for i in idx]
        inv = [eye + power[i] for i in idx]
        for _ in range(n_double):
            power = [mm(power[i], power[i]) for i in idx]
            inv = [inv[i] + mm(inv[i], power[i]) for i in idx]
        u2 = [mm(inv[i], prs[i][0:c2] + apk[i]) for i in idx]
        uv = [jnp.concatenate([u2[i], v2[i]], axis=0).astype(BF16) for i in idx]
        y2 = [prs[i][c2:c4] + jnp.dot(apr[i][c2:c4].astype(BF16), uv[i], preferred_element_type=F32)
              for i in idx]
        s_new = [s2[i] * ec[i] + lax.dot_general(uv[i], qkb[i], tn, preferred_element_type=F32) for i in idx]
        return [(y2[i][0:chunk] + y2[i][chunk:c2], s_new[i]) for i in idx]

    def chunk_step(i, carry):
        c = i // (bb // gsz)
        b0 = (i % (bb // gsz)) * gsz
        r0 = pl.multiple_of(c * chunk, chunk)
        rows = pl.ds(r0, chunk)
        where = [(b0 + j, slice(p * pair_w, (p + 1) * pair_w), p) for j in range(gsz) for p in range(n_pair)]
        args = []
        for b, lanes, p in where:
            ld = lambda ref: two(ref[b, rows, lanes])
            args.append((ld(v_s),
                         jnp.concatenate([ld(pt_s), ld(rt_s)], axis=0).astype(BF16),
                         jnp.concatenate([ld(qh_s), ld(kh_s)], axis=0).astype(BF16),
                         jnp.concatenate([ld(qb_s), ld(kb_s)], axis=0).astype(BF16),
                         s2_ref[b, p], ec_s[b, pl.ds(r0, 1), lanes]))
        for (b, lanes, p), (y, s_new) in zip(where, units(args)):
            ys[b, rows, lanes] = y
            s2_ref[b, p] = s_new
        return carry

    lax.fori_loop(0, (tt // chunk) * (bb // gsz), chunk_step, 0)

    y = ys[...].reshape(n, W_GROUP)
    mean = _head_sum(y, ones_bd) * (1.0 / K_HD)
    yc = y - mean
    var = _head_sum(yc * yc, ones_bd) * (1.0 / K_HD)
    out = (yc * lax.rsqrt(var + GN_EPS) * lnw_ref[...] + lnb_ref[...] + bonus) * gate
    y_ref[...] = out.reshape(w3).astype(BF16)

    @pl.when(tb == n_t - 1)
    def _():
        for b in range(bb):
            for p in range(n_pair):
                so_ref[b, 2 * p] = s2_ref[b, p, 0:K_HD, 0:K_HD]
                so_ref[b, 2 * p + 1] = s2_ref[b, p, K_HD:pair_w, K_HD:pair_w]


def _rwkvc(p3d, shift, s0_all, layer, prm, *, bb, tt, gsz, chunk):
    bsz, length, _ = p3d.shape
    n_t = length // tt

    def col(j):
        return pl.BlockSpec((bb, tt, W_GROUP), lambda b, t: (b, t, j))

    def whole(a):
        nd = a.ndim
        return pl.BlockSpec(a.shape, lambda b, t: (0,) * nd)

    state = pl.BlockSpec((bb, K_HEADS, K_HD, K_HD), lambda b, t: (b, 0, 0, 0))
    params = [prm[nm] for nm in ("mu", "mu_lo", "w0", "w2", "a0", "a2", "g2", "k_k", "k_a", "r_k",
                                 "ln_w", "ln_b", "ones_bd")]
    blk = pltpu.VMEM((bb, tt, W_GROUP), F32)
    return pl.pallas_call(
        functools.partial(_rwkvc_body, bb=bb, tt=tt, n_t=n_t, gsz=gsz, chunk=chunk),
        grid=(bsz // bb, n_t),
        in_specs=[col(COL_K_R), col(COL_K_K), col(COL_K_V),
                  pl.BlockSpec((bb, tt, LORA_COLS), lambda b, t: (b, t, COL_K_LORA)),
                  pl.BlockSpec((bb, 1, K_COLS), lambda b, t: (b, 0, 0)),
                  pl.BlockSpec((None, bb, K_HEADS, K_HD, K_HD), lambda b, t: (layer, b, 0, 0, 0))]
                 + [whole(a) for a in params],
        out_specs=[pl.BlockSpec((bb, tt, W_GROUP), lambda b, t: (b, t, 0)), state],
        out_shape=[jax.ShapeDtypeStruct((bsz, length, W_GROUP), BF16),
                   jax.ShapeDtypeStruct(s0_all.shape[1:], F32)],
        scratch_shapes=[pltpu.VMEM((bb, K_HEADS // 2, 2 * K_HD, 2 * K_HD), F32),
                        pltpu.VMEM((bb, 1, 3 * W_GROUP), F32),
                        pltpu.VMEM((bb, 1, LORA_COLS), F32),
                        blk, blk, blk, blk, blk, blk, blk, blk, blk],
        compiler_params=_cparams(("parallel", "arbitrary")),
        name="rwkv7c",
    )(p3d, p3d, p3d, p3d, shift, s0_all, *params)


def _gate_mixing(w_s, b_s, seq):
    cl = min(A_CHUNK, seq)
    wm = jnp.tril(w_s[:, :cl, :cl])
    bias = b_s[:, :cl]
    rep = A_CHUNK // cl
    if rep > 1:
        eye = jnp.eye(rep, dtype=w_s.dtype)
        wm = jnp.einsum("ab,hts->hatbs", eye, wm).reshape(w_s.shape[0], A_CHUNK, A_CHUNK)
        bias = jnp.tile(bias, (1, rep))
    bias = jnp.repeat(bias.T, A_CHUNK, axis=1)
    return wm.astype(BF16), bias


def _pad_rows(w, row0):
    return jnp.zeros((LORA_COLS, W_GROUP), F32).at[row0:row0 + w.shape[0]].set(w).astype(BF16)


def _layer_params(l, ffn1_norm, ffn1_w_gate, ffn1_w_up, ffn1_w_down, mix_norm, w_in, w_out,
                  a_w_s, a_b_s, a_ln_g, a_ln_b, c_conv_w,
                  k_mu, k_w0, k_w2, k_a0, k_a2, k_g2, k_k_k, k_k_a, k_r_k, k_ln_w, k_ln_b,
                  ffn2_norm, ffn2_w_gate, ffn2_w_up, ffn2_w_down):
    row = lambda a: a[l].reshape(1, -1)
    head_of = np.arange(V7X_MXU_DIM) // K_HD
    ones_bd = jnp.asarray(head_of[:, None] == head_of[None, :], BF16)
    rwkv = dict(
        mu=k_mu[l][None, :3 * W_GROUP], mu_lo=k_mu[l][None, 3 * W_GROUP:],
        w0=row(k_w0), w2=_pad_rows(k_w2[l], 0),
        a0=row(k_a0), a2=_pad_rows(k_a2[l], W_LORA),
        g2=_pad_rows(k_g2[l], W_LORA + A_LORA),
        k_k=row(k_k_k), k_a=row(k_k_a), r_k=row(k_r_k), ln_w=row(k_ln_w), ln_b=row(k_ln_b),
        ones_bd=ones_bd)
    return dict(
        layer=l,
        ffn1=(row(ffn1_norm), ffn1_w_gate, ffn1_w_up, ffn1_w_down),
        ffn2=(row(ffn2_norm), ffn2_w_gate, ffn2_w_up, ffn2_w_down),
        mix_norm=row(mix_norm), w_in=w_in, w_out=w_out,
        a_w_s=a_w_s[l], a_b_s=a_b_s[l], a_ln_g=row(a_ln_g), a_ln_b=row(a_ln_b),
        conv_w=c_conv_w[l], rwkv=rwkv)


def _stream_layer(x2d, bsz, length, pos0, ret_s0_all, ret_layer, conv_buf, rw_shift, rw_s0_all, p, fn, *,
                  final, cfg):
    layer = p["layer"]
    x1 = _ffn(x2d, *p["ffn1"], fn, layer=layer, final=False, tm=cfg["tm"], tf=cfg["tf"])
    proj = _proj(x1, p["mix_norm"], p["w_in"], layer=layer, tm=cfg["proj_tm"], tn=cfg["tn"])
    p3d = proj.reshape(bsz, length, proj.shape[1])

    wm, bias = _gate_mixing(p["a_w_s"], p["a_b_s"], length)
    ya, v_rows = _gate(proj, wm, bias, p["a_ln_g"], p["a_ln_b"], tm=cfg["gate_tm"],
                       with_rows=cfg["with_rows"])
    yb, ret_s = _retention(p3d, ret_s0_all, ret_layer, pos0, bb=cfg["ret_bb"])
    yc, conv_new = _conv(p3d, conv_buf, p["conv_w"], bb=cfg["conv_bb"], tt=cfg["conv_tt"])
    yd, rw_s = _rwkvc(p3d, rw_shift[:, None, :], rw_s0_all, ret_layer, p["rwkv"],
                      bb=cfg["rwkv_bb"], tt=cfg["rwkv_tt"], gsz=cfg["rwkv_gsz"], chunk=min(16, length))
    shift_new = p3d[:, length - 1, K_COL0:K_COL0 + K_COLS]

    flat = lambda y: y.reshape(bsz * length, W_GROUP)
    x2 = _outproj(x1, ya, flat(yb), flat(yc), flat(yd), p["w_out"], layer=layer, tm=cfg["tm"])
    x3 = _ffn(x2, *p["ffn2"], fn, layer=layer, final=final, tm=cfg["tm"], tf=cfg["tf"])
    return x3, ret_s, conv_new, shift_new, rw_s, v_rows


def _stream_cfg(bsz, length, sample):
    m = bsz * length
    tm = min(512, m)
    if sample:
        return dict(tm=tm, proj_tm=min(1024, m), tf=512, tn=1280, gate_tm=min(512, m), with_rows=True,
                    ret_bb=8, conv_bb=32, conv_tt=length, rwkv_bb=16, rwkv_tt=length, rwkv_gsz=4, rwkv_unroll=1)
    return dict(tm=tm, proj_tm=min(1024, m), tf=512, tn=1280, gate_tm=min(512, m), with_rows=False,
                ret_bb=1, conv_bb=1, conv_tt=min(512, length), rwkv_bb=bsz, rwkv_tt=min(128, length), rwkv_gsz=4, rwkv_unroll=2)


def kernel(x_prompt, x_sample, state_ret, state_conv, state_rwkv_shift, state_rwkv, ffn1_norm, ffn1_w_gate, ffn1_w_up, ffn1_w_down, mix_norm, w_in, w_out, a_w_s, a_b_s, a_ln_g, a_ln_b, c_conv_w, k_mu, k_w0, k_w2, k_a0, k_a2, k_g2, k_k_k, k_k_a, k_r_k, k_ln_w, k_ln_b, ffn2_norm, ffn2_w_gate, ffn2_w_up, ffn2_w_down, final_norm):
    bp, lp, d = x_prompt.shape
    bs, ls, _ = x_sample.shape
    depth = ffn1_norm.shape[0]
    cfg_p = _stream_cfg(bp, lp, sample=False)
    cfg_s = _stream_cfg(bs, ls, sample=True)
    fn = final_norm.reshape(1, d)

    zero_ret = jnp.zeros((1, bp, R_HEADS, R_HD, R_HD), F32)
    zero_conv = jnp.zeros((bp, C_WIDTH - 1, W_GROUP), F32)
    zero_shift = jnp.zeros((bp, K_COLS), F32)
    zero_rw = jnp.zeros((1, bp, K_HEADS, K_HD, K_HD), F32)

    ffn1_w_gate, ffn1_w_up, ffn1_w_down, w_in, w_out, ffn2_w_gate, ffn2_w_up, ffn2_w_down = (
        w.astype(BF16) for w in (ffn1_w_gate, ffn1_w_up, ffn1_w_down, w_in, w_out,
                                 ffn2_w_gate, ffn2_w_up, ffn2_w_down))
    xp = x_prompt.reshape(bp * lp, d)
    xs = x_sample.reshape(bs * ls, d)
    outs = [[] for _ in range(9)]
    for l in range(depth):
        p = _layer_params(l, ffn1_norm, ffn1_w_gate, ffn1_w_up, ffn1_w_down, mix_norm, w_in, w_out,
                          a_w_s, a_b_s, a_ln_g, a_ln_b, c_conv_w,
                          k_mu, k_w0, k_w2, k_a0, k_a2, k_g2, k_k_k, k_k_a, k_r_k, k_ln_w, k_ln_b,
                          ffn2_norm, ffn2_w_gate, ffn2_w_up, ffn2_w_down)
        final = l == depth - 1
        xp, rp, cp, sp, wp, _ = _stream_layer(xp, bp, lp, 0.0, zero_ret, 0, zero_conv, zero_shift, zero_rw,
                                              p, fn, final=final, cfg=cfg_p)
        xs, rs, cs, ss, ws, vs = _stream_layer(xs, bs, ls, float(PAST_LEN), state_ret, l, state_conv[l],
                                               state_rwkv_shift[l], state_rwkv, p, fn,
                                               final=final, cfg=cfg_s)
        for acc, val in zip(outs, (rp, rs, cp, cs, sp, ss, wp, ws, vs.reshape(bs, ls, W_GROUP))):
            acc.append(val)

    return (xp.reshape(bp, lp, d), xs.reshape(bs, ls, d)) + tuple(jnp.stack(o) for o in outs)
```

```python
import functools

import numpy as np
import jax
import jax.numpy as jnp
from jax import lax
from jax.experimental import pallas as pl
from jax.experimental.pallas import tpu as pltpu

F32 = jnp.float32
BF16 = jnp.bfloat16

W_GROUP = 512
A_CHUNK = 128
R_HEADS = 4
R_HD = 128
R_CHUNK = 128
ROPE_BASE = 10000.0
C_WIDTH = 3
K_HD = 64
K_HEADS = 8
W_LORA = 64
A_LORA = 64
G_LORA = 128
LORA_COLS = W_LORA + A_LORA + G_LORA
K_COLS = 3 * W_GROUP + LORA_COLS
EPS = 1e-6
GN_EPS = 64e-5
PAST_LEN = 16384

COL_A_U, COL_A_V = 0, 1
COL_R_Q, COL_R_K, COL_R_V, COL_R_G = 2, 3, 4, 5
COL_C_B, COL_C_C, COL_C_H = 6, 7, 8
COL_K_R, COL_K_K, COL_K_V = 9, 10, 11
COL_K_LORA = (12 * W_GROUP) // LORA_COLS
K_COL0 = 9 * W_GROUP

V7X_VMEM_LIMIT_BYTES = 56 * 1024 * 1024
RWKV_DECAY_SCALE = float(np.exp(-0.5))
RWKV_CHUNK = 16
V7X_MXU_DIM = 256


def _cparams(sem, vmem=V7X_VMEM_LIMIT_BYTES):
    return pltpu.CompilerParams(dimension_semantics=sem, vmem_limit_bytes=vmem)


def _rms(x, w):
    return x * lax.rsqrt(jnp.mean(x * x, axis=-1, keepdims=True) + EPS) * w


def _ffn_body(x_ref, nw_ref, wg_ref, wu_ref, wd_ref, fn_ref, o_ref, hn_ref, *, n_f, final):
    j = pl.program_id(1)

    @pl.when(j == 0)
    def _():
        hn_ref[...] = _rms(x_ref[...], nw_ref[...]).astype(BF16)
        o_ref[...] = jnp.zeros_like(o_ref)

    h = hn_ref[...]
    g = jnp.dot(h, wg_ref[...], preferred_element_type=F32)
    u = jnp.dot(h, wu_ref[...], preferred_element_type=F32)
    a = (g * jax.nn.sigmoid(g) * u).astype(BF16)
    o_ref[...] += jnp.dot(a, wd_ref[...], preferred_element_type=F32)

    @pl.when(j == n_f - 1)
    def _():
        y = x_ref[...] + 0.5 * o_ref[...]
        if final:
            y = _rms(y, fn_ref[...])
        o_ref[...] = y


def _ffn(x, nw, wg, wu, wd, fn, *, layer, final, tm, tf):
    m, d = x.shape
    f = wg.shape[2]
    n_f = f // tf
    return pl.pallas_call(
        functools.partial(_ffn_body, n_f=n_f, final=final),
        grid=(m // tm, n_f),
        in_specs=[
            pl.BlockSpec((tm, d), lambda i, j: (i, 0)),
            pl.BlockSpec((1, d), lambda i, j: (0, 0)),
            pl.BlockSpec((None, d, tf), lambda i, j: (layer, 0, j)),
            pl.BlockSpec((None, d, tf), lambda i, j: (layer, 0, j)),
            pl.BlockSpec((None, tf, d), lambda i, j: (layer, j, 0)),
            pl.BlockSpec((1, d), lambda i, j: (0, 0)),
        ],
        out_specs=pl.BlockSpec((tm, d), lambda i, j: (i, 0)),
        out_shape=jax.ShapeDtypeStruct((m, d), F32),
        scratch_shapes=[pltpu.VMEM((tm, d), BF16)],
        compiler_params=_cparams(("parallel", "arbitrary")),
        name="ffn_final" if final else "ffn",
    )(x, nw, wg, wu, wd, fn)


def _proj_body(x_ref, nw_ref, w_ref, o_ref, hn_ref):
    @pl.when(pl.program_id(1) == 0)
    def _():
        hn_ref[...] = _rms(x_ref[...], nw_ref[...]).astype(BF16)

    o_ref[...] = jnp.dot(hn_ref[...], w_ref[...], preferred_element_type=F32)


def _proj(x, nw, w, *, layer, tm, tn):
    m, d = x.shape
    n = w.shape[2]
    return pl.pallas_call(
        _proj_body,
        grid=(m // tm, n // tn),
        in_specs=[
            pl.BlockSpec((tm, d), lambda i, j: (i, 0)),
            pl.BlockSpec((1, d), lambda i, j: (0, 0)),
            pl.BlockSpec((None, d, tn), lambda i, j: (layer, 0, j)),
        ],
        out_specs=pl.BlockSpec((tm, tn), lambda i, j: (i, j)),
        out_shape=jax.ShapeDtypeStruct((m, n), F32),
        scratch_shapes=[pltpu.VMEM((tm, d), BF16)],
        compiler_params=_cparams(("parallel", "arbitrary")),
        name="in_proj",
    )(x, nw, w)


def _outproj_body(x_ref, ya_ref, yb_ref, yc_ref, yd_ref, w_ref, o_ref):
    acc = x_ref[...]
    for gi, y_ref in enumerate((ya_ref, yb_ref, yc_ref, yd_ref)):
        acc = acc + jnp.dot(y_ref[...], w_ref[gi * W_GROUP:(gi + 1) * W_GROUP, :],
                            preferred_element_type=F32)
    o_ref[...] = acc


def _outproj(x, ya, yb, yc, yd, w, *, layer, tm):
    m, d = x.shape
    yspec = pl.BlockSpec((tm, W_GROUP), lambda i: (i, 0))
    return pl.pallas_call(
        _outproj_body,
        grid=(m // tm,),
        in_specs=[pl.BlockSpec((tm, d), lambda i: (i, 0)), yspec, yspec, yspec, yspec,
                  pl.BlockSpec((None,) + w.shape[1:], lambda i: (layer, 0, 0))],
        out_specs=pl.BlockSpec((tm, d), lambda i: (i, 0)),
        out_shape=jax.ShapeDtypeStruct((m, d), F32),
        compiler_params=_cparams(("parallel",)),
        name="out_proj",
    )(x, ya, yb, yc, yd, w)


def _gate_body(u_ref, v_ref, wm_ref, bias_ref, g_ref, b_ref, y_ref, *vr_ref, tm):
    gu = jax.nn.gelu(u_ref[...], approximate=True)
    gv = jax.nn.gelu(v_ref[...], approximate=True)
    mu = jnp.mean(gv, axis=-1, keepdims=True)
    var = jnp.mean(jnp.square(gv - mu), axis=-1, keepdims=True)
    vn = (gv - mu) * lax.rsqrt(var + EPS) * g_ref[...] + b_ref[...]
    if vr_ref:
        vr_ref[0][...] = vn
    vnb = vn.astype(BF16)
    for c in range(tm // A_CHUNK):
        rows = slice(c * A_CHUNK, (c + 1) * A_CHUNK)
        for h in range(W_GROUP // A_CHUNK):
            cols = slice(h * A_CHUNK, (h + 1) * A_CHUNK)
            z = jnp.dot(wm_ref[h], vnb[rows, cols], preferred_element_type=F32) + bias_ref[:, cols]
            y_ref[rows, cols] = (gu[rows, cols] * z).astype(BF16)


def _gate(p2d, wm, bias, ln_g, ln_b, *, tm, with_rows):
    m = p2d.shape[0]
    row_spec = pl.BlockSpec((tm, W_GROUP), lambda i: (i, 0))
    out_shape = [jax.ShapeDtypeStruct((m, W_GROUP), BF16)]
    out_specs = [row_spec]
    if with_rows:
        out_shape.append(jax.ShapeDtypeStruct((m, W_GROUP), F32))
        out_specs.append(row_spec)
    res = pl.pallas_call(
        functools.partial(_gate_body, tm=tm),
        grid=(m // tm,),
        in_specs=[
            pl.BlockSpec((tm, W_GROUP), lambda i: (i, COL_A_U)),
            pl.BlockSpec((tm, W_GROUP), lambda i: (i, COL_A_V)),
            pl.BlockSpec(wm.shape, lambda i: (0, 0, 0)),
            pl.BlockSpec(bias.shape, lambda i: (0, 0)),
            pl.BlockSpec((1, W_GROUP), lambda i: (0, 0)),
            pl.BlockSpec((1, W_GROUP), lambda i: (0, 0)),
        ],
        out_specs=out_specs,
        out_shape=out_shape,
        compiler_params=_cparams(("parallel",)),
        name="spatial_gate",
    )(p2d, p2d, wm, bias, ln_g, ln_b)
    return res if with_rows else (res[0], None)


def _ret_body(q_ref, k_ref, v_ref, g_ref, cos_ref, sin_ref, dm_ref, qd_ref, kd_ref, s0_ref,
              y_ref, so_ref, s_ref, *, bb, n_c, chunk_decay):
    c = pl.program_id(1)

    @pl.when(c == 0)
    def _():
        s_ref[...] = s0_ref[...]

    cos = cos_ref[...]
    sin = sin_ref[...]
    nt = (((1,), (1,)), ((), ()))
    tn = (((0,), (0,)), ((), ()))
    units = [(b, h, slice(h * R_HD, (h + 1) * R_HD)) for b in range(bb) for h in range(R_HEADS)]
    idx = range(len(units))

    def rope(x):
        return x * cos + pltpu.roll(x, R_HD // 2, axis=1) * sin

    qr = [rope(q_ref[b, :, cols]) for b, h, cols in units]
    kr = [rope(k_ref[b, :, cols]) * (R_HD ** -0.5) for b, h, cols in units]
    v = [v_ref[b, :, cols].astype(BF16) for b, h, cols in units]
    s = [s_ref[b, h] for b, h, cols in units]
    sc = [lax.dot_general(qr[i].astype(BF16), kr[i].astype(BF16), nt, preferred_element_type=F32)
          * dm_ref[units[i][1]] for i in idx]
    cross = [jnp.dot((qr[i] * qd_ref[:, units[i][2]]).astype(BF16), s[i].astype(BF16),
                     preferred_element_type=F32) for i in idx]
    kv = [lax.dot_general((kr[i] * kd_ref[:, units[i][2]]).astype(BF16), v[i], tn,
                          preferred_element_type=F32) for i in idx]
    o = [cross[i] + jnp.dot(sc[i].astype(BF16), v[i], preferred_element_type=F32) for i in idx]
    for i, (b, h, cols) in enumerate(units):
        s_ref[b, h] = chunk_decay[h] * s[i] + kv[i]
        on = o[i] * lax.rsqrt(jnp.mean(o[i] * o[i], axis=-1, keepdims=True) + EPS)
        g = g_ref[b, :, cols]
        y_ref[b, :, cols] = (on * (g * jax.nn.sigmoid(g))).astype(BF16)

    @pl.when(c == n_c - 1)
    def _():
        so_ref[...] = s_ref[...]


def _ret_tables(cl, pos0, length):
    half = R_HD // 2
    inv = ROPE_BASE ** (-jnp.arange(half, dtype=F32) / half)
    pos = pos0 + jnp.arange(length, dtype=F32)
    ang = pos[:, None] * inv[None, :]
    cos = jnp.cos(ang)
    sin = jnp.sin(ang)
    cos_t = jnp.concatenate([cos, cos], axis=-1)
    sin_t = jnp.concatenate([-sin, sin], axis=-1)
    log_gamma = np.log(1.0 - 2.0 ** (-5.0 - np.arange(R_HEADS, dtype=np.float64)))
    idx = np.arange(cl, dtype=np.float64)
    diff = idx[:, None] - idx[None, :]
    dmat = np.where(diff >= 0, np.exp(np.maximum(diff, 0.0)[None] * log_gamma[:, None, None]), 0.0)
    kdec = np.exp((cl - 1.0 - idx)[:, None] * log_gamma[None, :])
    qdec = np.exp((idx + 1.0)[:, None] * log_gamma[None, :])
    chunk_decay = tuple(float(x) for x in np.exp(cl * log_gamma))
    rep = lambda a: jnp.asarray(np.repeat(a, R_HD, axis=1), F32)
    return cos_t, sin_t, jnp.asarray(dmat, F32), rep(qdec), rep(kdec), chunk_decay


def _retention(p3d, s0_all, layer, pos0, *, bb):
    bsz, length, _ = p3d.shape
    cl = min(R_CHUNK, length)
    n_c = length // cl
    cos_t, sin_t, dmat, qdec, kdec, chunk_decay = _ret_tables(cl, pos0, length)

    def col(j):
        return pl.BlockSpec((bb, cl, W_GROUP), lambda b, c: (b, c, j))

    tab = pl.BlockSpec((cl, R_HD), lambda b, c: (c, 0))
    state = pl.BlockSpec((bb, R_HEADS, R_HD, R_HD), lambda b, c: (b, 0, 0, 0))
    return pl.pallas_call(
        functools.partial(_ret_body, bb=bb, n_c=n_c, chunk_decay=chunk_decay),
        grid=(bsz // bb, n_c),
        in_specs=[col(COL_R_Q), col(COL_R_K), col(COL_R_V), col(COL_R_G), tab, tab,
                  pl.BlockSpec(dmat.shape, lambda b, c: (0, 0, 0)),
                  pl.BlockSpec(qdec.shape, lambda b, c: (0, 0)),
                  pl.BlockSpec(kdec.shape, lambda b, c: (0, 0)),
                  pl.BlockSpec((None, bb, R_HEADS, R_HD, R_HD), lambda b, c: (layer, b, 0, 0, 0))],
        out_specs=[pl.BlockSpec((bb, cl, W_GROUP), lambda b, c: (b, c, 0)), state],
        out_shape=[jax.ShapeDtypeStruct((bsz, length, W_GROUP), BF16),
                   jax.ShapeDtypeStruct(s0_all.shape[1:], F32)],
        scratch_shapes=[pltpu.VMEM((bb, R_HEADS, R_HD, R_HD), F32)],
        compiler_params=_cparams(("parallel", "arbitrary")),
        name="retention",
    )(p3d, p3d, p3d, p3d, cos_t, sin_t, dmat, qdec, kdec, s0_all)


def _conv_body(bg_ref, cg_ref, h_ref, buf_ref, w_ref, y_ref, st_ref, carry_ref, *, bb, tt):
    @pl.when(pl.program_id(1) == 0)
    def _():
        carry_ref[...] = buf_ref[...]

    shape = (bb, tt, W_GROUP)
    z = cg_ref[...] * h_ref[...]
    z2 = z.reshape(bb * tt, W_GROUP)
    r1 = pltpu.roll(z2, 1, axis=0).reshape(shape)
    r2 = pltpu.roll(z2, 2, axis=0).reshape(shape)
    tpos = lax.broadcasted_iota(jnp.int32, shape, 1)
    c0 = carry_ref[:, 0:1, :]
    c1 = carry_ref[:, 1:2, :]
    zm1 = jnp.where(tpos == 0, c1, r1)
    zm2 = jnp.where(tpos == 0, c0, jnp.where(tpos == 1, c1, r2))
    y = w_ref[0:1, :] * zm2 + w_ref[1:2, :] * zm1 + w_ref[2:3, :] * z
    y_ref[...] = (bg_ref[...] * y).astype(BF16)
    new = cg_ref[:, tt - 2:tt, :] * h_ref[:, tt - 2:tt, :]
    carry_ref[...] = new
    st_ref[...] = new


def _conv(p3d, buf, w, *, bb, tt):
    bsz, length, _ = p3d.shape

    def col(j):
        return pl.BlockSpec((bb, tt, W_GROUP), lambda b, t: (b, t, j))

    state = pl.BlockSpec((bb, C_WIDTH - 1, W_GROUP), lambda b, t: (b, 0, 0))
    return pl.pallas_call(
        functools.partial(_conv_body, bb=bb, tt=tt),
        grid=(bsz // bb, length // tt),
        in_specs=[col(COL_C_B), col(COL_C_C), col(COL_C_H), state,
                  pl.BlockSpec(w.shape, lambda b, t: (0, 0))],
        out_specs=[pl.BlockSpec((bb, tt, W_GROUP), lambda b, t: (b, t, 0)), state],
        out_shape=[jax.ShapeDtypeStruct((bsz, length, W_GROUP), BF16),
                   jax.ShapeDtypeStruct(buf.shape, F32)],
        scratch_shapes=[pltpu.VMEM((bb, C_WIDTH - 1, W_GROUP), F32)],
        compiler_params=_cparams(("parallel", "arbitrary")),
        name="short_conv",
    )(p3d, p3d, p3d, buf, w)


def _group_dot(xb, ones_bd):
    wb = ones_bd.shape[0]
    return jnp.concatenate(
        [jnp.dot(xb[:, i * wb:(i + 1) * wb], ones_bd, preferred_element_type=F32)
         for i in range(W_GROUP // wb)], axis=1)


def _head_sum(x, ones_bd):
    hi = x.astype(BF16)
    lo = (x - hi.astype(F32)).astype(BF16)
    return _group_dot(hi, ones_bd) + _group_dot(lo, ones_bd)


def _rwkvc_body(r_ref, k_ref, v_ref, lo_ref, sh_ref, s0_ref,
                mu_ref, mulo_ref, w0_ref, w2_ref, a0_ref, a2_ref, g2_ref, kk_ref, ka_ref, rk_ref,
                lnw_ref, lnb_ref, ones_ref,
                y_ref, so_ref,
                s2_ref, carry_ref, carrylo_ref, pt_s, rt_s, qh_s, kh_s, qb_s, kb_s, v_s, ec_s, ys,
                *, bb, tt, n_t, gsz, chunk):
    tb = pl.program_id(1)
    n = bb * tt
    w3 = (bb, tt, W_GROUP)
    ones_bd = ones_ref[...]
    n_pair = K_HEADS // 2
    pair_w = 2 * K_HD
    zero_blk = jnp.zeros((K_HD, K_HD), F32)

    @pl.when(tb == 0)
    def _():
        for b in range(bb):
            for p in range(n_pair):
                top = jnp.concatenate([s0_ref[b, 2 * p], zero_blk], axis=1)
                bot = jnp.concatenate([zero_blk, s0_ref[b, 2 * p + 1]], axis=1)
                s2_ref[b, p] = jnp.concatenate([top, bot], axis=0)
        carry_ref[...] = sh_ref[:, :, 0:3 * W_GROUP]
        carrylo_ref[...] = sh_ref[:, :, 3 * W_GROUP:K_COLS]

    tpos = lax.broadcasted_iota(jnp.int32, w3, 1)
    tpos_lo = lax.broadcasted_iota(jnp.int32, (bb, tt, LORA_COLS), 1)

    def shifted(x, carry, mu, mask):
        prev = pltpu.roll(x.reshape(n, x.shape[-1]), 1, axis=0).reshape(x.shape)
        prev = jnp.where(mask == 0, carry, prev)
        return (x + (prev - x) * mu).reshape(n, x.shape[-1])

    r_in, k_in, v_in, lo_in = r_ref[...], k_ref[...], v_ref[...], lo_ref[...]
    r = shifted(r_in, carry_ref[:, :, 0:W_GROUP], mu_ref[:, 0:W_GROUP], tpos)
    k = shifted(k_in, carry_ref[:, :, W_GROUP:2 * W_GROUP], mu_ref[:, W_GROUP:2 * W_GROUP], tpos)
    v = shifted(v_in, carry_ref[:, :, 2 * W_GROUP:3 * W_GROUP], mu_ref[:, 2 * W_GROUP:3 * W_GROUP], tpos)
    lo = shifted(lo_in, carrylo_ref[...], mulo_ref[...], tpos_lo)
    carry_ref[:, :, 0:W_GROUP] = r_ref[:, tt - 1:tt, :]
    carry_ref[:, :, W_GROUP:2 * W_GROUP] = k_ref[:, tt - 1:tt, :]
    carry_ref[:, :, 2 * W_GROUP:3 * W_GROUP] = v_ref[:, tt - 1:tt, :]
    carrylo_ref[...] = lo_ref[:, tt - 1:tt, :]

    zw = w0_ref[...] + jnp.dot(jnp.tanh(lo).astype(BF16), w2_ref[...], preferred_element_type=F32)
    logw = -RWKV_DECAY_SCALE * jax.nn.sigmoid(zw)
    a = jax.nn.sigmoid(a0_ref[...] + jnp.dot(lo.astype(BF16), a2_ref[...], preferred_element_type=F32))
    gate = jnp.dot(jax.nn.sigmoid(lo).astype(BF16), g2_ref[...], preferred_element_type=F32)
    kk = k * kk_ref[...]
    kk = kk * lax.rsqrt(jnp.maximum(_head_sum(kk * kk, ones_bd), 1e-24))
    k2 = k * (1.0 + (a - 1.0) * ka_ref[...])
    bonus = _head_sum(r * k2 * rk_ref[...], ones_bd) * v
    q = -(kk * a)

    cpos = lax.broadcasted_iota(jnp.int32, (n, W_GROUP), 0) % chunk
    g = logw
    step = 1
    while step < chunk:
        g = g + jnp.where(cpos >= step, pltpu.roll(g, step, axis=0), 0.0)
        step *= 2
    g3 = g.reshape(n // chunk, chunk, W_GROUP)
    gtot = jnp.broadcast_to(g3[:, chunk - 1:chunk, :], g3.shape).reshape(n, W_GROUP)
    e_neg = jnp.exp(-g)
    e_rem = jnp.exp(gtot - g)
    pt_s[...] = (kk * jnp.exp(g - logw)).reshape(w3)
    rt_s[...] = (r * jnp.exp(g)).reshape(w3)
    qh_s[...] = (q * e_neg).reshape(w3)
    kh_s[...] = (k2 * e_neg).reshape(w3)
    qb_s[...] = (q * e_rem).reshape(w3)
    kb_s[...] = (k2 * e_rem).reshape(w3)
    v_s[...] = v.reshape(w3)
    ec_s[...] = jnp.exp(gtot).reshape(w3)

    c2, c4 = 2 * chunk, 4 * chunk
    even = lax.broadcasted_iota(jnp.int32, (chunk, pair_w), 1) < K_HD
    ri = lax.broadcasted_iota(jnp.int32, (c4, c4), 0)
    ci = lax.broadcasted_iota(jnp.int32, (c4, c4), 1)
    keep = ci % chunk < ri % chunk + ri // c2
    right = lax.broadcasted_iota(jnp.int32, (c2, c4), 1) >= c2
    eye = (lax.broadcasted_iota(jnp.int32, (c2, c2), 0)
           == lax.broadcasted_iota(jnp.int32, (c2, c2), 1)).astype(F32)
    nt = (((1,), (1,)), ((), ()))
    tn = (((0,), (0,)), ((), ()))
    n_double = chunk.bit_length() - 2

    def two(x):
        return jnp.concatenate([jnp.where(even, x, 0.0), jnp.where(even, 0.0, x)], axis=0)

    def mm(x, y):
        return jnp.dot(x.astype(BF16), y.astype(BF16), preferred_element_type=F32)

    def units(args):
        v2, pr, qk, qkb, s2, ec = zip(*args)
        idx = range(len(args))
        apr = [jnp.where(keep, lax.dot_general(pr[i], qk[i], nt, preferred_element_type=F32), 0.0) for i in idx]
        prs = [lax.dot_general(pr[i], s2[i].astype(BF16), nt, preferred_element_type=F32) for i in idx]
        apk = [mm(jnp.where(right, apr[i][0:c2], 0.0), jnp.concatenate([v2[i], v2[i]], axis=0)) for i in idx]
        power = [apr[i][0:c2, 0:c2] for i in idx]
        inv = [eye + power[i] for i in idx]
        for _ in range(n_double):
            power = [mm(power[i], power[i]) for i in idx]
            inv = [inv[i] + mm(inv[i], power[i]) for i in idx]
        u2 = [mm(inv[i], prs[i][0:c2] + apk[i]) for i in idx]
        uv = [jnp.concatenate([u2[i], v2[i]], axis=0).astype(BF16) for i in idx]
        y2 = [prs[i][c2:c4] + jnp.dot(apr[i][c2:c4].astype(BF16), uv[i], preferred_element_type=F32)
              for i in idx]
        s_new = [s2[i] * ec[i] + lax.dot_general(uv[i], qkb[i], tn, preferred_element_type=F32) for i in idx]
        return [(y2[i][0:chunk] + y2[i][chunk:c2], s_new[i]) for i in idx]

    def chunk_step(i, carry):
        c = i // (bb // gsz)
        b0 = (i % (bb // gsz)) * gsz
        r0 = pl.multiple_of(c * chunk, chunk)
        rows = pl.ds(r0, chunk)
        where = [(b0 + j, slice(p * pair_w, (p + 1) * pair_w), p) for j in range(gsz) for p in range(n_pair)]
        args = []
        for b, lanes, p in where:
            ld = lambda ref: two(ref[b, rows, lanes])
            args.append((ld(v_s),
                         jnp.concatenate([ld(pt_s), ld(rt_s)], axis=0).astype(BF16),
                         jnp.concatenate([ld(qh_s), ld(kh_s)], axis=0).astype(BF16),
                         jnp.concatenate([ld(qb_s), ld(kb_s)], axis=0).astype(BF16),
                         s2_ref[b, p], ec_s[b, pl.ds(r0, 1), lanes]))
        for (b, lanes, p), (y, s_new) in zip(where, units(args)):
            ys[b, rows, lanes] = y
            s2_ref[b, p] = s_new
        return carry

    lax.fori_loop(0, (tt // chunk) * (bb // gsz), chunk_step, 0)

    y = ys[...].reshape(n, W_GROUP)
    mean = _head_sum(y, ones_bd) * (1.0 / K_HD)
    yc = y - mean
    var = _head_sum(yc * yc, ones_bd) * (1.0 / K_HD)
    out = (yc * lax.rsqrt(var + GN_EPS) * lnw_ref[...] + lnb_ref[...] + bonus) * gate
    y_ref[...] = out.reshape(w3).astype(BF16)

    @pl.when(tb == n_t - 1)
    def _():
        for b in range(bb):
            for p in range(n_pair):
                so_ref[b, 2 * p] = s2_ref[b, p, 0:K_HD, 0:K_HD]
                so_ref[b, 2 * p + 1] = s2_ref[b, p, K_HD:pair_w, K_HD:pair_w]


def _rwkvc(p3d, shift, s0_all, layer, prm, *, bb, tt, gsz, chunk):
    bsz, length, _ = p3d.shape
    n_t = length // tt

    def col(j):
        return pl.BlockSpec((bb, tt, W_GROUP), lambda b, t: (b, t, j))

    def whole(a):
        nd = a.ndim
        return pl.BlockSpec(a.shape, lambda b, t: (0,) * nd)

    state = pl.BlockSpec((bb, K_HEADS, K_HD, K_HD), lambda b, t: (b, 0, 0, 0))
    params = [prm[nm] for nm in ("mu", "mu_lo", "w0", "w2", "a0", "a2", "g2", "k_k", "k_a", "r_k",
                                 "ln_w", "ln_b", "ones_bd")]
    blk = pltpu.VMEM((bb, tt, W_GROUP), F32)
    return pl.pallas_call(
        functools.partial(_rwkvc_body, bb=bb, tt=tt, n_t=n_t, gsz=gsz, chunk=chunk),
        grid=(bsz // bb, n_t),
        in_specs=[col(COL_K_R), col(COL_K_K), col(COL_K_V),
                  pl.BlockSpec((bb, tt, LORA_COLS), lambda b, t: (b, t, COL_K_LORA)),
                  pl.BlockSpec((bb, 1, K_COLS), lambda b, t: (b, 0, 0)),
                  pl.BlockSpec((None, bb, K_HEADS, K_HD, K_HD), lambda b, t: (layer, b, 0, 0, 0))]
                 + [whole(a) for a in params],
        out_specs=[pl.BlockSpec((bb, tt, W_GROUP), lambda b, t: (b, t, 0)), state],
        out_shape=[jax.ShapeDtypeStruct((bsz, length, W_GROUP), BF16),
                   jax.ShapeDtypeStruct(s0_all.shape[1:], F32)],
        scratch_shapes=[pltpu.VMEM((bb, K_HEADS // 2, 2 * K_HD, 2 * K_HD), F32),
                        pltpu.VMEM((bb, 1, 3 * W_GROUP), F32),
                        pltpu.VMEM((bb, 1, LORA_COLS), F32),
                        blk, blk, blk, blk, blk, blk, blk, blk, blk],
        compiler_params=_cparams(("parallel", "arbitrary")),
        name="rwkv7c",
    )(p3d, p3d, p3d, p3d, shift, s0_all, *params)


def _gate_mixing(w_s, b_s, seq):
    cl = min(A_CHUNK, seq)
    wm = jnp.tril(w_s[:, :cl, :cl])
    bias = b_s[:, :cl]
    rep = A_CHUNK // cl
    if rep > 1:
        eye = jnp.eye(rep, dtype=w_s.dtype)
        wm = jnp.einsum("ab,hts->hatbs", eye, wm).reshape(w_s.shape[0], A_CHUNK, A_CHUNK)
        bias = jnp.tile(bias, (1, rep))
    bias = jnp.repeat(bias.T, A_CHUNK, axis=1)
    return wm.astype(BF16), bias


def _pad_rows(w, row0):
    return jnp.zeros((LORA_COLS, W_GROUP), F32).at[row0:row0 + w.shape[0]].set(w).astype(BF16)


def _layer_params(l, ffn1_norm, ffn1_w_gate, ffn1_w_up, ffn1_w_down, mix_norm, w_in, w_out,
                  a_w_s, a_b_s, a_ln_g, a_ln_b, c_conv_w,
                  k_mu, k_w0, k_w2, k_a0, k_a2, k_g2, k_k_k, k_k_a, k_r_k, k_ln_w, k_ln_b,
                  ffn2_norm, ffn2_w_gate, ffn2_w_up, ffn2_w_down):
    row = lambda a: a[l].reshape(1, -1)
    head_of = np.arange(V7X_MXU_DIM) // K_HD
    ones_bd = jnp.asarray(head_of[:, None] == head_of[None, :], BF16)
    rwkv = dict(
        mu=k_mu[l][None, :3 * W_GROUP], mu_lo=k_mu[l][None, 3 * W_GROUP:],
        w0=row(k_w0), w2=_pad_rows(k_w2[l], 0),
        a0=row(k_a0), a2=_pad_rows(k_a2[l], W_LORA),
        g2=_pad_rows(k_g2[l], W_LORA + A_LORA),
        k_k=row(k_k_k), k_a=row(k_k_a), r_k=row(k_r_k), ln_w=row(k_ln_w), ln_b=row(k_ln_b),
        ones_bd=ones_bd)
    return dict(
        layer=l,
        ffn1=(row(ffn1_norm), ffn1_w_gate, ffn1_w_up, ffn1_w_down),
        ffn2=(row(ffn2_norm), ffn2_w_gate, ffn2_w_up, ffn2_w_down),
        mix_norm=row(mix_norm), w_in=w_in, w_out=w_out,
        a_w_s=a_w_s[l], a_b_s=a_b_s[l], a_ln_g=row(a_ln_g), a_ln_b=row(a_ln_b),
        conv_w=c_conv_w[l], rwkv=rwkv)


def _stream_layer(x2d, bsz, length, pos0, ret_s0_all, ret_layer, conv_buf, rw_shift, rw_s0_all, p, fn, *,
                  final, cfg):
    layer = p["layer"]
    x1 = _ffn(x2d, *p["ffn1"], fn, layer=layer, final=False, tm=cfg["ffn_tm"], tf=cfg["tf"])
    proj = _proj(x1, p["mix_norm"], p["w_in"], layer=layer, tm=cfg["proj_tm"], tn=cfg["tn"])
    p3d = proj.reshape(bsz, length, proj.shape[1])

    wm, bias = _gate_mixing(p["a_w_s"], p["a_b_s"], length)
    ya, v_rows = _gate(proj, wm, bias, p["a_ln_g"], p["a_ln_b"], tm=cfg["gate_tm"],
                       with_rows=cfg["with_rows"])
    yb, ret_s = _retention(p3d, ret_s0_all, ret_layer, pos0, bb=cfg["ret_bb"])
    yc, conv_new = _conv(p3d, conv_buf, p["conv_w"], bb=cfg["conv_bb"], tt=cfg["conv_tt"])
    yd, rw_s = _rwkvc(p3d, rw_shift[:, None, :], rw_s0_all, ret_layer, p["rwkv"],
                      bb=cfg["rwkv_bb"], tt=cfg["rwkv_tt"], gsz=cfg["rwkv_gsz"], chunk=cfg["rwkv_chunk"])
    shift_new = p3d[:, length - 1, K_COL0:K_COL0 + K_COLS]

    flat = lambda y: y.reshape(bsz * length, W_GROUP)
    x2 = _outproj(x1, ya, flat(yb), flat(yc), flat(yd), p["w_out"], layer=layer, tm=cfg["tm"])
    x3 = _ffn(x2, *p["ffn2"], fn, layer=layer, final=final, tm=cfg["ffn_tm"], tf=cfg["tf"])
    return x3, ret_s, conv_new, shift_new, rw_s, v_rows


def _stream_cfg(bsz, length, sample):
    m = bsz * length
    tm = min(512, m)
    if sample:
        return dict(tm=tm, ffn_tm=tm, proj_tm=min(1024, m), tf=512, tn=1280, gate_tm=min(512, m), with_rows=True,
                    ret_bb=8, conv_bb=32, conv_tt=length, rwkv_bb=16, rwkv_tt=length, rwkv_gsz=4,
                    rwkv_chunk=min(RWKV_CHUNK, length))
    return dict(tm=tm, ffn_tm=tm, proj_tm=min(1024, m), tf=512, tn=1280, gate_tm=min(512, m), with_rows=False,
                ret_bb=1, conv_bb=1, conv_tt=min(512, length), rwkv_bb=bsz, rwkv_tt=min(128, length),
                rwkv_gsz=4, rwkv_chunk=min(RWKV_CHUNK, length))


def kernel(x_prompt, x_sample, state_ret, state_conv, state_rwkv_shift, state_rwkv, ffn1_norm, ffn1_w_gate, ffn1_w_up, ffn1_w_down, mix_norm, w_in, w_out, a_w_s, a_b_s, a_ln_g, a_ln_b, c_conv_w, k_mu, k_w0, k_w2, k_a0, k_a2, k_g2, k_k_k, k_k_a, k_r_k, k_ln_w, k_ln_b, ffn2_norm, ffn2_w_gate, ffn2_w_up, ffn2_w_down, final_norm):
    bp, lp, d = x_prompt.shape
    bs, ls, _ = x_sample.shape
    depth = ffn1_norm.shape[0]
    cfg_p = _stream_cfg(bp, lp, sample=False)
    cfg_s = _stream_cfg(bs, ls, sample=True)
    fn = final_norm.reshape(1, d)

    zero_ret = jnp.zeros((1, bp, R_HEADS, R_HD, R_HD), F32)
    zero_conv = jnp.zeros((bp, C_WIDTH - 1, W_GROUP), F32)
    zero_shift = jnp.zeros((bp, K_COLS), F32)
    zero_rw = jnp.zeros((1, bp, K_HEADS, K_HD, K_HD), F32)

    ffn1_w_gate, ffn1_w_up, ffn1_w_down, w_in, w_out, ffn2_w_gate, ffn2_w_up, ffn2_w_down = (
        w.astype(BF16) for w in (ffn1_w_gate, ffn1_w_up, ffn1_w_down, w_in, w_out,
                                 ffn2_w_gate, ffn2_w_up, ffn2_w_down))
    xp = x_prompt.reshape(bp * lp, d)
    xs = x_sample.reshape(bs * ls, d)
    outs = [[] for _ in range(9)]
    for l in range(depth):
        p = _layer_params(l, ffn1_norm, ffn1_w_gate, ffn1_w_up, ffn1_w_down, mix_norm, w_in, w_out,
                          a_w_s, a_b_s, a_ln_g, a_ln_b, c_conv_w,
                          k_mu, k_w0, k_w2, k_a0, k_a2, k_g2, k_k_k, k_k_a, k_r_k, k_ln_w, k_ln_b,
                          ffn2_norm, ffn2_w_gate, ffn2_w_up, ffn2_w_down)
        final = l == depth - 1
        xp, rp, cp, sp, wp, _ = _stream_layer(xp, bp, lp, 0.0, zero_ret, 0, zero_conv, zero_shift, zero_rw,
                                              p, fn, final=final, cfg=cfg_p)
        xs, rs, cs, ss, ws, vs = _stream_layer(xs, bs, ls, float(PAST_LEN), state_ret, l, state_conv[l],
                                               state_rwkv_shift[l], state_rwkv, p, fn,
                                               final=final, cfg=cfg_s)
        for acc, val in zip(outs, (rp, rs, cp, cs, sp, ss, wp, ws, vs.reshape(bs, ls, W_GROUP))):
            acc.append(val)

    return (xp.reshape(bp, lp, d), xs.reshape(bs, ls, d)) + tuple(jnp.stack(o) for o in outs)
```

```python
import functools

import numpy as np
import jax
import jax.numpy as jnp
from jax import lax
from jax.experimental import pallas as pl
from jax.experimental.pallas import tpu as pltpu

F32 = jnp.float32
BF16 = jnp.bfloat16

W_GROUP = 512
A_CHUNK = 128
R_HEADS = 4
R_HD = 128
R_CHUNK = 128
ROPE_BASE = 10000.0
C_WIDTH = 3
K_HD = 64
K_HEADS = 8
W_LORA = 64
A_LORA = 64
G_LORA = 128
LORA_COLS = W_LORA + A_LORA + G_LORA
K_COLS = 3 * W_GROUP + LORA_COLS
EPS = 1e-6
GN_EPS = 64e-5
PAST_LEN = 16384

COL_A_U, COL_A_V = 0, 1
COL_R_Q, COL_R_K, COL_R_V, COL_R_G = 2, 3, 4, 5
COL_C_B, COL_C_C, COL_C_H = 6, 7, 8
COL_K_R, COL_K_K, COL_K_V = 9, 10, 11
COL_K_LORA = (12 * W_GROUP) // LORA_COLS
K_COL0 = 9 * W_GROUP

V7X_VMEM_LIMIT_BYTES = 62 * 1024 * 1024
RWKV_DECAY_SCALE =float(np.exp(-0.5))
RWKV_CHUNK = 16
V7X_MXU_DIM = 256


def _cparams(sem, vmem=V7X_VMEM_LIMIT_BYTES):
    return pltpu.CompilerParams(dimension_semantics=sem, vmem_limit_bytes=vmem)


def _rms(x, w):
    return x * lax.rsqrt(jnp.mean(x * x, axis=-1, keepdims=True) + EPS) * w


def _ffn_body(x_ref, nw_ref, wg_ref, wu_ref, wd_ref, fn_ref, o_ref, *rest, n_f, final, cast):
    j = pl.program_id(1)
    hn_ref = rest[-1]
    if cast:
        casted = rest[:3]
        for src, dst in zip((wg_ref, wu_ref, wd_ref), casted):
            dst[...] = src[...].astype(BF16)
        wg_ref, wu_ref, wd_ref = casted

    @pl.when(j == 0)
    def _():
        hn_ref[...] = _rms(x_ref[...], nw_ref[...]).astype(BF16)
        o_ref[...] = jnp.zeros_like(o_ref)

    h = hn_ref[...]
    tf = wg_ref.shape[1]
    acc = o_ref[...]
    pending = None
    for c0 in range(0, tf, V7X_MXU_DIM):
        c1 = min(c0 + V7X_MXU_DIM, tf)
        g = jnp.dot(h, wg_ref[:, c0:c1], preferred_element_type=F32)
        u = jnp.dot(h, wu_ref[:, c0:c1], preferred_element_type=F32)
        if pending is not None:
            acc = acc + jnp.dot(pending[0], wd_ref[pending[1]:pending[2], :], preferred_element_type=F32)
        pending = ((g * jax.nn.sigmoid(g) * u).astype(BF16), c0, c1)
    o_ref[...] = acc + jnp.dot(pending[0], wd_ref[pending[1]:pending[2], :], preferred_element_type=F32)

    @pl.when(j == n_f - 1)
    def _():
        y = x_ref[...] + 0.5 * o_ref[...]
        if final:
            y = _rms(y, fn_ref[...])
        o_ref[...] = y


def _ffn(x, nw, wg, wu, wd, fn, *, layer, final, tm, tf, cast=False):
    m, d = x.shape
    f = wg.shape[2]
    n_f = f // tf
    out_specs = [pl.BlockSpec((tm, d), lambda i, j: (i, 0))]
    out_shape = [jax.ShapeDtypeStruct((m, d), F32)]
    if cast:
        assert m == tm, "each weight block must be visited once"
        out_specs += [pl.BlockSpec((None, d, tf), lambda i, j: (0, 0, j)),
                      pl.BlockSpec((None, d, tf), lambda i, j: (0, 0, j)),
                      pl.BlockSpec((None, tf, d), lambda i, j: (0, j, 0))]
        out_shape += [jax.ShapeDtypeStruct((1, d, f), BF16), jax.ShapeDtypeStruct((1, d, f), BF16),
                      jax.ShapeDtypeStruct((1, f, d), BF16)]
    res = pl.pallas_call(
        functools.partial(_ffn_body, n_f=n_f, final=final, cast=cast),
        grid=(m // tm, n_f),
        in_specs=[
            pl.BlockSpec((tm, d), lambda i, j: (i, 0)),
            pl.BlockSpec((1, d), lambda i, j: (0, 0)),
            pl.BlockSpec((None, d, tf), lambda i, j: (layer, 0, j)),
            pl.BlockSpec((None, d, tf), lambda i, j: (layer, 0, j)),
            pl.BlockSpec((None, tf, d), lambda i, j: (layer, j, 0)),
            pl.BlockSpec((1, d), lambda i, j: (0, 0)),
        ],
        out_specs=out_specs,
        out_shape=out_shape,
        scratch_shapes=[pltpu.VMEM((tm, d), BF16)],
        compiler_params=_cparams(("parallel", "arbitrary")),
        name="ffn_final" if final else "ffn",
    )(x, nw, wg, wu, wd, fn)
    return (res[0], tuple(res[1:])) if cast else res[0]


def _proj_body(x_ref, nw_ref, w_ref, o_ref, hn_ref):
    @pl.when(pl.program_id(1) == 0)
    def _():
        hn_ref[...] = _rms(x_ref[...], nw_ref[...]).astype(BF16)

    o_ref[...] = jnp.dot(hn_ref[...], w_ref[...], preferred_element_type=F32)


def _proj(x, nw, w, *, layer, tm, tn):
    m, d = x.shape
    n = w.shape[2]
    return pl.pallas_call(
        _proj_body,
        grid=(m // tm, n // tn),
        in_specs=[
            pl.BlockSpec((tm, d), lambda i, j: (i, 0)),
            pl.BlockSpec((1, d), lambda i, j: (0, 0)),
            pl.BlockSpec((None, d, tn), lambda i, j: (layer, 0, j)),
        ],
        out_specs=pl.BlockSpec((tm, tn), lambda i, j: (i, j)),
        out_shape=jax.ShapeDtypeStruct((m, n), F32),
        scratch_shapes=[pltpu.VMEM((tm, d), BF16)],
        compiler_params=_cparams(("parallel", "arbitrary")),
        name="in_proj",
    )(x, nw, w)


def _outproj_body(x_ref, ya_ref, yb_ref, yc_ref, yd_ref, w_ref, o_ref):
    acc = x_ref[...]
    for gi, y_ref in enumerate((ya_ref, yb_ref, yc_ref, yd_ref)):
        acc = acc + jnp.dot(y_ref[...], w_ref[gi * W_GROUP:(gi + 1) * W_GROUP, :],
                            preferred_element_type=F32)
    o_ref[...] = acc


def _outproj(x, ya, yb, yc, yd, w, *, layer, tm):
    m, d = x.shape
    yspec = pl.BlockSpec((tm, W_GROUP), lambda i: (i, 0))
    return pl.pallas_call(
        _outproj_body,
        grid=(m // tm,),
        in_specs=[pl.BlockSpec((tm, d), lambda i: (i, 0)), yspec, yspec, yspec, yspec,
                  pl.BlockSpec((None,) + w.shape[1:], lambda i: (layer, 0, 0))],
        out_specs=pl.BlockSpec((tm, d), lambda i: (i, 0)),
        out_shape=jax.ShapeDtypeStruct((m, d), F32),
        compiler_params=_cparams(("parallel",)),
        name="out_proj",
    )(x, ya, yb, yc, yd, w)


def _gate_body(u_ref, v_ref, wm_ref, bias_ref, g_ref, b_ref, y_ref, *vr_ref, tm):
    gu = jax.nn.gelu(u_ref[...], approximate=True)
    gv = jax.nn.gelu(v_ref[...], approximate=True)
    mu = jnp.mean(gv, axis=-1, keepdims=True)
    var = jnp.mean(jnp.square(gv - mu), axis=-1, keepdims=True)
    vn = (gv - mu) * lax.rsqrt(var + EPS) * g_ref[...] + b_ref[...]
    if vr_ref:
        vr_ref[0][...] = vn
    vnb = vn.astype(BF16)
    for c in range(tm // A_CHUNK):
        rows = slice(c * A_CHUNK, (c + 1) * A_CHUNK)
        for h in range(W_GROUP // A_CHUNK):
            cols = slice(h * A_CHUNK, (h + 1) * A_CHUNK)
            z = jnp.dot(wm_ref[h], vnb[rows, cols], preferred_element_type=F32) + bias_ref[:, cols]
            y_ref[rows, cols] = (gu[rows, cols] * z).astype(BF16)


def _gate(p2d, wm, bias, ln_g, ln_b, *, tm, with_rows):
    m = p2d.shape[0]
    row_spec = pl.BlockSpec((tm, W_GROUP), lambda i: (i, 0))
    out_shape = [jax.ShapeDtypeStruct((m, W_GROUP), BF16)]
    out_specs = [row_spec]
    if with_rows:
        out_shape.append(jax.ShapeDtypeStruct((m, W_GROUP), F32))
        out_specs.append(row_spec)
    res = pl.pallas_call(
        functools.partial(_gate_body, tm=tm),
        grid=(m // tm,),
        in_specs=[
            pl.BlockSpec((tm, W_GROUP), lambda i: (i, COL_A_U)),
            pl.BlockSpec((tm, W_GROUP), lambda i: (i, COL_A_V)),
            pl.BlockSpec(wm.shape, lambda i: (0, 0, 0)),
            pl.BlockSpec(bias.shape, lambda i: (0, 0)),
            pl.BlockSpec((1, W_GROUP), lambda i: (0, 0)),
            pl.BlockSpec((1, W_GROUP), lambda i: (0, 0)),
        ],
        out_specs=out_specs,
        out_shape=out_shape,
        compiler_params=_cparams(("parallel",)),
        name="spatial_gate",
    )(p2d, p2d, wm, bias, ln_g, ln_b)
    return res if with_rows else (res[0], None)


def _ret_body(q_ref, k_ref, v_ref, g_ref, cos_ref, sin_ref, dm_ref, qd_ref, kd_ref, s0_ref,
              y_ref, so_ref, s_ref, *, bb, n_c, chunk_decay):
    c = pl.program_id(1)

    @pl.when(c == 0)
    def _():
        s_ref[...] = s0_ref[...]

    cos = cos_ref[...]
    sin = sin_ref[...]
    nt = (((1,), (1,)), ((), ()))
    tn = (((0,), (0,)), ((), ()))
    units = [(b, h, slice(h * R_HD, (h + 1) * R_HD)) for b in range(bb) for h in range(R_HEADS)]
    idx = range(len(units))

    def rope(x):
        return x * cos + pltpu.roll(x, R_HD // 2, axis=1) * sin

    qr = [rope(q_ref[b, :, cols]) for b, h, cols in units]
    kr = [rope(k_ref[b, :, cols]) * (R_HD ** -0.5) for b, h, cols in units]
    v = [v_ref[b, :, cols].astype(BF16) for b, h, cols in units]
    s = [s_ref[b, h] for b, h, cols in units]
    sc = [lax.dot_general(qr[i].astype(BF16), kr[i].astype(BF16), nt, preferred_element_type=F32)
          * dm_ref[units[i][1]] for i in idx]
    cross = [jnp.dot((qr[i] * qd_ref[:, units[i][2]]).astype(BF16), s[i].astype(BF16),
                     preferred_element_type=F32) for i in idx]
    kv = [lax.dot_general((kr[i] * kd_ref[:, units[i][2]]).astype(BF16), v[i], tn,
                          preferred_element_type=F32) for i in idx]
    o = [cross[i] + jnp.dot(sc[i].astype(BF16), v[i], preferred_element_type=F32) for i in idx]
    for i, (b, h, cols) in enumerate(units):
        s_ref[b, h] = chunk_decay[h] * s[i] + kv[i]
        on = o[i] * lax.rsqrt(jnp.mean(o[i] * o[i], axis=-1, keepdims=True) + EPS)
        g = g_ref[b, :, cols]
        y_ref[b, :, cols] = (on * (g * jax.nn.sigmoid(g))).astype(BF16)

    @pl.when(c == n_c - 1)
    def _():
        so_ref[...] = s_ref[...]


def _ret_tables(cl, pos0, length):
    half = R_HD // 2
    inv = ROPE_BASE ** (-jnp.arange(half, dtype=F32) / half)
    pos = pos0 + jnp.arange(length, dtype=F32)
    ang = pos[:, None] * inv[None, :]
    cos = jnp.cos(ang)
    sin = jnp.sin(ang)
    cos_t = jnp.concatenate([cos, cos], axis=-1)
    sin_t = jnp.concatenate([-sin, sin], axis=-1)
    log_gamma = np.log(1.0 - 2.0 ** (-5.0 - np.arange(R_HEADS, dtype=np.float64)))
    idx = np.arange(cl, dtype=np.float64)
    diff = idx[:, None] - idx[None, :]
    dmat = np.where(diff >= 0, np.exp(np.maximum(diff, 0.0)[None] * log_gamma[:, None, None]), 0.0)
    kdec = np.exp((cl - 1.0 - idx)[:, None] * log_gamma[None, :])
    qdec = np.exp((idx + 1.0)[:, None] * log_gamma[None, :])
    chunk_decay = tuple(float(x) for x in np.exp(cl * log_gamma))
    rep = lambda a: jnp.asarray(np.repeat(a, R_HD, axis=1), F32)
    return cos_t, sin_t, jnp.asarray(dmat, F32), rep(qdec), rep(kdec), chunk_decay


def _retention(p3d, s0_all, layer, pos0, *, bb):
    bsz, length, _ = p3d.shape
    cl = min(R_CHUNK, length)
    n_c = length // cl
    cos_t, sin_t, dmat, qdec, kdec, chunk_decay = _ret_tables(cl, pos0, length)

    def col(j):
        return pl.BlockSpec((bb, cl, W_GROUP), lambda b, c: (b, c, j))

    tab = pl.BlockSpec((cl, R_HD), lambda b, c: (c, 0))
    state = pl.BlockSpec((bb, R_HEADS, R_HD, R_HD), lambda b, c: (b, 0, 0, 0))
    return pl.pallas_call(
        functools.partial(_ret_body, bb=bb, n_c=n_c, chunk_decay=chunk_decay),
        grid=(bsz // bb, n_c),
        in_specs=[col(COL_R_Q), col(COL_R_K), col(COL_R_V), col(COL_R_G), tab, tab,
                  pl.BlockSpec(dmat.shape, lambda b, c: (0, 0, 0)),
                  pl.BlockSpec(qdec.shape, lambda b, c: (0, 0)),
                  pl.BlockSpec(kdec.shape, lambda b, c: (0, 0)),
                  pl.BlockSpec((None, bb, R_HEADS, R_HD, R_HD), lambda b, c: (layer, b, 0, 0, 0))],
        out_specs=[pl.BlockSpec((bb, cl, W_GROUP), lambda b, c: (b, c, 0)), state],
        out_shape=[jax.ShapeDtypeStruct((bsz, length, W_GROUP), BF16),
                   jax.ShapeDtypeStruct(s0_all.shape[1:], F32)],
        scratch_shapes=[pltpu.VMEM((bb, R_HEADS, R_HD, R_HD), F32)],
        compiler_params=_cparams(("parallel", "arbitrary")),
        name="retention",
    )(p3d, p3d, p3d, p3d, cos_t, sin_t, dmat, qdec, kdec, s0_all)


def _conv_body(bg_ref, cg_ref, h_ref, buf_ref, w_ref, y_ref, st_ref, carry_ref, *, bb, tt):
    @pl.when(pl.program_id(1) == 0)
    def _():
        carry_ref[...] = buf_ref[...]

    shape = (bb, tt, W_GROUP)
    z = cg_ref[...] * h_ref[...]
    z2 = z.reshape(bb * tt, W_GROUP)
    r1 = pltpu.roll(z2, 1, axis=0).reshape(shape)
    r2 = pltpu.roll(z2, 2, axis=0).reshape(shape)
    tpos = lax.broadcasted_iota(jnp.int32, shape, 1)
    c0 = carry_ref[:, 0:1, :]
    c1 = carry_ref[:, 1:2, :]
    zm1 = jnp.where(tpos == 0, c1, r1)
    zm2 = jnp.where(tpos == 0, c0, jnp.where(tpos == 1, c1, r2))
    y = w_ref[0:1, :] * zm2 + w_ref[1:2, :] * zm1 + w_ref[2:3, :] * z
    y_ref[...] = (bg_ref[...] * y).astype(BF16)
    new = cg_ref[:, tt - 2:tt, :] * h_ref[:, tt - 2:tt, :]
    carry_ref[...] = new
    st_ref[...] = new


def _conv(p3d, buf, w, *, bb, tt):
    bsz, length, _ = p3d.shape

    def col(j):
        return pl.BlockSpec((bb, tt, W_GROUP), lambda b, t: (b, t, j))

    state = pl.BlockSpec((bb, C_WIDTH - 1, W_GROUP), lambda b, t: (b, 0, 0))
    return pl.pallas_call(
        functools.partial(_conv_body, bb=bb, tt=tt),
        grid=(bsz // bb, length // tt),
        in_specs=[col(COL_C_B), col(COL_C_C), col(COL_C_H), state,
                  pl.BlockSpec(w.shape, lambda b, t: (0, 0))],
        out_specs=[pl.BlockSpec((bb, tt, W_GROUP), lambda b, t: (b, t, 0)), state],
        out_shape=[jax.ShapeDtypeStruct((bsz, length, W_GROUP), BF16),
                   jax.ShapeDtypeStruct(buf.shape, F32)],
        scratch_shapes=[pltpu.VMEM((bb, C_WIDTH - 1, W_GROUP), F32)],
        compiler_params=_cparams(("parallel", "arbitrary")),
        name="short_conv",
    )(p3d, p3d, p3d, buf, w)


def _group_dot(xb, ones_bd):
    wb = ones_bd.shape[0]
    return jnp.concatenate(
        [jnp.dot(xb[:, i * wb:(i + 1) * wb], ones_bd, preferred_element_type=F32)
         for i in range(W_GROUP // wb)], axis=1)


def _head_sum(x, ones_bd):
    hi = x.astype(BF16)
    lo = (x - hi.astype(F32)).astype(BF16)
    return _group_dot(hi, ones_bd) + _group_dot(lo, ones_bd)


def _rwkvc_body(r_ref, k_ref, v_ref, lo_ref, sh_ref, s0_ref,
                mu_ref, mulo_ref, w0_ref, w2_ref, a0_ref, a2_ref, g2_ref, kk_ref, ka_ref, rk_ref,
                lnw_ref, lnb_ref, ones_ref,
                y_ref, so_ref,
                s2_ref, carry_ref, carrylo_ref, pt_s, rt_s, qh_s, kh_s, qb_s, kb_s, v_s, ec_s, ys,
                *, bb, tt, n_t, gsz, chunk):
    tb = pl.program_id(1)
    n = bb * tt
    w3 = (bb, tt, W_GROUP)
    ones_bd = ones_ref[...]
    n_pair = K_HEADS // 2
    pair_w = 2 * K_HD
    zero_blk = jnp.zeros((K_HD, K_HD), F32)

    @pl.when(tb == 0)
    def _():
        for b in range(bb):
            for p in range(n_pair):
                top = jnp.concatenate([s0_ref[b, 2 * p], zero_blk], axis=1)
                bot = jnp.concatenate([zero_blk, s0_ref[b, 2 * p + 1]], axis=1)
                s2_ref[b, p] = jnp.concatenate([top, bot], axis=0)
        carry_ref[...] = sh_ref[:, :, 0:3 * W_GROUP]
        carrylo_ref[...] = sh_ref[:, :, 3 * W_GROUP:K_COLS]

    tpos = lax.broadcasted_iota(jnp.int32, w3, 1)
    tpos_lo = lax.broadcasted_iota(jnp.int32, (bb, tt, LORA_COLS), 1)

    def shifted(x, carry, mu, mask):
        prev = pltpu.roll(x.reshape(n, x.shape[-1]), 1, axis=0).reshape(x.shape)
        prev = jnp.where(mask == 0, carry, prev)
        return (x + (prev - x) * mu).reshape(n, x.shape[-1])

    r_in, k_in, v_in, lo_in = r_ref[...], k_ref[...], v_ref[...], lo_ref[...]
    r = shifted(r_in, carry_ref[:, :, 0:W_GROUP], mu_ref[:, 0:W_GROUP], tpos)
    k = shifted(k_in, carry_ref[:, :, W_GROUP:2 * W_GROUP], mu_ref[:, W_GROUP:2 * W_GROUP], tpos)
    v = shifted(v_in, carry_ref[:, :, 2 * W_GROUP:3 * W_GROUP], mu_ref[:, 2 * W_GROUP:3 * W_GROUP], tpos)
    lo = shifted(lo_in, carrylo_ref[...], mulo_ref[...], tpos_lo)
    carry_ref[:, :, 0:W_GROUP] = r_ref[:, tt - 1:tt, :]
    carry_ref[:, :, W_GROUP:2 * W_GROUP] = k_ref[:, tt - 1:tt, :]
    carry_ref[:, :, 2 * W_GROUP:3 * W_GROUP] = v_ref[:, tt - 1:tt, :]
    carrylo_ref[...] = lo_ref[:, tt - 1:tt, :]

    zw = w0_ref[...] + jnp.dot(jnp.tanh(lo).astype(BF16), w2_ref[...], preferred_element_type=F32)
    logw = -RWKV_DECAY_SCALE * jax.nn.sigmoid(zw)
    a = jax.nn.sigmoid(a0_ref[...] + jnp.dot(lo.astype(BF16), a2_ref[...], preferred_element_type=F32))
    gate = jnp.dot(jax.nn.sigmoid(lo).astype(BF16), g2_ref[...], preferred_element_type=F32)
    kk = k * kk_ref[...]
    kk = kk * lax.rsqrt(jnp.maximum(_head_sum(kk * kk, ones_bd), 1e-24))
    k2 = k * (1.0 + (a - 1.0) * ka_ref[...])
    bonus = _head_sum(r * k2 * rk_ref[...], ones_bd) * v
    q = -(kk * a)

    cpos = lax.broadcasted_iota(jnp.int32, (n, W_GROUP), 0) % chunk
    g = logw
    step = 1
    while step < chunk:
        g = g + jnp.where(cpos >= step, pltpu.roll(g, step, axis=0), 0.0)
        step *= 2
    g3 = g.reshape(n // chunk, chunk, W_GROUP)
    gtot = jnp.broadcast_to(g3[:, chunk - 1:chunk, :], g3.shape).reshape(n, W_GROUP)
    e_neg = jnp.exp(-g)
    e_rem = jnp.exp(gtot - g)
    pt_s[...] = (kk * jnp.exp(g - logw)).reshape(w3)
    rt_s[...] = (r * jnp.exp(g)).reshape(w3)
    qh_s[...] = (q * e_neg).reshape(w3)
    kh_s[...] = (k2 * e_neg).reshape(w3)
    qb_s[...] = (q * e_rem).reshape(w3)
    kb_s[...] = (k2 * e_rem).reshape(w3)
    v_s[...] = v.reshape(w3)
    ec_s[...] = jnp.exp(gtot).reshape(w3)

    c2, c4 = 2 * chunk, 4 * chunk
    even = lax.broadcasted_iota(jnp.int32, (chunk, pair_w), 1) < K_HD
    ri = lax.broadcasted_iota(jnp.int32, (c4, c4), 0)
    ci = lax.broadcasted_iota(jnp.int32, (c4, c4), 1)
    keep = ci % chunk < ri % chunk + ri // c2
    right = lax.broadcasted_iota(jnp.int32, (c2, c4), 1) >= c2
    eye = (lax.broadcasted_iota(jnp.int32, (c2, c2), 0)
           == lax.broadcasted_iota(jnp.int32, (c2, c2), 1)).astype(F32)
    nt = (((1,), (1,)), ((), ()))
    tn = (((0,), (0,)), ((), ()))
    n_double = chunk.bit_length() - 2

    def two(x):
        return jnp.concatenate([jnp.where(even, x, 0.0), jnp.where(even, 0.0, x)], axis=0)

    def mm(x, y):
        return jnp.dot(x.astype(BF16), y.astype(BF16), preferred_element_type=F32)

    def units(args):
        v2, pr, qk, qkb, s2, ec = zip(*args)
        idx = range(len(args))
        apr = [jnp.where(keep, lax.dot_general(pr[i], qk[i], nt, preferred_element_type=F32), 0.0) for i in idx]
        prs = [lax.dot_general(pr[i], s2[i].astype(BF16), nt, preferred_element_type=F32) for i in idx]
        apk = [mm(jnp.where(right, apr[i][0:c2], 0.0), jnp.concatenate([v2[i], v2[i]], axis=0)) for i in idx]
        power = [apr[i][0:c2, 0:c2] for i in idx]
        inv = [eye + power[i] for i in idx]
        for _ in range(n_double):
            power = [mm(power[i], power[i]) for i in idx]
            inv = [inv[i] + mm(inv[i], power[i]) for i in idx]
        u2 = [mm(inv[i], prs[i][0:c2] + apk[i]) for i in idx]
        uv = [jnp.concatenate([u2[i], v2[i]], axis=0).astype(BF16) for i in idx]
        y2 = [prs[i][c2:c4] + jnp.dot(apr[i][c2:c4].astype(BF16), uv[i], preferred_element_type=F32)
              for i in idx]
        s_new = [s2[i] * ec[i] + lax.dot_general(uv[i], qkb[i], tn, preferred_element_type=F32) for i in idx]
        return [(y2[i][0:chunk] + y2[i][chunk:c2], s_new[i]) for i in idx]

    def chunk_step(i, carry):
        c = i // (bb // gsz)
        b0 = (i % (bb // gsz)) * gsz
        r0 = pl.multiple_of(c * chunk, chunk)
        rows = pl.ds(r0, chunk)
        where = [(b0 + j, slice(p * pair_w, (p + 1) * pair_w), p) for j in range(gsz) for p in range(n_pair)]
        args = []
        for b, lanes, p in where:
            ld = lambda ref: two(ref[b, rows, lanes])
            args.append((ld(v_s),
                         jnp.concatenate([ld(pt_s), ld(rt_s)], axis=0).astype(BF16),
                         jnp.concatenate([ld(qh_s), ld(kh_s)], axis=0).astype(BF16),
                         jnp.concatenate([ld(qb_s), ld(kb_s)], axis=0).astype(BF16),
                         s2_ref[b, p], ec_s[b, pl.ds(r0, 1), lanes]))
        for (b, lanes, p), (y, s_new) in zip(where, units(args)):
            ys[b, rows, lanes] = y
            s2_ref[b, p] = s_new
        return carry

    lax.fori_loop(0, (tt // chunk) * (bb // gsz), chunk_step, 0)

    y = ys[...].reshape(n, W_GROUP)
    mean = _head_sum(y, ones_bd) * (1.0 / K_HD)
    yc = y - mean
    var = _head_sum(yc * yc, ones_bd) * (1.0 / K_HD)
    out = (yc * lax.rsqrt(var + GN_EPS) * lnw_ref[...] + lnb_ref[...] + bonus) * gate
    y_ref[...] = out.reshape(w3).astype(BF16)

    @pl.when(tb == n_t - 1)
    def _():
        for b in range(bb):
            for p in range(n_pair):
                so_ref[b, 2 * p] = s2_ref[b, p, 0:K_HD, 0:K_HD]
                so_ref[b, 2 * p + 1] = s2_ref[b, p, K_HD:pair_w, K_HD:pair_w]


def _rwkvc(p3d, shift, s0_all, layer, prm, *, bb, tt, gsz, chunk):
    bsz, length, _ = p3d.shape
    n_t = length // tt

    def col(j):
        return pl.BlockSpec((bb, tt, W_GROUP), lambda b, t: (b, t, j))

    def whole(a):
        nd = a.ndim
        return pl.BlockSpec(a.shape, lambda b, t: (0,) * nd)

    state = pl.BlockSpec((bb, K_HEADS, K_HD, K_HD), lambda b, t: (b, 0, 0, 0))
    params = [prm[nm] for nm in ("mu", "mu_lo", "w0", "w2", "a0", "a2", "g2", "k_k", "k_a", "r_k",
                                 "ln_w", "ln_b", "ones_bd")]
    blk = pltpu.VMEM((bb, tt, W_GROUP), F32)
    return pl.pallas_call(
        functools.partial(_rwkvc_body, bb=bb, tt=tt, n_t=n_t, gsz=gsz, chunk=chunk),
        grid=(bsz // bb, n_t),
        in_specs=[col(COL_K_R), col(COL_K_K), col(COL_K_V),
                  pl.BlockSpec((bb, tt, LORA_COLS), lambda b, t: (b, t, COL_K_LORA)),
                  pl.BlockSpec((bb, 1, K_COLS), lambda b, t: (b, 0, 0)),
                  pl.BlockSpec((None, bb, K_HEADS, K_HD, K_HD), lambda b, t: (layer, b, 0, 0, 0))]
                 + [whole(a) for a in params],
        out_specs=[pl.BlockSpec((bb, tt, W_GROUP), lambda b, t: (b, t, 0)), state],
        out_shape=[jax.ShapeDtypeStruct((bsz, length, W_GROUP), BF16),
                   jax.ShapeDtypeStruct(s0_all.shape[1:], F32)],
        scratch_shapes=[pltpu.VMEM((bb, K_HEADS // 2, 2 * K_HD, 2 * K_HD), F32),
                        pltpu.VMEM((bb, 1, 3 * W_GROUP), F32),
                        pltpu.VMEM((bb, 1, LORA_COLS), F32),
                        blk, blk, blk, blk, blk, blk, blk, blk, blk],
        compiler_params=_cparams(("parallel", "arbitrary")),
        name="rwkv7c",
    )(p3d, p3d, p3d, p3d, shift, s0_all, *params)


def _gate_mixing(w_s, b_s, seq):
    cl = min(A_CHUNK, seq)
    wm = jnp.tril(w_s[:, :cl, :cl])
    bias = b_s[:, :cl]
    rep = A_CHUNK // cl
    if rep > 1:
        eye = jnp.eye(rep, dtype=w_s.dtype)
        wm = jnp.einsum("ab,hts->hatbs", eye, wm).reshape(w_s.shape[0], A_CHUNK, A_CHUNK)
        bias = jnp.tile(bias, (1, rep))
    bias = jnp.repeat(bias.T, A_CHUNK, axis=1)
    return wm.astype(BF16), bias


def _pad_rows(w, row0):
    return jnp.zeros((LORA_COLS, W_GROUP), F32).at[row0:row0 + w.shape[0]].set(w).astype(BF16)


def _layer_params(l, ffn1_norm, ffn1_w_gate, ffn1_w_up, ffn1_w_down, mix_norm, w_in, w_out,
                  a_w_s, a_b_s, a_ln_g, a_ln_b, c_conv_w,
                  k_mu, k_w0, k_w2, k_a0, k_a2, k_g2, k_k_k, k_k_a, k_r_k, k_ln_w, k_ln_b,
                  ffn2_norm, ffn2_w_gate, ffn2_w_up, ffn2_w_down):
    row = lambda a: a[l].reshape(1, -1)
    head_of = np.arange(V7X_MXU_DIM) // K_HD
    ones_bd = jnp.asarray(head_of[:, None] == head_of[None, :], BF16)
    rwkv = dict(
        mu=k_mu[l][None, :3 * W_GROUP], mu_lo=k_mu[l][None, 3 * W_GROUP:],
        w0=row(k_w0), w2=_pad_rows(k_w2[l], 0),
        a0=row(k_a0), a2=_pad_rows(k_a2[l], W_LORA),
        g2=_pad_rows(k_g2[l], W_LORA + A_LORA),
        k_k=row(k_k_k), k_a=row(k_k_a), r_k=row(k_r_k), ln_w=row(k_ln_w), ln_b=row(k_ln_b),
        ones_bd=ones_bd)
    return dict(
        layer=l, ffn_layer=l,
        ffn1=(row(ffn1_norm), ffn1_w_gate, ffn1_w_up, ffn1_w_down),
        ffn2=(row(ffn2_norm), ffn2_w_gate, ffn2_w_up, ffn2_w_down),
        mix_norm=row(mix_norm), w_in=w_in, w_out=w_out,
        a_w_s=a_w_s[l], a_b_s=a_b_s[l], a_ln_g=row(a_ln_g), a_ln_b=row(a_ln_b),
        conv_w=c_conv_w[l], rwkv=rwkv)


def _stream_layer(x2d, bsz, length, pos0, ret_s0_all, ret_layer, conv_buf, rw_shift, rw_s0_all, p, fn, *,
                  final, cfg):
    layer = p["layer"]
    ffn_kw = dict(layer=p["ffn_layer"], tm=cfg["ffn_tm"], tf=cfg["tf"], cast=cfg["ffn_cast"])
    x1 = _ffn(x2d, *p["ffn1"], fn, final=False, **ffn_kw)
    casted = {}
    if cfg["ffn_cast"]:
        x1, casted["ffn1"] = x1
    proj = _proj(x1, p["mix_norm"], p["w_in"], layer=layer, tm=cfg["proj_tm"], tn=cfg["tn"])
    p3d = proj.reshape(bsz, length, proj.shape[1])

    wm, bias = _gate_mixing(p["a_w_s"], p["a_b_s"], length)
    ya, v_rows = _gate(proj, wm, bias, p["a_ln_g"], p["a_ln_b"], tm=cfg["gate_tm"],
                       with_rows=cfg["with_rows"])
    yb, ret_s = _retention(p3d, ret_s0_all, ret_layer, pos0, bb=cfg["ret_bb"])
    yc, conv_new = _conv(p3d, conv_buf, p["conv_w"], bb=cfg["conv_bb"], tt=cfg["conv_tt"])
    yd, rw_s = _rwkvc(p3d, rw_shift[:, None, :], rw_s0_all, ret_layer, p["rwkv"],
                      bb=cfg["rwkv_bb"], tt=cfg["rwkv_tt"], gsz=cfg["rwkv_gsz"], chunk=cfg["rwkv_chunk"])
    shift_new = p3d[:, length - 1, K_COL0:K_COL0 + K_COLS]

    flat = lambda y: y.reshape(bsz * length, W_GROUP)
    x2 = _outproj(x1, ya, flat(yb), flat(yc), flat(yd), p["w_out"], layer=layer, tm=cfg["tm"])
    x3 = _ffn(x2, *p["ffn2"], fn, final=final, **ffn_kw)
    if cfg["ffn_cast"]:
        x3, casted["ffn2"] = x3
    return x3, ret_s, conv_new, shift_new, rw_s, v_rows, casted


def _stream_cfg(bsz, length, sample):
    m = bsz * length
    tm = min(512, m)
    if sample:
        return dict(tm=tm, ffn_tm=m, ffn_cast=True, proj_tm=min(1024, m), tf=256, tn=1280,
                    gate_tm=min(512, m), with_rows=True,
                    ret_bb=8, conv_bb=32, conv_tt=length, rwkv_bb=16, rwkv_tt=length, rwkv_gsz=4,
                    rwkv_chunk=min(RWKV_CHUNK, length))
    return dict(tm=tm, ffn_tm=tm, ffn_cast=False, proj_tm=min(1024, m), tf=512, tn=1280,
                gate_tm=min(512, m), with_rows=False,
                ret_bb=1, conv_bb=1, conv_tt=min(512, length), rwkv_bb=bsz, rwkv_tt=min(128, length),
                rwkv_gsz=4, rwkv_chunk=min(RWKV_CHUNK, length))


def kernel(x_prompt, x_sample, state_ret, state_conv, state_rwkv_shift, state_rwkv, ffn1_norm, ffn1_w_gate, ffn1_w_up, ffn1_w_down, mix_norm, w_in, w_out, a_w_s, a_b_s, a_ln_g, a_ln_b, c_conv_w, k_mu, k_w0, k_w2, k_a0, k_a2, k_g2, k_k_k, k_k_a, k_r_k, k_ln_w, k_ln_b, ffn2_norm, ffn2_w_gate, ffn2_w_up, ffn2_w_down, final_norm):
    bp, lp, d = x_prompt.shape
    bs, ls, _ = x_sample.shape
    depth = ffn1_norm.shape[0]
    cfg_p = _stream_cfg(bp, lp, sample=False)
    cfg_s = _stream_cfg(bs, ls, sample=True)
    fn = final_norm.reshape(1, d)

    zero_ret = jnp.zeros((1, bp, R_HEADS, R_HD, R_HD), F32)
    zero_conv = jnp.zeros((bp, C_WIDTH - 1, W_GROUP), F32)
    zero_shift = jnp.zeros((bp, K_COLS), F32)
    zero_rw = jnp.zeros((1, bp, K_HEADS, K_HD, K_HD), F32)

    w_in, w_out = w_in.astype(BF16), w_out.astype(BF16)
    xp = x_prompt.reshape(bp * lp, d)
    xs = x_sample.reshape(bs * ls, d)
    outs = [[] for _ in range(9)]
    for l in range(depth):
        p = _layer_params(l, ffn1_norm, ffn1_w_gate, ffn1_w_up, ffn1_w_down, mix_norm, w_in, w_out,
                          a_w_s, a_b_s, a_ln_g, a_ln_b, c_conv_w,
                          k_mu, k_w0, k_w2, k_a0, k_a2, k_g2, k_k_k, k_k_a, k_r_k, k_ln_w, k_ln_b,
                          ffn2_norm, ffn2_w_gate, ffn2_w_up, ffn2_w_down)
        final = l == depth - 1
        xs, rs, cs, ss, ws, vs, w16 = _stream_layer(xs, bs, ls, float(PAST_LEN), state_ret, l, state_conv[l],
                                                    state_rwkv_shift[l], state_rwkv, p, fn,
                                                    final=final, cfg=cfg_s)
        p = dict(p, ffn_layer=0, ffn1=p["ffn1"][:1] + w16["ffn1"], ffn2=p["ffn2"][:1] + w16["ffn2"])
        xp, rp, cp, sp, wp, _, _ = _stream_layer(xp, bp, lp, 0.0, zero_ret, 0, zero_conv, zero_shift, zero_rw,
                                                 p, fn, final=final, cfg=cfg_p)
        for acc, val in zip(outs, (rp, rs, cp, cs, sp, ss, wp, ws, vs.reshape(bs, ls, W_GROUP))):
            acc.append(val)

    return (xp.reshape(bp, lp, d), xs.reshape(bs, ls, d)) + tuple(jnp.stack(o) for o in outs)
```

```python
import functools

import numpy as np
import jax
import jax.numpy as jnp
from jax import lax
from jax.experimental import pallas as pl
from jax.experimental.pallas import tpu as pltpu

F32 = jnp.float32
BF16 = jnp.bfloat16

W_GROUP = 512
A_CHUNK = 128
R_HEADS = 4
R_HD = 128
R_CHUNK = 128
ROPE_BASE = 10000.0
C_WIDTH = 3
K_HD = 64
K_HEADS = 8
W_LORA = 64
A_LORA = 64
G_LORA = 128
LORA_COLS = W_LORA + A_LORA + G_LORA
K_COLS = 3 * W_GROUP + LORA_COLS
EPS = 1e-6
GN_EPS = 64e-5
PAST_LEN = 16384

COL_A_U, COL_A_V = 0, 1
COL_R_Q, COL_R_K, COL_R_V, COL_R_G = 2, 3, 4, 5
COL_C_B, COL_C_C, COL_C_H = 6, 7, 8
COL_K_R, COL_K_K, COL_K_V = 9, 10, 11
COL_K_LORA = (12 * W_GROUP) // LORA_COLS
K_COL0 = 9 * W_GROUP

V7X_VMEM_LIMIT_BYTES = 62 * 1024 * 1024
RWKV_DECAY_SCALE =float(np.exp(-0.5))
RWKV_CHUNK = 64
V7X_MXU_DIM = 256


def _cparams(sem, vmem=V7X_VMEM_LIMIT_BYTES):
    return pltpu.CompilerParams(dimension_semantics=sem, vmem_limit_bytes=vmem)


def _rms(x, w):
    return x * lax.rsqrt(jnp.mean(x * x, axis=-1, keepdims=True) + EPS) * w


def _ffn_body(x_ref, nw_ref, wg_ref, wu_ref, wd_ref, fn_ref, o_ref, *rest, n_f, final, cast):
    j = pl.program_id(1)
    hn_ref = rest[-1]
    if cast:
        casted = rest[:3]
        for src, dst in zip((wg_ref, wu_ref, wd_ref), casted):
            dst[...] = src[...].astype(BF16)
        wg_ref, wu_ref, wd_ref = casted

    @pl.when(j == 0)
    def _():
        hn_ref[...] = _rms(x_ref[...], nw_ref[...]).astype(BF16)
        o_ref[...] = jnp.zeros_like(o_ref)

    h = hn_ref[...]
    tf = wg_ref.shape[1]
    acc = o_ref[...]
    pending = None
    for c0 in range(0, tf, V7X_MXU_DIM):
        c1 = min(c0 + V7X_MXU_DIM, tf)
        g = jnp.dot(h, wg_ref[:, c0:c1], preferred_element_type=F32)
        u = jnp.dot(h, wu_ref[:, c0:c1], preferred_element_type=F32)
        if pending is not None:
            acc = acc + jnp.dot(pending[0], wd_ref[pending[1]:pending[2], :], preferred_element_type=F32)
        pending = ((g * jax.nn.sigmoid(g) * u).astype(BF16), c0, c1)
    o_ref[...] = acc + jnp.dot(pending[0], wd_ref[pending[1]:pending[2], :], preferred_element_type=F32)

    @pl.when(j == n_f - 1)
    def _():
        y = x_ref[...] + 0.5 * o_ref[...]
        if final:
            y = _rms(y, fn_ref[...])
        o_ref[...] = y


def _ffn(x, nw, wg, wu, wd, fn, *, layer, final, tm, tf, cast=False):
    m, d = x.shape
    f = wg.shape[2]
    n_f = f // tf
    out_specs = [pl.BlockSpec((tm, d), lambda i, j: (i, 0))]
    out_shape = [jax.ShapeDtypeStruct((m, d), F32)]
    if cast:
        assert m == tm, "each weight block must be visited once"
        out_specs += [pl.BlockSpec((None, d, tf), lambda i, j: (0, 0, j)),
                      pl.BlockSpec((None, d, tf), lambda i, j: (0, 0, j)),
                      pl.BlockSpec((None, tf, d), lambda i, j: (0, j, 0))]
        out_shape += [jax.ShapeDtypeStruct((1, d, f), BF16), jax.ShapeDtypeStruct((1, d, f), BF16),
                      jax.ShapeDtypeStruct((1, f, d), BF16)]
    res = pl.pallas_call(
        functools.partial(_ffn_body, n_f=n_f, final=final, cast=cast),
        grid=(m // tm, n_f),
        in_specs=[
            pl.BlockSpec((tm, d), lambda i, j: (i, 0)),
            pl.BlockSpec((1, d), lambda i, j: (0, 0)),
            pl.BlockSpec((None, d, tf), lambda i, j: (layer, 0, j)),
            pl.BlockSpec((None, d, tf), lambda i, j: (layer, 0, j)),
            pl.BlockSpec((None, tf, d), lambda i, j: (layer, j, 0)),
            pl.BlockSpec((1, d), lambda i, j: (0, 0)),
        ],
        out_specs=out_specs,
        out_shape=out_shape,
        scratch_shapes=[pltpu.VMEM((tm, d), BF16)],
        compiler_params=_cparams(("parallel", "arbitrary")),
        name="ffn_final" if final else "ffn",
    )(x, nw, wg, wu, wd, fn)
    return (res[0], tuple(res[1:])) if cast else res[0]


def _proj_body(x_ref, nw_ref, w_ref, o_ref, *rest, cast):
    hn_ref = rest[-1]
    if cast:
        rest[0][...] = w_ref[...].astype(BF16)
        w_ref = rest[0]

    @pl.when(pl.program_id(1) == 0)
    def _():
        hn_ref[...] = _rms(x_ref[...], nw_ref[...]).astype(BF16)

    o_ref[...] = jnp.dot(hn_ref[...], w_ref[...], preferred_element_type=F32)


def _proj(x, nw, w, *, layer, tm, tn, cast=False):
    m, d = x.shape
    n = w.shape[2]
    out_specs = [pl.BlockSpec((tm, tn), lambda i, j: (i, j))]
    out_shape = [jax.ShapeDtypeStruct((m, n), F32)]
    if cast:
        assert m == tm, "each weight block must be visited once"
        out_specs.append(pl.BlockSpec((None, d, tn), lambda i, j: (0, 0, j)))
        out_shape.append(jax.ShapeDtypeStruct((1, d, n), BF16))
    res = pl.pallas_call(
        functools.partial(_proj_body, cast=cast),
        grid=(m // tm, n // tn),
        in_specs=[
            pl.BlockSpec((tm, d), lambda i, j: (i, 0)),
            pl.BlockSpec((1, d), lambda i, j: (0, 0)),
            pl.BlockSpec((None, d, tn), lambda i, j: (layer, 0, j)),
        ],
        out_specs=out_specs,
        out_shape=out_shape,
        scratch_shapes=[pltpu.VMEM((tm, d), BF16)],
        compiler_params=_cparams(("parallel", "arbitrary")),
        name="in_proj",
    )(x, nw, w)
    return tuple(res) if cast else res[0]


def _outproj_body(x_ref, ya_ref, yb_ref, yc_ref, yd_ref, w_ref, o_ref):
    acc = x_ref[...]
    for gi, y_ref in enumerate((ya_ref, yb_ref, yc_ref, yd_ref)):
        acc = acc + jnp.dot(y_ref[...], w_ref[gi * W_GROUP:(gi + 1) * W_GROUP, :],
                            preferred_element_type=F32)
    o_ref[...] = acc


def _outproj(x, ya, yb, yc, yd, w, *, layer, tm):
    m, d = x.shape
    yspec = pl.BlockSpec((tm, W_GROUP), lambda i: (i, 0))
    return pl.pallas_call(
        _outproj_body,
        grid=(m // tm,),
        in_specs=[pl.BlockSpec((tm, d), lambda i: (i, 0)), yspec, yspec, yspec, yspec,
                  pl.BlockSpec((None,) + w.shape[1:], lambda i: (layer, 0, 0))],
        out_specs=pl.BlockSpec((tm, d), lambda i: (i, 0)),
        out_shape=jax.ShapeDtypeStruct((m, d), F32),
        compiler_params=_cparams(("parallel",)),
        name="out_proj",
    )(x, ya, yb, yc, yd, w)


def _gate_body(u_ref, v_ref, wm_ref, bias_ref, g_ref, b_ref, y_ref, *vr_ref, tm):
    gu = jax.nn.gelu(u_ref[...], approximate=True)
    gv = jax.nn.gelu(v_ref[...], approximate=True)
    mu = jnp.mean(gv, axis=-1, keepdims=True)
    var = jnp.mean(jnp.square(gv - mu), axis=-1, keepdims=True)
    vn = (gv - mu) * lax.rsqrt(var + EPS) * g_ref[...] + b_ref[...]
    if vr_ref:
        vr_ref[0][...] = vn
    vnb = vn.astype(BF16)
    for c in range(tm // A_CHUNK):
        rows = slice(c * A_CHUNK, (c + 1) * A_CHUNK)
        for h in range(W_GROUP // A_CHUNK):
            cols = slice(h * A_CHUNK, (h + 1) * A_CHUNK)
            z = jnp.dot(wm_ref[h], vnb[rows, cols], preferred_element_type=F32) + bias_ref[:, cols]
            y_ref[rows, cols] = (gu[rows, cols] * z).astype(BF16)


def _gate(p2d, wm, bias, ln_g, ln_b, *, tm, with_rows):
    m = p2d.shape[0]
    row_spec = pl.BlockSpec((tm, W_GROUP), lambda i: (i, 0))
    out_shape = [jax.ShapeDtypeStruct((m, W_GROUP), BF16)]
    out_specs = [row_spec]
    if with_rows:
        out_shape.append(jax.ShapeDtypeStruct((m, W_GROUP), F32))
        out_specs.append(row_spec)
    res = pl.pallas_call(
        functools.partial(_gate_body, tm=tm),
        grid=(m // tm,),
        in_specs=[
            pl.BlockSpec((tm, W_GROUP), lambda i: (i, COL_A_U)),
            pl.BlockSpec((tm, W_GROUP), lambda i: (i, COL_A_V)),
            pl.BlockSpec(wm.shape, lambda i: (0, 0, 0)),
            pl.BlockSpec(bias.shape, lambda i: (0, 0)),
            pl.BlockSpec((1, W_GROUP), lambda i: (0, 0)),
            pl.BlockSpec((1, W_GROUP), lambda i: (0, 0)),
        ],
        out_specs=out_specs,
        out_shape=out_shape,
        compiler_params=_cparams(("parallel",)),
        name="spatial_gate",
    )(p2d, p2d, wm, bias, ln_g, ln_b)
    return res if with_rows else (res[0], None)


def _ret_body(q_ref, k_ref, v_ref, g_ref, cos_ref, sin_ref, dm_ref, qd_ref, kd_ref, s0_ref,
              y_ref, so_ref, s_ref, *, bb, n_c, chunk_decay):
    c = pl.program_id(1)

    @pl.when(c == 0)
    def _():
        s_ref[...] = s0_ref[...]

    cos = cos_ref[...]
    sin = sin_ref[...]
    nt = (((1,), (1,)), ((), ()))
    tn = (((0,), (0,)), ((), ()))
    units = [(b, h, slice(h * R_HD, (h + 1) * R_HD)) for b in range(bb) for h in range(R_HEADS)]
    idx = range(len(units))

    def rope(x):
        return x * cos + pltpu.roll(x, R_HD // 2, axis=1) * sin

    qr = [rope(q_ref[b, :, cols]) for b, h, cols in units]
    kr = [rope(k_ref[b, :, cols]) * (R_HD ** -0.5) for b, h, cols in units]
    v = [v_ref[b, :, cols].astype(BF16) for b, h, cols in units]
    s = [s_ref[b, h] for b, h, cols in units]
    sc = [lax.dot_general(qr[i].astype(BF16), kr[i].astype(BF16), nt, preferred_element_type=F32)
          * dm_ref[units[i][1]] for i in idx]
    cross = [jnp.dot((qr[i] * qd_ref[:, units[i][2]]).astype(BF16), s[i].astype(BF16),
                     preferred_element_type=F32) for i in idx]
    kv = [lax.dot_general((kr[i] * kd_ref[:, units[i][2]]).astype(BF16), v[i], tn,
                          preferred_element_type=F32) for i in idx]
    o = [cross[i] + jnp.dot(sc[i].astype(BF16), v[i], preferred_element_type=F32) for i in idx]
    for i, (b, h, cols) in enumerate(units):
        s_ref[b, h] = chunk_decay[h] * s[i] + kv[i]
        on = o[i] * lax.rsqrt(jnp.mean(o[i] * o[i], axis=-1, keepdims=True) + EPS)
        g = g_ref[b, :, cols]
        y_ref[b, :, cols] = (on * (g * jax.nn.sigmoid(g))).astype(BF16)

    @pl.when(c == n_c - 1)
    def _():
        so_ref[...] = s_ref[...]


def _ret_tables(cl, pos0, length):
    half = R_HD // 2
    inv = ROPE_BASE ** (-jnp.arange(half, dtype=F32) / half)
    pos = pos0 + jnp.arange(length, dtype=F32)
    ang = pos[:, None] * inv[None, :]
    cos = jnp.cos(ang)
    sin = jnp.sin(ang)
    cos_t = jnp.concatenate([cos, cos], axis=-1)
    sin_t = jnp.concatenate([-sin, sin], axis=-1)
    log_gamma = np.log(1.0 - 2.0 ** (-5.0 - np.arange(R_HEADS, dtype=np.float64)))
    idx = np.arange(cl, dtype=np.float64)
    diff = idx[:, None] - idx[None, :]
    dmat = np.where(diff >= 0, np.exp(np.maximum(diff, 0.0)[None] * log_gamma[:, None, None]), 0.0)
    kdec = np.exp((cl - 1.0 - idx)[:, None] * log_gamma[None, :])
    qdec = np.exp((idx + 1.0)[:, None] * log_gamma[None, :])
    chunk_decay = tuple(float(x) for x in np.exp(cl * log_gamma))
    rep = lambda a: jnp.asarray(np.repeat(a, R_HD, axis=1), F32)
    return cos_t, sin_t, jnp.asarray(dmat, F32), rep(qdec), rep(kdec), chunk_decay


def _retention(p3d, s0_all, layer, pos0, *, bb):
    bsz, length, _ = p3d.shape
    cl = min(R_CHUNK, length)
    n_c = length // cl
    cos_t, sin_t, dmat, qdec, kdec, chunk_decay = _ret_tables(cl, pos0, length)

    def col(j):
        return pl.BlockSpec((bb, cl, W_GROUP), lambda b, c: (b, c, j))

    tab = pl.BlockSpec((cl, R_HD), lambda b, c: (c, 0))
    state = pl.BlockSpec((bb, R_HEADS, R_HD, R_HD), lambda b, c: (b, 0, 0, 0))
    return pl.pallas_call(
        functools.partial(_ret_body, bb=bb, n_c=n_c, chunk_decay=chunk_decay),
        grid=(bsz // bb, n_c),
        in_specs=[col(COL_R_Q), col(COL_R_K), col(COL_R_V), col(COL_R_G), tab, tab,
                  pl.BlockSpec(dmat.shape, lambda b, c: (0, 0, 0)),
                  pl.BlockSpec(qdec.shape, lambda b, c: (0, 0)),
                  pl.BlockSpec(kdec.shape, lambda b, c: (0, 0)),
                  pl.BlockSpec((None, bb, R_HEADS, R_HD, R_HD), lambda b, c: (layer, b, 0, 0, 0))],
        out_specs=[pl.BlockSpec((bb, cl, W_GROUP), lambda b, c: (b, c, 0)), state],
        out_shape=[jax.ShapeDtypeStruct((bsz, length, W_GROUP), BF16),
                   jax.ShapeDtypeStruct(s0_all.shape[1:], F32)],
        scratch_shapes=[pltpu.VMEM((bb, R_HEADS, R_HD, R_HD), F32)],
        compiler_params=_cparams(("parallel", "arbitrary")),
        name="retention",
    )(p3d, p3d, p3d, p3d, cos_t, sin_t, dmat, qdec, kdec, s0_all)


def _conv_body(bg_ref, cg_ref, h_ref, buf_ref, w_ref, y_ref, st_ref, carry_ref, *, bb, tt):
    @pl.when(pl.program_id(1) == 0)
    def _():
        carry_ref[...] = buf_ref[...]

    shape = (bb, tt, W_GROUP)
    z = cg_ref[...] * h_ref[...]
    z2 = z.reshape(bb * tt, W_GROUP)
    r1 = pltpu.roll(z2, 1, axis=0).reshape(shape)
    r2 = pltpu.roll(z2, 2, axis=0).reshape(shape)
    tpos = lax.broadcasted_iota(jnp.int32, shape, 1)
    c0 = carry_ref[:, 0:1, :]
    c1 = carry_ref[:, 1:2, :]
    zm1 = jnp.where(tpos == 0, c1, r1)
    zm2 = jnp.where(tpos == 0, c0, jnp.where(tpos == 1, c1, r2))
    y = w_ref[0:1, :] * zm2 + w_ref[1:2, :] * zm1 + w_ref[2:3, :] * z
    y_ref[...] = (bg_ref[...] * y).astype(BF16)
    new = cg_ref[:, tt - 2:tt, :] * h_ref[:, tt - 2:tt, :]
    carry_ref[...] = new
    st_ref[...] = new


def _conv(p3d, buf, w, *, bb, tt):
    bsz, length, _ = p3d.shape

    def col(j):
        return pl.BlockSpec((bb, tt, W_GROUP), lambda b, t: (b, t, j))

    state = pl.BlockSpec((bb, C_WIDTH - 1, W_GROUP), lambda b, t: (b, 0, 0))
    return pl.pallas_call(
        functools.partial(_conv_body, bb=bb, tt=tt),
        grid=(bsz // bb, length // tt),
        in_specs=[col(COL_C_B), col(COL_C_C), col(COL_C_H), state,
                  pl.BlockSpec(w.shape, lambda b, t: (0, 0))],
        out_specs=[pl.BlockSpec((bb, tt, W_GROUP), lambda b, t: (b, t, 0)), state],
        out_shape=[jax.ShapeDtypeStruct((bsz, length, W_GROUP), BF16),
                   jax.ShapeDtypeStruct(buf.shape, F32)],
        scratch_shapes=[pltpu.VMEM((bb, C_WIDTH - 1, W_GROUP), F32)],
        compiler_params=_cparams(("parallel", "arbitrary")),
        name="short_conv",
    )(p3d, p3d, p3d, buf, w)


def _group_dot(xb, ones_bd):
    wb = ones_bd.shape[0]
    return jnp.concatenate(
        [jnp.dot(xb[:, i * wb:(i + 1) * wb], ones_bd, preferred_element_type=F32)
         for i in range(W_GROUP // wb)], axis=1)


def _head_sum(x, ones_bd):
    hi = x.astype(BF16)
    lo = (x - hi.astype(F32)).astype(BF16)
    return _group_dot(hi, ones_bd) + _group_dot(lo, ones_bd)


def _rwkvc_body(r_ref, k_ref, v_ref, lo_ref, sh_ref, s0_ref,
                mu_ref, mulo_ref, w0_ref, w2_ref, a0_ref, a2_ref, g2_ref, kk_ref, ka_ref, rk_ref,
                lnw_ref, lnb_ref, ones_ref,
                y_ref, so_ref,
                s2_ref, carry_ref, carrylo_ref, pt_s, rt_s, qh_s, kh_s, qb_s, kb_s, v_s, ec_s, ys,
                *, bb, tt, n_t, gsz, chunk):
    tb = pl.program_id(1)
    n = bb * tt
    w3 = (bb, tt, W_GROUP)
    ones_bd = ones_ref[...]
    n_pair = K_HEADS // 2
    pair_w = 2 * K_HD
    zero_blk = jnp.zeros((K_HD, K_HD), F32)

    @pl.when(tb == 0)
    def _():
        for b in range(bb):
            for p in range(n_pair):
                top = jnp.concatenate([s0_ref[b, 2 * p], zero_blk], axis=1)
                bot = jnp.concatenate([zero_blk, s0_ref[b, 2 * p + 1]], axis=1)
                s2_ref[b, p] = jnp.concatenate([top, bot], axis=0)
        carry_ref[...] = sh_ref[:, :, 0:3 * W_GROUP]
        carrylo_ref[...] = sh_ref[:, :, 3 * W_GROUP:K_COLS]

    tpos = lax.broadcasted_iota(jnp.int32, w3, 1)
    tpos_lo = lax.broadcasted_iota(jnp.int32, (bb, tt, LORA_COLS), 1)

    def shifted(x, carry, mu, mask):
        prev = pltpu.roll(x.reshape(n, x.shape[-1]), 1, axis=0).reshape(x.shape)
        prev = jnp.where(mask == 0, carry, prev)
        return (x + (prev - x) * mu).reshape(n, x.shape[-1])

    r_in, k_in, v_in, lo_in = r_ref[...], k_ref[...], v_ref[...], lo_ref[...]
    r = shifted(r_in, carry_ref[:, :, 0:W_GROUP], mu_ref[:, 0:W_GROUP], tpos)
    k = shifted(k_in, carry_ref[:, :, W_GROUP:2 * W_GROUP], mu_ref[:, W_GROUP:2 * W_GROUP], tpos)
    v = shifted(v_in, carry_ref[:, :, 2 * W_GROUP:3 * W_GROUP], mu_ref[:, 2 * W_GROUP:3 * W_GROUP], tpos)
    lo = shifted(lo_in, carrylo_ref[...], mulo_ref[...], tpos_lo)
    carry_ref[:, :, 0:W_GROUP] = r_ref[:, tt - 1:tt, :]
    carry_ref[:, :, W_GROUP:2 * W_GROUP] = k_ref[:, tt - 1:tt, :]
    carry_ref[:, :, 2 * W_GROUP:3 * W_GROUP] = v_ref[:, tt - 1:tt, :]
    carrylo_ref[...] = lo_ref[:, tt - 1:tt, :]

    zw = w0_ref[...] + jnp.dot(jnp.tanh(lo).astype(BF16), w2_ref[...], preferred_element_type=F32)
    logw = -RWKV_DECAY_SCALE * jax.nn.sigmoid(zw)
    a = jax.nn.sigmoid(a0_ref[...] + jnp.dot(lo.astype(BF16), a2_ref[...], preferred_element_type=F32))
    gate = jnp.dot(jax.nn.sigmoid(lo).astype(BF16), g2_ref[...], preferred_element_type=F32)
    kk = k * kk_ref[...]
    kk = kk * lax.rsqrt(jnp.maximum(_head_sum(kk * kk, ones_bd), 1e-24))
    k2 = k * (1.0 + (a - 1.0) * ka_ref[...])
    bonus = _head_sum(r * k2 * rk_ref[...], ones_bd) * v
    q = -(kk * a)

    cpos = lax.broadcasted_iota(jnp.int32, (n, W_GROUP), 0) % chunk
    g = logw
    step = 1
    while step < chunk:
        g = g + jnp.where(cpos >= step, pltpu.roll(g, step, axis=0), 0.0)
        step *= 2
    g3 = g.reshape(n // chunk, chunk, W_GROUP)
    gtot = jnp.broadcast_to(g3[:, chunk - 1:chunk, :], g3.shape).reshape(n, W_GROUP)
    e_neg = jnp.exp(-g)
    e_rem = jnp.exp(gtot - g)
    pt_s[...] = (kk * jnp.exp(g - logw)).reshape(w3)
    rt_s[...] = (r * jnp.exp(g)).reshape(w3)
    qh_s[...] = (q * e_neg).reshape(w3)
    kh_s[...] = (k2 * e_neg).reshape(w3)
    qb_s[...] = (q * e_rem).reshape(w3)
    kb_s[...] = (k2 * e_rem).reshape(w3)
    v_s[...] = v.reshape(w3)
    ec_s[...] = jnp.exp(gtot).reshape(w3)

    c2, c4 = 2 * chunk, 4 * chunk
    even = lax.broadcasted_iota(jnp.int32, (chunk, pair_w), 1) < K_HD
    ri = lax.broadcasted_iota(jnp.int32, (c4, c4), 0)
    ci = lax.broadcasted_iota(jnp.int32, (c4, c4), 1)
    keep = ci % chunk < ri % chunk + ri // c2
    right = lax.broadcasted_iota(jnp.int32, (c2, c4), 1) >= c2
    eye = (lax.broadcasted_iota(jnp.int32, (c2, c2), 0)
           == lax.broadcasted_iota(jnp.int32, (c2, c2), 1)).astype(F32)
    nt = (((1,), (1,)), ((), ()))
    tn = (((0,), (0,)), ((), ()))
    n_double = chunk.bit_length() - 2

    def two(x):
        return jnp.concatenate([jnp.where(even, x, 0.0), jnp.where(even, 0.0, x)], axis=0)

    def mm(x, y):
        return jnp.dot(x.astype(BF16), y.astype(BF16), preferred_element_type=F32)

    def units(args):
        v2, pr, qk, qkb, s2, ec = zip(*args)
        idx = range(len(args))
        apr = [jnp.where(keep, lax.dot_general(pr[i], qk[i], nt, preferred_element_type=F32), 0.0) for i in idx]
        prs = [lax.dot_general(pr[i], s2[i].astype(BF16), nt, preferred_element_type=F32) for i in idx]
        apk = [mm(jnp.where(right, apr[i][0:c2], 0.0), jnp.concatenate([v2[i], v2[i]], axis=0)) for i in idx]
        power = [apr[i][0:c2, 0:c2] for i in idx]
        inv = [eye + power[i] for i in idx]
        for _ in range(n_double):
            power = [mm(power[i], power[i]) for i in idx]
            inv = [inv[i] + mm(inv[i], power[i]) for i in idx]
        u2 = [mm(inv[i], prs[i][0:c2] + apk[i]) for i in idx]
        uv = [jnp.concatenate([u2[i], v2[i]], axis=0).astype(BF16) for i in idx]
        y2 = [prs[i][c2:c4] + jnp.dot(apr[i][c2:c4].astype(BF16), uv[i], preferred_element_type=F32)
              for i in idx]
        s_new = [s2[i] * ec[i] + lax.dot_general(uv[i], qkb[i], tn, preferred_element_type=F32) for i in idx]
        return [(y2[i][0:chunk] + y2[i][chunk:c2], s_new[i]) for i in idx]

    def chunk_step(i, carry):
        c = i // (bb // gsz)
        b0 = (i % (bb // gsz)) * gsz
        r0 = pl.multiple_of(c * chunk, chunk)
        rows = pl.ds(r0, chunk)
        where = [(b0 + j, slice(p * pair_w, (p + 1) * pair_w), p) for j in range(gsz) for p in range(n_pair)]
        args = []
        for b, lanes, p in where:
            ld = lambda ref: two(ref[b, rows, lanes])
            args.append((ld(v_s),
                         jnp.concatenate([ld(pt_s), ld(rt_s)], axis=0).astype(BF16),
                         jnp.concatenate([ld(qh_s), ld(kh_s)], axis=0).astype(BF16),
                         jnp.concatenate([ld(qb_s), ld(kb_s)], axis=0).astype(BF16),
                         s2_ref[b, p], ec_s[b, pl.ds(r0, 1), lanes]))
        for (b, lanes, p), (y, s_new) in zip(where, units(args)):
            ys[b, rows, lanes] = y
            s2_ref[b, p] = s_new
        return carry

    lax.fori_loop(0, (tt // chunk) * (bb // gsz), chunk_step, 0)

    y = ys[...].reshape(n, W_GROUP)
    mean = _head_sum(y, ones_bd) * (1.0 / K_HD)
    yc = y - mean
    var = _head_sum(yc * yc, ones_bd) * (1.0 / K_HD)
    out = (yc * lax.rsqrt(var + GN_EPS) * lnw_ref[...] + lnb_ref[...] + bonus) * gate
    y_ref[...] = out.reshape(w3).astype(BF16)

    @pl.when(tb == n_t - 1)
    def _():
        for b in range(bb):
            for p in range(n_pair):
                so_ref[b, 2 * p] = s2_ref[b, p, 0:K_HD, 0:K_HD]
                so_ref[b, 2 * p + 1] = s2_ref[b, p, K_HD:pair_w, K_HD:pair_w]


def _rwkvc(p3d, shift, s0_all, layer, prm, *, bb, tt, gsz, chunk):
    bsz, length, _ = p3d.shape
    n_t = length // tt

    def col(j):
        return pl.BlockSpec((bb, tt, W_GROUP), lambda b, t: (b, t, j))

    def whole(a):
        nd = a.ndim
        return pl.BlockSpec(a.shape, lambda b, t: (0,) * nd)

    state = pl.BlockSpec((bb, K_HEADS, K_HD, K_HD), lambda b, t: (b, 0, 0, 0))
    params = [prm[nm] for nm in ("mu", "mu_lo", "w0", "w2", "a0", "a2", "g2", "k_k", "k_a", "r_k",
                                 "ln_w", "ln_b", "ones_bd")]
    blk = pltpu.VMEM((bb, tt, W_GROUP), F32)
    return pl.pallas_call(
        functools.partial(_rwkvc_body, bb=bb, tt=tt, n_t=n_t, gsz=gsz, chunk=chunk),
        grid=(bsz // bb, n_t),
        in_specs=[col(COL_K_R), col(COL_K_K), col(COL_K_V),
                  pl.BlockSpec((bb, tt, LORA_COLS), lambda b, t: (b, t, COL_K_LORA)),
                  pl.BlockSpec((bb, 1, K_COLS), lambda b, t: (b, 0, 0)),
                  pl.BlockSpec((None, bb, K_HEADS, K_HD, K_HD), lambda b, t: (layer, b, 0, 0, 0))]
                 + [whole(a) for a in params],
        out_specs=[pl.BlockSpec((bb, tt, W_GROUP), lambda b, t: (b, t, 0)), state],
        out_shape=[jax.ShapeDtypeStruct((bsz, length, W_GROUP), BF16),
                   jax.ShapeDtypeStruct(s0_all.shape[1:], F32)],
        scratch_shapes=[pltpu.VMEM((bb, K_HEADS // 2, 2 * K_HD, 2 * K_HD), F32),
                        pltpu.VMEM((bb, 1, 3 * W_GROUP), F32),
                        pltpu.VMEM((bb, 1, LORA_COLS), F32),
                        blk, blk, blk, blk, blk, blk, blk, blk, blk],
        compiler_params=_cparams(("parallel", "arbitrary")),
        name="rwkv7c",
    )(p3d, p3d, p3d, p3d, shift, s0_all, *params)


def _gate_mixing(w_s, b_s, seq):
    cl = min(A_CHUNK, seq)
    wm = jnp.tril(w_s[:, :cl, :cl])
    bias = b_s[:, :cl]
    rep = A_CHUNK // cl
    if rep > 1:
        eye = jnp.eye(rep, dtype=w_s.dtype)
        wm = jnp.einsum("ab,hts->hatbs", eye, wm).reshape(w_s.shape[0], A_CHUNK, A_CHUNK)
        bias = jnp.tile(bias, (1, rep))
    bias = jnp.repeat(bias.T, A_CHUNK, axis=1)
    return wm.astype(BF16), bias


def _pad_rows(w, row0):
    return jnp.zeros((LORA_COLS, W_GROUP), F32).at[row0:row0 + w.shape[0]].set(w).astype(BF16)


def _layer_params(l, ffn1_norm, ffn1_w_gate, ffn1_w_up, ffn1_w_down, mix_norm, w_in, w_out,
                  a_w_s, a_b_s, a_ln_g, a_ln_b, c_conv_w,
                  k_mu, k_w0, k_w2, k_a0, k_a2, k_g2, k_k_k, k_k_a, k_r_k, k_ln_w, k_ln_b,
                  ffn2_norm, ffn2_w_gate, ffn2_w_up, ffn2_w_down):
    row = lambda a: a[l].reshape(1, -1)
    head_of = np.arange(V7X_MXU_DIM) // K_HD
    ones_bd = jnp.asarray(head_of[:, None] == head_of[None, :], BF16)
    rwkv = dict(
        mu=k_mu[l][None, :3 * W_GROUP], mu_lo=k_mu[l][None, 3 * W_GROUP:],
        w0=row(k_w0), w2=_pad_rows(k_w2[l], 0),
        a0=row(k_a0), a2=_pad_rows(k_a2[l], W_LORA),
        g2=_pad_rows(k_g2[l], W_LORA + A_LORA),
        k_k=row(k_k_k), k_a=row(k_k_a), r_k=row(k_r_k), ln_w=row(k_ln_w), ln_b=row(k_ln_b),
        ones_bd=ones_bd)
    return dict(
        layer=l, ffn_layer=l,
        ffn1=(row(ffn1_norm), ffn1_w_gate, ffn1_w_up, ffn1_w_down),
        ffn2=(row(ffn2_norm), ffn2_w_gate, ffn2_w_up, ffn2_w_down),
        mix_norm=row(mix_norm), w_in=w_in, w_out=w_out,
        a_w_s=a_w_s[l], a_b_s=a_b_s[l], a_ln_g=row(a_ln_g), a_ln_b=row(a_ln_b),
        conv_w=c_conv_w[l], rwkv=rwkv)


def _stream_layer(x2d, bsz, length, pos0, ret_s0_all, ret_layer, conv_buf, rw_shift, rw_s0_all, p, fn, *,
                  final, cfg):
    layer = p["layer"]
    ffn_kw = dict(layer=p["ffn_layer"], tm=cfg["ffn_tm"], tf=cfg["tf"], cast=cfg["ffn_cast"])
    x1 = _ffn(x2d, *p["ffn1"], fn, final=False, **ffn_kw)
    casted = {}
    if cfg["ffn_cast"]:
        x1, casted["ffn1"] = x1
    proj = _proj(x1, p["mix_norm"], p["w_in"], layer=p["ffn_layer"], tm=cfg["proj_tm"], tn=cfg["tn"],
                 cast=cfg["ffn_cast"])
    if cfg["ffn_cast"]:
        proj, casted["w_in"] = proj
    p3d = proj.reshape(bsz, length, proj.shape[1])

    wm, bias = _gate_mixing(p["a_w_s"], p["a_b_s"], length)
    ya, v_rows = _gate(proj, wm, bias, p["a_ln_g"], p["a_ln_b"], tm=cfg["gate_tm"],
                       with_rows=cfg["with_rows"])
    yb, ret_s = _retention(p3d, ret_s0_all, ret_layer, pos0, bb=cfg["ret_bb"])
    yc, conv_new = _conv(p3d, conv_buf, p["conv_w"], bb=cfg["conv_bb"], tt=cfg["conv_tt"])
    yd, rw_s = _rwkvc(p3d, rw_shift[:, None, :], rw_s0_all, ret_layer, p["rwkv"],
                      bb=cfg["rwkv_bb"], tt=cfg["rwkv_tt"], gsz=cfg["rwkv_gsz"], chunk=cfg["rwkv_chunk"])
    shift_new = p3d[:, length - 1, K_COL0:K_COL0 + K_COLS]

    flat = lambda y: y.reshape(bsz * length, W_GROUP)
    x2 = _outproj(x1, ya, flat(yb), flat(yc), flat(yd), p["w_out"], layer=layer, tm=cfg["tm"])
    x3 = _ffn(x2, *p["ffn2"], fn, final=final, **ffn_kw)
    if cfg["ffn_cast"]:
        x3, casted["ffn2"] = x3
    return x3, ret_s, conv_new, shift_new, rw_s, v_rows, casted


def _stream_cfg(bsz, length, sample):
    m = bsz * length
    tm = min(512, m)
    if sample:
        return dict(tm=tm, ffn_tm=m, ffn_cast=True, proj_tm=m, tf=256, tn=640,
                    gate_tm=min(512, m), with_rows=True,
                    ret_bb=8, conv_bb=32, conv_tt=length, rwkv_bb=16, rwkv_tt=length, rwkv_gsz=4,
                    rwkv_chunk=min(RWKV_CHUNK, length))
    return dict(tm=tm, ffn_tm=tm, ffn_cast=False, proj_tm=min(1024, m), tf=512, tn=1280,
                gate_tm=min(512, m), with_rows=False,
                ret_bb=1, conv_bb=1, conv_tt=min(512, length), rwkv_bb=bsz, rwkv_tt=min(128, length),
                rwkv_gsz=4, rwkv_chunk=min(RWKV_CHUNK, length))


def kernel(x_prompt, x_sample, state_ret, state_conv, state_rwkv_shift, state_rwkv, ffn1_norm, ffn1_w_gate, ffn1_w_up, ffn1_w_down, mix_norm, w_in, w_out, a_w_s, a_b_s, a_ln_g, a_ln_b, c_conv_w, k_mu, k_w0, k_w2, k_a0, k_a2, k_g2, k_k_k, k_k_a, k_r_k, k_ln_w, k_ln_b, ffn2_norm, ffn2_w_gate, ffn2_w_up, ffn2_w_down, final_norm):
    bp, lp, d = x_prompt.shape
    bs, ls, _ = x_sample.shape
    depth = ffn1_norm.shape[0]
    cfg_p = _stream_cfg(bp, lp, sample=False)
    cfg_s = _stream_cfg(bs, ls, sample=True)
    fn = final_norm.reshape(1, d)

    zero_ret = jnp.zeros((1, bp, R_HEADS, R_HD, R_HD), F32)
    zero_conv = jnp.zeros((bp, C_WIDTH - 1, W_GROUP), F32)
    zero_shift = jnp.zeros((bp, K_COLS), F32)
    zero_rw = jnp.zeros((1, bp, K_HEADS, K_HD, K_HD), F32)

    w_out = w_out.astype(BF16)
    xp = x_prompt.reshape(bp * lp, d)
    xs = x_sample.reshape(bs * ls, d)
    outs = [[] for _ in range(9)]
    for l in range(depth):
        p = _layer_params(l, ffn1_norm, ffn1_w_gate, ffn1_w_up, ffn1_w_down, mix_norm, w_in, w_out,
                          a_w_s, a_b_s, a_ln_g, a_ln_b, c_conv_w,
                          k_mu, k_w0, k_w2, k_a0, k_a2, k_g2, k_k_k, k_k_a, k_r_k, k_ln_w, k_ln_b,
                          ffn2_norm, ffn2_w_gate, ffn2_w_up, ffn2_w_down)
        final = l == depth - 1
        xs, rs, cs, ss, ws, vs, w16 = _stream_layer(xs, bs, ls, float(PAST_LEN), state_ret, l, state_conv[l],
                                                    state_rwkv_shift[l], state_rwkv, p, fn,
                                                    final=final, cfg=cfg_s)
        p = dict(p, ffn_layer=0, ffn1=p["ffn1"][:1] + w16["ffn1"], ffn2=p["ffn2"][:1] + w16["ffn2"],
                 w_in=w16["w_in"])
        xp, rp, cp, sp, wp, _, _ = _stream_layer(xp, bp, lp, 0.0, zero_ret, 0, zero_conv, zero_shift, zero_rw,
                                                 p, fn, final=final, cfg=cfg_p)
        for acc, val in zip(outs, (rp, rs, cp, cs, sp, ss, wp, ws, vs.reshape(bs, ls, W_GROUP))):
            acc.append(val)

    return (xp.reshape(bp, lp, d), xs.reshape(bs, ls, d)) + tuple(jnp.stack(o) for o in outs)
```

```python
import functools

import numpy as np
import jax
import jax.numpy as jnp
from jax import lax
from jax.experimental import pallas as pl
from jax.experimental.pallas import tpu as pltpu

F32 = jnp.float32
BF16 = jnp.bfloat16

W_GROUP = 512
A_CHUNK = 128
R_HEADS = 4
R_HD = 128
R_CHUNK = 128
ROPE_BASE = 10000.0
C_WIDTH = 3
K_HD = 64
K_HEADS = 8
W_LORA = 64
A_LORA = 64
G_LORA = 128
LORA_COLS = W_LORA + A_LORA + G_LORA
K_COLS = 3 * W_GROUP + LORA_COLS
EPS = 1e-6
GN_EPS = 64e-5
PAST_LEN = 16384

COL_A_U, COL_A_V = 0, 1
COL_R_Q, COL_R_K, COL_R_V, COL_R_G = 2, 3, 4, 5
COL_C_B, COL_C_C, COL_C_H = 6, 7, 8
COL_K_R, COL_K_K, COL_K_V = 9, 10, 11
COL_K_LORA = (12 * W_GROUP) // LORA_COLS
K_COL0 = 9 * W_GROUP

V7X_VMEM_LIMIT_BYTES = 62 * 1024 * 1024
RWKV_DECAY_SCALE =float(np.exp(-0.5))
RWKV_CHUNK = 64
V7X_MXU_DIM = 256


def _cparams(sem, vmem=V7X_VMEM_LIMIT_BYTES):
    return pltpu.CompilerParams(dimension_semantics=sem, vmem_limit_bytes=vmem)


def _rms(x, w):
    return x * lax.rsqrt(jnp.mean(x * x, axis=-1, keepdims=True) + EPS) * w


def _ffn_body(x_ref, nw_ref, wg_ref, wu_ref, wd_ref, fn_ref, o_ref, *rest, n_f, final, cast):
    j = pl.program_id(1)
    hn_ref = rest[-1]
    if cast:
        casted = rest[:3]
        for src, dst in zip((wg_ref, wu_ref, wd_ref), casted):
            dst[...] = src[...].astype(BF16)
        wg_ref, wu_ref, wd_ref = casted

    @pl.when(j == 0)
    def _():
        hn_ref[...] = _rms(x_ref[...], nw_ref[...]).astype(BF16)
        o_ref[...] = jnp.zeros_like(o_ref)

    h = hn_ref[...]
    tf = wg_ref.shape[1]
    acc = o_ref[...]
    pending = None
    for c0 in range(0, tf, V7X_MXU_DIM):
        c1 = min(c0 + V7X_MXU_DIM, tf)
        g = jnp.dot(h, wg_ref[:, c0:c1], preferred_element_type=F32)
        u = jnp.dot(h, wu_ref[:, c0:c1], preferred_element_type=F32)
        if pending is not None:
            acc = acc + jnp.dot(pending[0], wd_ref[pending[1]:pending[2], :], preferred_element_type=F32)
        pending = ((g * jax.nn.sigmoid(g) * u).astype(BF16), c0, c1)
    o_ref[...] = acc + jnp.dot(pending[0], wd_ref[pending[1]:pending[2], :], preferred_element_type=F32)

    @pl.when(j == n_f - 1)
    def _():
        y = x_ref[...] + 0.5 * o_ref[...]
        if final:
            y = _rms(y, fn_ref[...])
        o_ref[...] = y


def _ffn(x, nw, wg, wu, wd, fn, *, layer, final, tm, tf, cast=False):
    m, d = x.shape
    f = wg.shape[2]
    n_f = f // tf
    out_specs = [pl.BlockSpec((tm, d), lambda i, j: (i, 0))]
    out_shape = [jax.ShapeDtypeStruct((m, d), F32)]
    if cast:
        assert m == tm, "each weight block must be visited once"
        out_specs += [pl.BlockSpec((None, d, tf), lambda i, j: (0, 0, j)),
                      pl.BlockSpec((None, d, tf), lambda i, j: (0, 0, j)),
                      pl.BlockSpec((None, tf, d), lambda i, j: (0, j, 0))]
        out_shape += [jax.ShapeDtypeStruct((1, d, f), BF16), jax.ShapeDtypeStruct((1, d, f), BF16),
                      jax.ShapeDtypeStruct((1, f, d), BF16)]
    res = pl.pallas_call(
        functools.partial(_ffn_body, n_f=n_f, final=final, cast=cast),
        grid=(m // tm, n_f),
        in_specs=[
            pl.BlockSpec((tm, d), lambda i, j: (i, 0)),
            pl.BlockSpec((1, d), lambda i, j: (0, 0)),
            pl.BlockSpec((None, d, tf), lambda i, j: (layer, 0, j)),
            pl.BlockSpec((None, d, tf), lambda i, j: (layer, 0, j)),
            pl.BlockSpec((None, tf, d), lambda i, j: (layer, j, 0)),
            pl.BlockSpec((1, d), lambda i, j: (0, 0)),
        ],
        out_specs=out_specs,
        out_shape=out_shape,
        scratch_shapes=[pltpu.VMEM((tm, d), BF16)],
        compiler_params=_cparams(("parallel", "arbitrary")),
        name="ffn_final" if final else "ffn",
    )(x, nw, wg, wu, wd, fn)
    return (res[0], tuple(res[1:])) if cast else res[0]


def _proj_body(x_ref, nw_ref, w_ref, o_ref, *rest, cast):
    hn_ref = rest[-1]
    if cast:
        rest[0][...] = w_ref[...].astype(BF16)
        w_ref = rest[0]

    @pl.when(pl.program_id(1) == 0)
    def _():
        hn_ref[...] = _rms(x_ref[...], nw_ref[...]).astype(BF16)

    o_ref[...] = jnp.dot(hn_ref[...], w_ref[...], preferred_element_type=F32)


def _proj(x, nw, w, *, layer, tm, tn, cast=False):
    m, d = x.shape
    n = w.shape[2]
    out_specs = [pl.BlockSpec((tm, tn), lambda i, j: (i, j))]
    out_shape = [jax.ShapeDtypeStruct((m, n), F32)]
    if cast:
        assert m == tm, "each weight block must be visited once"
        out_specs.append(pl.BlockSpec((None, d, tn), lambda i, j: (0, 0, j)))
        out_shape.append(jax.ShapeDtypeStruct((1, d, n), BF16))
    res = pl.pallas_call(
        functools.partial(_proj_body, cast=cast),
        grid=(m // tm, n // tn),
        in_specs=[
            pl.BlockSpec((tm, d), lambda i, j: (i, 0)),
            pl.BlockSpec((1, d), lambda i, j: (0, 0)),
            pl.BlockSpec((None, d, tn), lambda i, j: (layer, 0, j)),
        ],
        out_specs=out_specs,
        out_shape=out_shape,
        scratch_shapes=[pltpu.VMEM((tm, d), BF16)],
        compiler_params=_cparams(("parallel", "arbitrary")),
        name="in_proj",
    )(x, nw, w)
    return tuple(res) if cast else res[0]


def _outproj_body(x_ref, ya_ref, yb_ref, yc_ref, yd_ref, w_ref, o_ref):
    acc = x_ref[...]
    for gi, y_ref in enumerate((ya_ref, yb_ref, yc_ref, yd_ref)):
        acc = acc + jnp.dot(y_ref[...], w_ref[gi * W_GROUP:(gi + 1) * W_GROUP, :],
                            preferred_element_type=F32)
    o_ref[...] = acc


def _outproj(x, ya, yb, yc, yd, w, *, layer, tm):
    m, d = x.shape
    yspec = pl.BlockSpec((tm, W_GROUP), lambda i: (i, 0))
    return pl.pallas_call(
        _outproj_body,
        grid=(m // tm,),
        in_specs=[pl.BlockSpec((tm, d), lambda i: (i, 0)), yspec, yspec, yspec, yspec,
                  pl.BlockSpec((None,) + w.shape[1:], lambda i: (layer, 0, 0))],
        out_specs=pl.BlockSpec((tm, d), lambda i: (i, 0)),
        out_shape=jax.ShapeDtypeStruct((m, d), F32),
        compiler_params=_cparams(("parallel",)),
        name="out_proj",
    )(x, ya, yb, yc, yd, w)


def _gate_body(u_ref, v_ref, wm_ref, bias_ref, g_ref, b_ref, y_ref, *vr_ref, tm):
    gu = jax.nn.gelu(u_ref[...], approximate=True)
    gv = jax.nn.gelu(v_ref[...], approximate=True)
    mu = jnp.mean(gv, axis=-1, keepdims=True)
    var = jnp.mean(jnp.square(gv - mu), axis=-1, keepdims=True)
    vn = (gv - mu) * lax.rsqrt(var + EPS) * g_ref[...] + b_ref[...]
    if vr_ref:
        vr_ref[0][...] = vn
    vnb = vn.astype(BF16)
    for c in range(tm // A_CHUNK):
        rows = slice(c * A_CHUNK, (c + 1) * A_CHUNK)
        for h in range(W_GROUP // A_CHUNK):
            cols = slice(h * A_CHUNK, (h + 1) * A_CHUNK)
            z = jnp.dot(wm_ref[h], vnb[rows, cols], preferred_element_type=F32) + bias_ref[:, cols]
            y_ref[rows, cols] = (gu[rows, cols] * z).astype(BF16)


def _gate(p2d, wm, bias, ln_g, ln_b, *, tm, with_rows):
    m = p2d.shape[0]
    row_spec = pl.BlockSpec((tm, W_GROUP), lambda i: (i, 0))
    out_shape = [jax.ShapeDtypeStruct((m, W_GROUP), BF16)]
    out_specs = [row_spec]
    if with_rows:
        out_shape.append(jax.ShapeDtypeStruct((m, W_GROUP), F32))
        out_specs.append(row_spec)
    res = pl.pallas_call(
        functools.partial(_gate_body, tm=tm),
        grid=(m // tm,),
        in_specs=[
            pl.BlockSpec((tm, W_GROUP), lambda i: (i, COL_A_U)),
            pl.BlockSpec((tm, W_GROUP), lambda i: (i, COL_A_V)),
            pl.BlockSpec(wm.shape, lambda i: (0, 0, 0)),
            pl.BlockSpec(bias.shape, lambda i: (0, 0)),
            pl.BlockSpec((1, W_GROUP), lambda i: (0, 0)),
            pl.BlockSpec((1, W_GROUP), lambda i: (0, 0)),
        ],
        out_specs=out_specs,
        out_shape=out_shape,
        compiler_params=_cparams(("parallel",)),
        name="spatial_gate",
    )(p2d, p2d, wm, bias, ln_g, ln_b)
    return res if with_rows else (res[0], None)


def _ret_body(q_ref, k_ref, v_ref, g_ref, cos_ref, sin_ref, dm_ref, qd_ref, kd_ref, s0_ref,
              *rest, bb, n_c, chunk_decay):
    y_ref, so_ref, s_ref = rest[-3:]
    c = pl.program_id(1)

    @pl.when(c == 0)
    def _():
        s_ref[...] = s0_ref[...]

    cos = cos_ref[...]
    sin = sin_ref[...]
    nt = (((1,), (1,)), ((), ()))
    tn = (((0,), (0,)), ((), ()))
    units = [(b, h, slice(h * R_HD, (h + 1) * R_HD)) for b in range(bb) for h in range(R_HEADS)]
    idx = range(len(units))

    def rope(x):
        return x * cos + pltpu.roll(x, R_HD // 2, axis=1) * sin

    qr = [rope(q_ref[b, :, cols]) for b, h, cols in units]
    kr = [rope(k_ref[b, :, cols]) * (R_HD ** -0.5) for b, h, cols in units]
    v = [v_ref[b, :, cols].astype(BF16) for b, h, cols in units]
    s = [s_ref[b, h] for b, h, cols in units]
    sc = [lax.dot_general(qr[i].astype(BF16), kr[i].astype(BF16), nt, preferred_element_type=F32)
          * dm_ref[units[i][1]] for i in idx]
    cross = [jnp.dot((qr[i] * qd_ref[:, units[i][2]]).astype(BF16), s[i].astype(BF16),
                     preferred_element_type=F32) for i in idx]
    kv = [lax.dot_general((kr[i] * kd_ref[:, units[i][2]]).astype(BF16), v[i], tn,
                          preferred_element_type=F32) for i in idx]
    o = [cross[i] + jnp.dot(sc[i].astype(BF16), v[i], preferred_element_type=F32) for i in idx]
    for i, (b, h, cols) in enumerate(units):
        s_ref[b, h] = chunk_decay[h] * s[i] + kv[i]
        on = o[i] * lax.rsqrt(jnp.mean(o[i] * o[i], axis=-1, keepdims=True) + EPS)
        g = g_ref[b, :, cols]
        y_ref[b, :, cols] = (on * (g * jax.nn.sigmoid(g))).astype(BF16)

    @pl.when(c == n_c - 1)
    def _():
        so_ref[...] = s_ref[...]


def _ret_tables(cl, pos0, length):
    half = R_HD // 2
    inv = ROPE_BASE ** (-jnp.arange(half, dtype=F32) / half)
    pos = pos0 + jnp.arange(length, dtype=F32)
    ang = pos[:, None] * inv[None, :]
    cos = jnp.cos(ang)
    sin = jnp.sin(ang)
    cos_t = jnp.concatenate([cos, cos], axis=-1)
    sin_t = jnp.concatenate([-sin, sin], axis=-1)
    log_gamma = np.log(1.0 - 2.0 ** (-5.0 - np.arange(R_HEADS, dtype=np.float64)))
    idx = np.arange(cl, dtype=np.float64)
    diff = idx[:, None] - idx[None, :]
    dmat = np.where(diff >= 0, np.exp(np.maximum(diff, 0.0)[None] * log_gamma[:, None, None]), 0.0)
    kdec = np.exp((cl - 1.0 - idx)[:, None] * log_gamma[None, :])
    qdec = np.exp((idx + 1.0)[:, None] * log_gamma[None, :])
    chunk_decay = tuple(float(x) for x in np.exp(cl * log_gamma))
    rep = lambda a: jnp.asarray(np.repeat(a, R_HD, axis=1), F32)
    return cos_t, sin_t, jnp.asarray(dmat, F32), rep(qdec), rep(kdec), chunk_decay


def _stack_slot(stack, per_layer_shape, block, n_inputs):
    prev, slot, depth = stack
    nd = len(per_layer_shape)
    spec = pl.BlockSpec((None,) + block, lambda b, t: (slot, b) + (0,) * (nd - 1))
    shape = jax.ShapeDtypeStruct((depth,) + tuple(per_layer_shape), F32)
    if prev is None:
        return spec, shape, [], [], {}
    return spec, shape, [prev], [pl.BlockSpec(memory_space=pl.ANY)], {n_inputs: 1}


def _retention(p3d, s0_all, layer, pos0, stack, *, bb):
    bsz, length, _ = p3d.shape
    cl = min(R_CHUNK, length)
    n_c = length // cl
    cos_t, sin_t, dmat, qdec, kdec, chunk_decay = _ret_tables(cl, pos0, length)

    def col(j):
        return pl.BlockSpec((bb, cl, W_GROUP), lambda b, c: (b, c, j))

    tab = pl.BlockSpec((cl, R_HD), lambda b, c: (c, 0))
    state, state_shape, extra, extra_specs, aliases = _stack_slot(
        stack, s0_all.shape[1:], (bb, R_HEADS, R_HD, R_HD), n_inputs=10)
    return pl.pallas_call(
        functools.partial(_ret_body, bb=bb, n_c=n_c, chunk_decay=chunk_decay),
        grid=(bsz // bb, n_c),
        in_specs=[col(COL_R_Q), col(COL_R_K), col(COL_R_V), col(COL_R_G), tab, tab,
                  pl.BlockSpec(dmat.shape, lambda b, c: (0, 0, 0)),
                  pl.BlockSpec(qdec.shape, lambda b, c: (0, 0)),
                  pl.BlockSpec(kdec.shape, lambda b, c: (0, 0)),
                  pl.BlockSpec((None, bb, R_HEADS, R_HD, R_HD), lambda b, c: (layer, b, 0, 0, 0))]
                 + extra_specs,
        out_specs=[pl.BlockSpec((bb, cl, W_GROUP), lambda b, c: (b, c, 0)), state],
        out_shape=[jax.ShapeDtypeStruct((bsz, length, W_GROUP), BF16), state_shape],
        input_output_aliases=aliases,
        scratch_shapes=[pltpu.VMEM((bb, R_HEADS, R_HD, R_HD), F32)],
        compiler_params=_cparams(("parallel", "arbitrary")),
        name="retention",
    )(p3d, p3d, p3d, p3d, cos_t, sin_t, dmat, qdec, kdec, s0_all, *extra)


def _conv_body(bg_ref, cg_ref, h_ref, buf_ref, w_ref, y_ref, st_ref, carry_ref, *, bb, tt):
    @pl.when(pl.program_id(1) == 0)
    def _():
        carry_ref[...] = buf_ref[...]

    shape = (bb, tt, W_GROUP)
    z = cg_ref[...] * h_ref[...]
    z2 = z.reshape(bb * tt, W_GROUP)
    r1 = pltpu.roll(z2, 1, axis=0).reshape(shape)
    r2 = pltpu.roll(z2, 2, axis=0).reshape(shape)
    tpos = lax.broadcasted_iota(jnp.int32, shape, 1)
    c0 = carry_ref[:, 0:1, :]
    c1 = carry_ref[:, 1:2, :]
    zm1 = jnp.where(tpos == 0, c1, r1)
    zm2 = jnp.where(tpos == 0, c0, jnp.where(tpos == 1, c1, r2))
    y = w_ref[0:1, :] * zm2 + w_ref[1:2, :] * zm1 + w_ref[2:3, :] * z
    y_ref[...] = (bg_ref[...] * y).astype(BF16)
    new = cg_ref[:, tt - 2:tt, :] * h_ref[:, tt - 2:tt, :]
    carry_ref[...] = new
    st_ref[...] = new


def _conv(p3d, buf, w, *, bb, tt):
    bsz, length, _ = p3d.shape

    def col(j):
        return pl.BlockSpec((bb, tt, W_GROUP), lambda b, t: (b, t, j))

    state = pl.BlockSpec((bb, C_WIDTH - 1, W_GROUP), lambda b, t: (b, 0, 0))
    return pl.pallas_call(
        functools.partial(_conv_body, bb=bb, tt=tt),
        grid=(bsz // bb, length // tt),
        in_specs=[col(COL_C_B), col(COL_C_C), col(COL_C_H), state,
                  pl.BlockSpec(w.shape, lambda b, t: (0, 0))],
        out_specs=[pl.BlockSpec((bb, tt, W_GROUP), lambda b, t: (b, t, 0)), state],
        out_shape=[jax.ShapeDtypeStruct((bsz, length, W_GROUP), BF16),
                   jax.ShapeDtypeStruct(buf.shape, F32)],
        scratch_shapes=[pltpu.VMEM((bb, C_WIDTH - 1, W_GROUP), F32)],
        compiler_params=_cparams(("parallel", "arbitrary")),
        name="short_conv",
    )(p3d, p3d, p3d, buf, w)


def _group_dot(xb, ones_bd):
    wb = ones_bd.shape[0]
    return jnp.concatenate(
        [jnp.dot(xb[:, i * wb:(i + 1) * wb], ones_bd, preferred_element_type=F32)
         for i in range(W_GROUP // wb)], axis=1)


def _head_sum(x, ones_bd):
    hi = x.astype(BF16)
    lo = (x - hi.astype(F32)).astype(BF16)
    return _group_dot(hi, ones_bd) + _group_dot(lo, ones_bd)


def _rwkvc_body(r_ref, k_ref, v_ref, lo_ref, sh_ref, s0_ref,
                mu_ref, mulo_ref, w0_ref, w2_ref, a0_ref, a2_ref, g2_ref, kk_ref, ka_ref, rk_ref,
                lnw_ref, lnb_ref, ones_ref, *rest, bb, tt, n_t, gsz, chunk):
    (y_ref, so_ref,
     s2_ref, carry_ref, carrylo_ref, pt_s, rt_s, qh_s, kh_s, qb_s, kb_s, v_s, ec_s, ys) = rest[-14:]
    tb = pl.program_id(1)
    n = bb * tt
    w3 = (bb, tt, W_GROUP)
    ones_bd = ones_ref[...]
    n_pair = K_HEADS // 2
    pair_w = 2 * K_HD
    zero_blk = jnp.zeros((K_HD, K_HD), F32)

    @pl.when(tb == 0)
    def _():
        for b in range(bb):
            for p in range(n_pair):
                top = jnp.concatenate([s0_ref[b, 2 * p], zero_blk], axis=1)
                bot = jnp.concatenate([zero_blk, s0_ref[b, 2 * p + 1]], axis=1)
                s2_ref[b, p] = jnp.concatenate([top, bot], axis=0)
        carry_ref[...] = sh_ref[:, :, 0:3 * W_GROUP]
        carrylo_ref[...] = sh_ref[:, :, 3 * W_GROUP:K_COLS]

    tpos = lax.broadcasted_iota(jnp.int32, w3, 1)
    tpos_lo = lax.broadcasted_iota(jnp.int32, (bb, tt, LORA_COLS), 1)

    def shifted(x, carry, mu, mask):
        prev = pltpu.roll(x.reshape(n, x.shape[-1]), 1, axis=0).reshape(x.shape)
        prev = jnp.where(mask == 0, carry, prev)
        return (x + (prev - x) * mu).reshape(n, x.shape[-1])

    r_in, k_in, v_in, lo_in = r_ref[...], k_ref[...], v_ref[...], lo_ref[...]
    r = shifted(r_in, carry_ref[:, :, 0:W_GROUP], mu_ref[:, 0:W_GROUP], tpos)
    k = shifted(k_in, carry_ref[:, :, W_GROUP:2 * W_GROUP], mu_ref[:, W_GROUP:2 * W_GROUP], tpos)
    v = shifted(v_in, carry_ref[:, :, 2 * W_GROUP:3 * W_GROUP], mu_ref[:, 2 * W_GROUP:3 * W_GROUP], tpos)
    lo = shifted(lo_in, carrylo_ref[...], mulo_ref[...], tpos_lo)
    carry_ref[:, :, 0:W_GROUP] = r_ref[:, tt - 1:tt, :]
    carry_ref[:, :, W_GROUP:2 * W_GROUP] = k_ref[:, tt - 1:tt, :]
    carry_ref[:, :, 2 * W_GROUP:3 * W_GROUP] = v_ref[:, tt - 1:tt, :]
    carrylo_ref[...] = lo_ref[:, tt - 1:tt, :]

    zw = w0_ref[...] + jnp.dot(jnp.tanh(lo).astype(BF16), w2_ref[...], preferred_element_type=F32)
    logw = -RWKV_DECAY_SCALE * jax.nn.sigmoid(zw)
    a = jax.nn.sigmoid(a0_ref[...] + jnp.dot(lo.astype(BF16), a2_ref[...], preferred_element_type=F32))
    gate = jnp.dot(jax.nn.sigmoid(lo).astype(BF16), g2_ref[...], preferred_element_type=F32)
    kk = k * kk_ref[...]
    kk = kk * lax.rsqrt(jnp.maximum(_head_sum(kk * kk, ones_bd), 1e-24))
    k2 = k * (1.0 + (a - 1.0) * ka_ref[...])
    bonus = _head_sum(r * k2 * rk_ref[...], ones_bd) * v
    q = -(kk * a)

    cpos = lax.broadcasted_iota(jnp.int32, (n, W_GROUP), 0) % chunk
    g = logw
    step = 1
    while step < chunk:
        g = g + jnp.where(cpos >= step, pltpu.roll(g, step, axis=0), 0.0)
        step *= 2
    g3 = g.reshape(n // chunk, chunk, W_GROUP)
    gtot = jnp.broadcast_to(g3[:, chunk - 1:chunk, :], g3.shape).reshape(n, W_GROUP)
    e_neg = jnp.exp(-g)
    e_rem = jnp.exp(gtot - g)
    pt_s[...] = (kk * jnp.exp(g - logw)).reshape(w3)
    rt_s[...] = (r * jnp.exp(g)).reshape(w3)
    qh_s[...] = (q * e_neg).reshape(w3)
    kh_s[...] = (k2 * e_neg).reshape(w3)
    qb_s[...] = (q * e_rem).reshape(w3)
    kb_s[...] = (k2 * e_rem).reshape(w3)
    v_s[...] = v.reshape(w3)
    ec_s[...] = jnp.exp(gtot).reshape(w3)

    c2, c4 = 2 * chunk, 4 * chunk
    even = lax.broadcasted_iota(jnp.int32, (chunk, pair_w), 1) < K_HD
    ri = lax.broadcasted_iota(jnp.int32, (c4, c4), 0)
    ci = lax.broadcasted_iota(jnp.int32, (c4, c4), 1)
    keep = ci % chunk < ri % chunk + ri // c2
    right = lax.broadcasted_iota(jnp.int32, (c2, c4), 1) >= c2
    eye = (lax.broadcasted_iota(jnp.int32, (c2, c2), 0)
           == lax.broadcasted_iota(jnp.int32, (c2, c2), 1)).astype(F32)
    nt = (((1,), (1,)), ((), ()))
    tn = (((0,), (0,)), ((), ()))
    n_double = chunk.bit_length() - 2

    def two(x):
        return jnp.concatenate([jnp.where(even, x, 0.0), jnp.where(even, 0.0, x)], axis=0)

    def mm(x, y):
        return jnp.dot(x.astype(BF16), y.astype(BF16), preferred_element_type=F32)

    def units(args):
        v2, pr, qk, qkb, s2, ec = zip(*args)
        idx = range(len(args))
        apr = [jnp.where(keep, lax.dot_general(pr[i], qk[i], nt, preferred_element_type=F32), 0.0) for i in idx]
        prs = [lax.dot_general(pr[i], s2[i].astype(BF16), nt, preferred_element_type=F32) for i in idx]
        apk = [mm(jnp.where(right, apr[i][0:c2], 0.0), jnp.concatenate([v2[i], v2[i]], axis=0)) for i in idx]
        power = [apr[i][0:c2, 0:c2] for i in idx]
        inv = [eye + power[i] for i in idx]
        for _ in range(n_double):
            power = [mm(power[i], power[i]) for i in idx]
            inv = [inv[i] + mm(inv[i], power[i]) for i in idx]
        u2 = [mm(inv[i], prs[i][0:c2] + apk[i]) for i in idx]
        uv = [jnp.concatenate([u2[i], v2[i]], axis=0).astype(BF16) for i in idx]
        y2 = [prs[i][c2:c4] + jnp.dot(apr[i][c2:c4].astype(BF16), uv[i], preferred_element_type=F32)
              for i in idx]
        s_new = [s2[i] * ec[i] + lax.dot_general(uv[i], qkb[i], tn, preferred_element_type=F32) for i in idx]
        return [(y2[i][0:chunk] + y2[i][chunk:c2], s_new[i]) for i in idx]

    def chunk_step(i, carry):
        c = i // (bb // gsz)
        b0 = (i % (bb // gsz)) * gsz
        r0 = pl.multiple_of(c * chunk, chunk)
        rows = pl.ds(r0, chunk)
        where = [(b0 + j, slice(p * pair_w, (p + 1) * pair_w), p) for j in range(gsz) for p in range(n_pair)]
        args = []
        for b, lanes, p in where:
            ld = lambda ref: two(ref[b, rows, lanes])
            args.append((ld(v_s),
                         jnp.concatenate([ld(pt_s), ld(rt_s)], axis=0).astype(BF16),
                         jnp.concatenate([ld(qh_s), ld(kh_s)], axis=0).astype(BF16),
                         jnp.concatenate([ld(qb_s), ld(kb_s)], axis=0).astype(BF16),
                         s2_ref[b, p], ec_s[b, pl.ds(r0, 1), lanes]))
        for (b, lanes, p), (y, s_new) in zip(where, units(args)):
            ys[b, rows, lanes] = y
            s2_ref[b, p] = s_new
        return carry

    lax.fori_loop(0, (tt // chunk) * (bb // gsz), chunk_step, 0)

    y = ys[...].reshape(n, W_GROUP)
    mean = _head_sum(y, ones_bd) * (1.0 / K_HD)
    yc = y - mean
    var = _head_sum(yc * yc, ones_bd) * (1.0 / K_HD)
    out = (yc * lax.rsqrt(var + GN_EPS) * lnw_ref[...] + lnb_ref[...] + bonus) * gate
    y_ref[...] = out.reshape(w3).astype(BF16)

    @pl.when(tb == n_t - 1)
    def _():
        for b in range(bb):
            for p in range(n_pair):
                so_ref[b, 2 * p] = s2_ref[b, p, 0:K_HD, 0:K_HD]
                so_ref[b, 2 * p + 1] = s2_ref[b, p, K_HD:pair_w, K_HD:pair_w]


def _rwkvc(p3d, shift, s0_all, layer, prm, stack, *, bb, tt, gsz, chunk):
    bsz, length, _ = p3d.shape
    n_t = length // tt

    def col(j):
        return pl.BlockSpec((bb, tt, W_GROUP), lambda b, t: (b, t, j))

    def whole(a):
        nd = a.ndim
        return pl.BlockSpec(a.shape, lambda b, t: (0,) * nd)

    params = [prm[nm] for nm in ("mu", "mu_lo", "w0", "w2", "a0", "a2", "g2", "k_k", "k_a", "r_k",
                                 "ln_w", "ln_b", "ones_bd")]
    state, state_shape, extra, extra_specs, aliases = _stack_slot(
        stack, s0_all.shape[1:], (bb, K_HEADS, K_HD, K_HD), n_inputs=6 + len(params))
    blk = pltpu.VMEM((bb, tt, W_GROUP), F32)
    return pl.pallas_call(
        functools.partial(_rwkvc_body, bb=bb, tt=tt, n_t=n_t, gsz=gsz, chunk=chunk),
        grid=(bsz // bb, n_t),
        in_specs=[col(COL_K_R), col(COL_K_K), col(COL_K_V),
                  pl.BlockSpec((bb, tt, LORA_COLS), lambda b, t: (b, t, COL_K_LORA)),
                  pl.BlockSpec((bb, 1, K_COLS), lambda b, t: (b, 0, 0)),
                  pl.BlockSpec((None, bb, K_HEADS, K_HD, K_HD), lambda b, t: (layer, b, 0, 0, 0))]
                 + [whole(a) for a in params] + extra_specs,
        out_specs=[pl.BlockSpec((bb, tt, W_GROUP), lambda b, t: (b, t, 0)), state],
        out_shape=[jax.ShapeDtypeStruct((bsz, length, W_GROUP), BF16), state_shape],
        input_output_aliases=aliases,
        scratch_shapes=[pltpu.VMEM((bb, K_HEADS // 2, 2 * K_HD, 2 * K_HD), F32),
                        pltpu.VMEM((bb, 1, 3 * W_GROUP), F32),
                        pltpu.VMEM((bb, 1, LORA_COLS), F32),
                        blk, blk, blk, blk, blk, blk, blk, blk, blk],
        compiler_params=_cparams(("parallel", "arbitrary")),
        name="rwkv7c",
    )(p3d, p3d, p3d, p3d, shift, s0_all, *params, *extra)


def _gate_mixing(w_s, b_s, seq):
    cl = min(A_CHUNK, seq)
    wm = jnp.tril(w_s[:, :cl, :cl])
    bias = b_s[:, :cl]
    rep = A_CHUNK // cl
    if rep > 1:
        eye = jnp.eye(rep, dtype=w_s.dtype)
        wm = jnp.einsum("ab,hts->hatbs", eye, wm).reshape(w_s.shape[0], A_CHUNK, A_CHUNK)
        bias = jnp.tile(bias, (1, rep))
    bias = jnp.repeat(bias.T, A_CHUNK, axis=1)
    return wm.astype(BF16), bias


def _pad_rows(w, row0):
    return jnp.zeros((LORA_COLS, W_GROUP), F32).at[row0:row0 + w.shape[0]].set(w).astype(BF16)


def _layer_params(l, ffn1_norm, ffn1_w_gate, ffn1_w_up, ffn1_w_down, mix_norm, w_in, w_out,
                  a_w_s, a_b_s, a_ln_g, a_ln_b, c_conv_w,
                  k_mu, k_w0, k_w2, k_a0, k_a2, k_g2, k_k_k, k_k_a, k_r_k, k_ln_w, k_ln_b,
                  ffn2_norm, ffn2_w_gate, ffn2_w_up, ffn2_w_down):
    row = lambda a: a[l].reshape(1, -1)
    head_of = np.arange(V7X_MXU_DIM) // K_HD
    ones_bd = jnp.asarray(head_of[:, None] == head_of[None, :], BF16)
    rwkv = dict(
        mu=k_mu[l][None, :3 * W_GROUP], mu_lo=k_mu[l][None, 3 * W_GROUP:],
        w0=row(k_w0), w2=_pad_rows(k_w2[l], 0),
        a0=row(k_a0), a2=_pad_rows(k_a2[l], W_LORA),
        g2=_pad_rows(k_g2[l], W_LORA + A_LORA),
        k_k=row(k_k_k), k_a=row(k_k_a), r_k=row(k_r_k), ln_w=row(k_ln_w), ln_b=row(k_ln_b),
        ones_bd=ones_bd)
    return dict(
        layer=l, ffn_layer=l,
        ffn1=(row(ffn1_norm), ffn1_w_gate, ffn1_w_up, ffn1_w_down),
        ffn2=(row(ffn2_norm), ffn2_w_gate, ffn2_w_up, ffn2_w_down),
        mix_norm=row(mix_norm), w_in=w_in, w_out=w_out,
        a_w_s=a_w_s[l], a_b_s=a_b_s[l], a_ln_g=row(a_ln_g), a_ln_b=row(a_ln_b),
        conv_w=c_conv_w[l], rwkv=rwkv)


def _stream_layer(x2d, bsz, length, pos0, ret_s0_all, ret_layer, conv_buf, rw_shift, rw_s0_all, p, fn, stacks, *,
                  final, cfg):
    layer = p["layer"]
    ffn_kw = dict(layer=p["ffn_layer"], tm=cfg["ffn_tm"], tf=cfg["tf"], cast=cfg["ffn_cast"])
    x1 = _ffn(x2d, *p["ffn1"], fn, final=False, **ffn_kw)
    casted = {}
    if cfg["ffn_cast"]:
        x1, casted["ffn1"] = x1
    proj = _proj(x1, p["mix_norm"], p["w_in"], layer=p["ffn_layer"], tm=cfg["proj_tm"], tn=cfg["tn"],
                 cast=cfg["ffn_cast"])
    if cfg["ffn_cast"]:
        proj, casted["w_in"] = proj
    p3d = proj.reshape(bsz, length, proj.shape[1])

    wm, bias = _gate_mixing(p["a_w_s"], p["a_b_s"], length)
    ya, v_rows = _gate(proj, wm, bias, p["a_ln_g"], p["a_ln_b"], tm=cfg["gate_tm"],
                       with_rows=cfg["with_rows"])
    yb, ret_s = _retention(p3d, ret_s0_all, ret_layer, pos0, stacks["ret"], bb=cfg["ret_bb"])
    yc, conv_new = _conv(p3d, conv_buf, p["conv_w"], bb=cfg["conv_bb"], tt=cfg["conv_tt"])
    yd, rw_s = _rwkvc(p3d, rw_shift[:, None, :], rw_s0_all, ret_layer, p["rwkv"], stacks["rwkv"],
                      bb=cfg["rwkv_bb"], tt=cfg["rwkv_tt"], gsz=cfg["rwkv_gsz"], chunk=cfg["rwkv_chunk"])
    shift_new = p3d[:, length - 1, K_COL0:K_COL0 + K_COLS]

    flat = lambda y: y.reshape(bsz * length, W_GROUP)
    x2 = _outproj(x1, ya, flat(yb), flat(yc), flat(yd), p["w_out"], layer=layer, tm=cfg["tm"])
    x3 = _ffn(x2, *p["ffn2"], fn, final=final, **ffn_kw)
    if cfg["ffn_cast"]:
        x3, casted["ffn2"] = x3
    return x3, ret_s, conv_new, shift_new, rw_s, v_rows, casted


def _stream_cfg(bsz, length, sample):
    m = bsz * length
    tm = min(512, m)
    if sample:
        return dict(tm=tm, ffn_tm=m, ffn_cast=True, proj_tm=m, tf=256, tn=640,
                    gate_tm=min(512, m), with_rows=True,
                    ret_bb=8, conv_bb=32, conv_tt=length, rwkv_bb=16, rwkv_tt=length, rwkv_gsz=4,
                    rwkv_chunk=min(RWKV_CHUNK, length))
    return dict(tm=tm, ffn_tm=tm, ffn_cast=False, proj_tm=min(1024, m), tf=512, tn=1280,
                gate_tm=min(512, m), with_rows=False,
                ret_bb=1, conv_bb=1, conv_tt=min(512, length), rwkv_bb=bsz, rwkv_tt=min(128, length),
                rwkv_gsz=4, rwkv_chunk=min(RWKV_CHUNK, length))


def kernel(x_prompt, x_sample, state_ret, state_conv, state_rwkv_shift, state_rwkv, ffn1_norm, ffn1_w_gate, ffn1_w_up, ffn1_w_down, mix_norm, w_in, w_out, a_w_s, a_b_s, a_ln_g, a_ln_b, c_conv_w, k_mu, k_w0, k_w2, k_a0, k_a2, k_g2, k_k_k, k_k_a, k_r_k, k_ln_w, k_ln_b, ffn2_norm, ffn2_w_gate, ffn2_w_up, ffn2_w_down, final_norm):
    bp, lp, d = x_prompt.shape
    bs, ls, _ = x_sample.shape
    depth = ffn1_norm.shape[0]
    cfg_p = _stream_cfg(bp, lp, sample=False)
    cfg_s = _stream_cfg(bs, ls, sample=True)
    fn = final_norm.reshape(1, d)

    zero_ret = jnp.zeros((1, bp, R_HEADS, R_HD, R_HD), F32)
    zero_conv = jnp.zeros((bp, C_WIDTH - 1, W_GROUP), F32)
    zero_shift = jnp.zeros((bp, K_COLS), F32)
    zero_rw = jnp.zeros((1, bp, K_HEADS, K_HD, K_HD), F32)

    w_out = w_out.astype(BF16)
    xp = x_prompt.reshape(bp * lp, d)
    xs = x_sample.reshape(bs * ls, d)
    outs = [[] for _ in range(5)]
    rp = rs = wp = ws = None
    for l in range(depth):
        p = _layer_params(l, ffn1_norm, ffn1_w_gate, ffn1_w_up, ffn1_w_down, mix_norm, w_in, w_out,
                          a_w_s, a_b_s, a_ln_g, a_ln_b, c_conv_w,
                          k_mu, k_w0, k_w2, k_a0, k_a2, k_g2, k_k_k, k_k_a, k_r_k, k_ln_w, k_ln_b,
                          ffn2_norm, ffn2_w_gate, ffn2_w_up, ffn2_w_down)
        final = l == depth - 1
        xs, rs, cs, ss, ws, vs, w16 = _stream_layer(xs, bs, ls, float(PAST_LEN), state_ret, l, state_conv[l],
                                                    state_rwkv_shift[l], state_rwkv, p, fn,
                                                    dict(ret=(rs, l, depth), rwkv=(ws, l, depth)),
                                                    final=final, cfg=cfg_s)
        p = dict(p, ffn_layer=0, ffn1=p["ffn1"][:1] + w16["ffn1"], ffn2=p["ffn2"][:1] + w16["ffn2"],
                 w_in=w16["w_in"])
        xp, rp, cp, sp, wp, _, _ = _stream_layer(xp, bp, lp, 0.0, zero_ret, 0, zero_conv, zero_shift, zero_rw,
                                                 p, fn, dict(ret=(rp, l, depth), rwkv=(wp, l, depth)),
                                                 final=final, cfg=cfg_p)
        for acc, val in zip(outs, (cp, cs, sp, ss, vs.reshape(bs, ls, W_GROUP))):
            acc.append(val)

    conv_p, conv_s, shift_p, shift_s, v_s = (jnp.stack(o) for o in outs)
    return (xp.reshape(bp, lp, d), xs.reshape(bs, ls, d), rp, rs, conv_p, conv_s, shift_p, shift_s, wp, ws, v_s)
```

```python
import functools

import numpy as np
import jax
import jax.numpy as jnp
from jax import lax
from jax.experimental import pallas as pl
from jax.experimental.pallas import tpu as pltpu

F32 = jnp.float32
BF16 = jnp.bfloat16

W_GROUP = 512
A_CHUNK = 128
R_HEADS = 4
R_HD = 128
R_CHUNK = 128
ROPE_BASE = 10000.0
C_WIDTH = 3
K_HD = 64
K_HEADS = 8
W_LORA = 64
A_LORA = 64
G_LORA = 128
LORA_COLS = W_LORA + A_LORA + G_LORA
K_COLS = 3 * W_GROUP + LORA_COLS
EPS = 1e-6
GN_EPS = 64e-5
PAST_LEN = 16384

COL_A_U, COL_A_V = 0, 1
COL_R_Q, COL_R_K, COL_R_V, COL_R_G = 2, 3, 4, 5
COL_C_B, COL_C_C, COL_C_H = 6, 7, 8
COL_K_R, COL_K_K, COL_K_V = 9, 10, 11
COL_K_LORA = (12 * W_GROUP) // LORA_COLS
K_COL0 = 9 * W_GROUP

V7X_VMEM_LIMIT_BYTES = 62 * 1024 * 1024
RWKV_DECAY_SCALE =float(np.exp(-0.5))
RWKV_CHUNK = 64
V7X_MXU_DIM = 256


def _cparams(sem, vmem=V7X_VMEM_LIMIT_BYTES):
    return pltpu.CompilerParams(dimension_semantics=sem, vmem_limit_bytes=vmem)


def _rms(x, w):
    return x * lax.rsqrt(jnp.mean(x * x, axis=-1, keepdims=True) + EPS) * w


def _ffn_body(x_ref, nw_ref, wg_ref, wu_ref, wd_ref, fn_ref, o_ref, *rest, n_f, final, cast):
    j = pl.program_id(1)
    hn_ref = rest[-1]
    if cast:
        casted = rest[:3]
        for src, dst in zip((wg_ref, wu_ref, wd_ref), casted):
            dst[...] = src[...].astype(BF16)
        wg_ref, wu_ref, wd_ref = casted

    @pl.when(j == 0)
    def _():
        hn_ref[...] = _rms(x_ref[...], nw_ref[...]).astype(BF16)
        o_ref[...] = jnp.zeros_like(o_ref)

    h = hn_ref[...]
    tf = wg_ref.shape[1]
    acc = o_ref[...]
    pending = None
    for c0 in range(0, tf, V7X_MXU_DIM):
        c1 = min(c0 + V7X_MXU_DIM, tf)
        g = jnp.dot(h, wg_ref[:, c0:c1], preferred_element_type=F32)
        u = jnp.dot(h, wu_ref[:, c0:c1], preferred_element_type=F32)
        if pending is not None:
            acc = acc + jnp.dot(pending[0], wd_ref[pending[1]:pending[2], :], preferred_element_type=F32)
        pending = ((g * jax.nn.sigmoid(g) * u).astype(BF16), c0, c1)
    o_ref[...] = acc + jnp.dot(pending[0], wd_ref[pending[1]:pending[2], :], preferred_element_type=F32)

    @pl.when(j == n_f - 1)
    def _():
        y = x_ref[...] + 0.5 * o_ref[...]
        if final:
            y = _rms(y, fn_ref[...])
        o_ref[...] = y


def _ffn(x, nw, wg, wu, wd, fn, *, layer, final, tm, tf, cast=False):
    m, d = x.shape
    f = wg.shape[2]
    n_f = f // tf
    out_specs = [pl.BlockSpec((tm, d), lambda i, j: (i, 0))]
    out_shape = [jax.ShapeDtypeStruct((m, d), F32)]
    if cast:
        assert m == tm, "each weight block must be visited once"
        out_specs += [pl.BlockSpec((None, d, tf), lambda i, j: (0, 0, j)),
                      pl.BlockSpec((None, d, tf), lambda i, j: (0, 0, j)),
                      pl.BlockSpec((None, tf, d), lambda i, j: (0, j, 0))]
        out_shape += [jax.ShapeDtypeStruct((1, d, f), BF16), jax.ShapeDtypeStruct((1, d, f), BF16),
                      jax.ShapeDtypeStruct((1, f, d), BF16)]
    res = pl.pallas_call(
        functools.partial(_ffn_body, n_f=n_f, final=final, cast=cast),
        grid=(m // tm, n_f),
        in_specs=[
            pl.BlockSpec((tm, d), lambda i, j: (i, 0)),
            pl.BlockSpec((1, d), lambda i, j: (0, 0)),
            pl.BlockSpec((None, d, tf), lambda i, j: (layer, 0, j)),
            pl.BlockSpec((None, d, tf), lambda i, j: (layer, 0, j)),
            pl.BlockSpec((None, tf, d), lambda i, j: (layer, j, 0)),
            pl.BlockSpec((1, d), lambda i, j: (0, 0)),
        ],
        out_specs=out_specs,
        out_shape=out_shape,
        scratch_shapes=[pltpu.VMEM((tm, d), BF16)],
        compiler_params=_cparams(("parallel", "arbitrary")),
        name="ffn_final" if final else "ffn",
    )(x, nw, wg, wu, wd, fn)
    return (res[0], tuple(res[1:])) if cast else res[0]


def _proj_body(x_ref, nw_ref, w_ref, o_ref, *rest, cast):
    hn_ref = rest[-1]
    if cast:
        rest[0][...] = w_ref[...].astype(BF16)
        w_ref = rest[0]

    @pl.when(pl.program_id(1) == 0)
    def _():
        hn_ref[...] = _rms(x_ref[...], nw_ref[...]).astype(BF16)

    o_ref[...] = jnp.dot(hn_ref[...], w_ref[...], preferred_element_type=F32)


def _proj(x, nw, w, *, layer, tm, tn, cast=False):
    m, d = x.shape
    n = w.shape[2]
    out_specs = [pl.BlockSpec((tm, tn), lambda i, j: (i, j))]
    out_shape = [jax.ShapeDtypeStruct((m, n), F32)]
    if cast:
        assert m == tm, "each weight block must be visited once"
        out_specs.append(pl.BlockSpec((None, d, tn), lambda i, j: (0, 0, j)))
        out_shape.append(jax.ShapeDtypeStruct((1, d, n), BF16))
    res = pl.pallas_call(
        functools.partial(_proj_body, cast=cast),
        grid=(m // tm, n // tn),
        in_specs=[
            pl.BlockSpec((tm, d), lambda i, j: (i, 0)),
            pl.BlockSpec((1, d), lambda i, j: (0, 0)),
            pl.BlockSpec((None, d, tn), lambda i, j: (layer, 0, j)),
        ],
        out_specs=out_specs,
        out_shape=out_shape,
        scratch_shapes=[pltpu.VMEM((tm, d), BF16)],
        compiler_params=_cparams(("parallel", "arbitrary")),
        name="in_proj",
    )(x, nw, w)
    return tuple(res) if cast else res[0]


def _outproj_body(x_ref, ya_ref, yb_ref, yc_ref, yd_ref, w_ref, o_ref):
    acc = x_ref[...]
    for gi, y_ref in enumerate((ya_ref, yb_ref, yc_ref, yd_ref)):
        acc = acc + jnp.dot(y_ref[...], w_ref[gi * W_GROUP:(gi + 1) * W_GROUP, :],
                            preferred_element_type=F32)
    o_ref[...] = acc


def _outproj(x, ya, yb, yc, yd, w, *, layer, tm):
    m, d = x.shape
    yspec = pl.BlockSpec((tm, W_GROUP), lambda i: (i, 0))
    return pl.pallas_call(
        _outproj_body,
        grid=(m // tm,),
        in_specs=[pl.BlockSpec((tm, d), lambda i: (i, 0)), yspec, yspec, yspec, yspec,
                  pl.BlockSpec((None,) + w.shape[1:], lambda i: (layer, 0, 0))],
        out_specs=pl.BlockSpec((tm, d), lambda i: (i, 0)),
        out_shape=jax.ShapeDtypeStruct((m, d), F32),
        compiler_params=_cparams(("parallel",)),
        name="out_proj",
    )(x, ya, yb, yc, yd, w)


def _gate_body(u_ref, v_ref, wm_ref, bias_ref, g_ref, b_ref, y_ref, *vr_ref, tm):
    gu = jax.nn.gelu(u_ref[...], approximate=True)
    gv = jax.nn.gelu(v_ref[...], approximate=True)
    mu = jnp.mean(gv, axis=-1, keepdims=True)
    var = jnp.mean(jnp.square(gv - mu), axis=-1, keepdims=True)
    vn = (gv - mu) * lax.rsqrt(var + EPS) * g_ref[...] + b_ref[...]
    if vr_ref:
        vr_ref[0][...] = vn
    vnb = vn.astype(BF16)
    for c in range(tm // A_CHUNK):
        rows = slice(c * A_CHUNK, (c + 1) * A_CHUNK)
        for h in range(W_GROUP // A_CHUNK):
            cols = slice(h * A_CHUNK, (h + 1) * A_CHUNK)
            z = jnp.dot(wm_ref[h], vnb[rows, cols], preferred_element_type=F32) + bias_ref[:, cols]
            y_ref[rows, cols] = (gu[rows, cols] * z).astype(BF16)


def _gate(p2d, wm, bias, ln_g, ln_b, *, tm, with_rows):
    m = p2d.shape[0]
    row_spec = pl.BlockSpec((tm, W_GROUP), lambda i: (i, 0))
    out_shape = [jax.ShapeDtypeStruct((m, W_GROUP), BF16)]
    out_specs = [row_spec]
    if with_rows:
        out_shape.append(jax.ShapeDtypeStruct((m, W_GROUP), F32))
        out_specs.append(row_spec)
    res = pl.pallas_call(
        functools.partial(_gate_body, tm=tm),
        grid=(m // tm,),
        in_specs=[
            pl.BlockSpec((tm, W_GROUP), lambda i: (i, COL_A_U)),
            pl.BlockSpec((tm, W_GROUP), lambda i: (i, COL_A_V)),
            pl.BlockSpec(wm.shape, lambda i: (0, 0, 0)),
            pl.BlockSpec(bias.shape, lambda i: (0, 0)),
            pl.BlockSpec((1, W_GROUP), lambda i: (0, 0)),
            pl.BlockSpec((1, W_GROUP), lambda i: (0, 0)),
        ],
        out_specs=out_specs,
        out_shape=out_shape,
        compiler_params=_cparams(("parallel",)),
        name="spatial_gate",
    )(p2d, p2d, wm, bias, ln_g, ln_b)
    return res if with_rows else (res[0], None)


def _ret_body(q_ref, k_ref, v_ref, g_ref, cos_ref, sin_ref, dm_ref, qd_ref, kd_ref, s0_ref,
              *rest, bb, n_c, chunk_decay):
    y_ref, so_ref, s_ref = rest[-3:]
    c = pl.program_id(1)

    @pl.when(c == 0)
    def _():
        s_ref[...] = s0_ref[...]

    cos = cos_ref[...]
    sin = sin_ref[...]
    nt = (((1,), (1,)), ((), ()))
    tn = (((0,), (0,)), ((), ()))
    units = [(b, h, slice(h * R_HD, (h + 1) * R_HD)) for b in range(bb) for h in range(R_HEADS)]
    idx = range(len(units))

    def rope(x):
        return x * cos + pltpu.roll(x, R_HD // 2, axis=1) * sin

    qr = [rope(q_ref[b, :, cols]) for b, h, cols in units]
    kr = [rope(k_ref[b, :, cols]) * (R_HD ** -0.5) for b, h, cols in units]
    v = [v_ref[b, :, cols].astype(BF16) for b, h, cols in units]
    s = [s_ref[b, h] for b, h, cols in units]
    sc = [lax.dot_general(qr[i].astype(BF16), kr[i].astype(BF16), nt, preferred_element_type=F32)
          * dm_ref[units[i][1]] for i in idx]
    cross = [jnp.dot((qr[i] * qd_ref[:, units[i][2]]).astype(BF16), s[i].astype(BF16),
                     preferred_element_type=F32) for i in idx]
    kv = [lax.dot_general((kr[i] * kd_ref[:, units[i][2]]).astype(BF16), v[i], tn,
                          preferred_element_type=F32) for i in idx]
    o = [cross[i] + jnp.dot(sc[i].astype(BF16), v[i], preferred_element_type=F32) for i in idx]
    for i, (b, h, cols) in enumerate(units):
        s_ref[b, h] = chunk_decay[h] * s[i] + kv[i]
        on = o[i] * lax.rsqrt(jnp.mean(o[i] * o[i], axis=-1, keepdims=True) + EPS)
        g = g_ref[b, :, cols]
        y_ref[b, :, cols] = (on * (g * jax.nn.sigmoid(g))).astype(BF16)

    @pl.when(c == n_c - 1)
    def _():
        so_ref[...] = s_ref[...]


def _ret_tables(cl, pos0, length):
    half = R_HD // 2
    inv = ROPE_BASE ** (-jnp.arange(half, dtype=F32) / half)
    pos = pos0 + jnp.arange(length, dtype=F32)
    ang = pos[:, None] * inv[None, :]
    cos = jnp.cos(ang)
    sin = jnp.sin(ang)
    cos_t = jnp.concatenate([cos, cos], axis=-1)
    sin_t = jnp.concatenate([-sin, sin], axis=-1)
    log_gamma = np.log(1.0 - 2.0 ** (-5.0 - np.arange(R_HEADS, dtype=np.float64)))
    idx = np.arange(cl, dtype=np.float64)
    diff = idx[:, None] - idx[None, :]
    dmat = np.where(diff >= 0, np.exp(np.maximum(diff, 0.0)[None] * log_gamma[:, None, None]), 0.0)
    kdec = np.exp((cl - 1.0 - idx)[:, None] * log_gamma[None, :])
    qdec = np.exp((idx + 1.0)[:, None] * log_gamma[None, :])
    chunk_decay = tuple(float(x) for x in np.exp(cl * log_gamma))
    rep = lambda a: jnp.asarray(np.repeat(a, R_HD, axis=1), F32)
    return cos_t, sin_t, jnp.asarray(dmat, F32), rep(qdec), rep(kdec), chunk_decay


def _stack_slot(stack, per_layer_shape, block, n_inputs):
    prev, slot, depth = stack
    nd = len(per_layer_shape)
    spec = pl.BlockSpec((None,) + block, lambda b, t: (slot, b) + (0,) * (nd - 1))
    shape = jax.ShapeDtypeStruct((depth,) + tuple(per_layer_shape), F32)
    if prev is None:
        return spec, shape, [], [], {}
    return spec, shape, [prev], [pl.BlockSpec(memory_space=pl.ANY)], {n_inputs: 1}


def _retention(p3d, s0_all, layer, pos0, stack, *, bb):
    bsz, length, _ = p3d.shape
    cl = min(R_CHUNK, length)
    n_c = length // cl
    cos_t, sin_t, dmat, qdec, kdec, chunk_decay = _ret_tables(cl, pos0, length)

    def col(j):
        return pl.BlockSpec((bb, cl, W_GROUP), lambda b, c: (b, c, j))

    tab = pl.BlockSpec((cl, R_HD), lambda b, c: (c, 0))
    state, state_shape, extra, extra_specs, aliases = _stack_slot(
        stack, s0_all.shape[1:], (bb, R_HEADS, R_HD, R_HD), n_inputs=10)
    return pl.pallas_call(
        functools.partial(_ret_body, bb=bb, n_c=n_c, chunk_decay=chunk_decay),
        grid=(bsz // bb, n_c),
        in_specs=[col(COL_R_Q), col(COL_R_K), col(COL_R_V), col(COL_R_G), tab, tab,
                  pl.BlockSpec(dmat.shape, lambda b, c: (0, 0, 0)),
                  pl.BlockSpec(qdec.shape, lambda b, c: (0, 0)),
                  pl.BlockSpec(kdec.shape, lambda b, c: (0, 0)),
                  pl.BlockSpec((None, bb, R_HEADS, R_HD, R_HD), lambda b, c: (layer, b, 0, 0, 0))]
                 + extra_specs,
        out_specs=[pl.BlockSpec((bb, cl, W_GROUP), lambda b, c: (b, c, 0)), state],
        out_shape=[jax.ShapeDtypeStruct((bsz, length, W_GROUP), BF16), state_shape],
        input_output_aliases=aliases,
        scratch_shapes=[pltpu.VMEM((bb, R_HEADS, R_HD, R_HD), F32)],
        compiler_params=_cparams(("parallel", "arbitrary")),
        name="retention",
    )(p3d, p3d, p3d, p3d, cos_t, sin_t, dmat, qdec, kdec, s0_all, *extra)


def _conv_body(bg_ref, cg_ref, h_ref, buf_ref, w_ref, y_ref, st_ref, carry_ref, *, bb, tt):
    @pl.when(pl.program_id(1) == 0)
    def _():
        carry_ref[...] = buf_ref[...]

    shape = (bb, tt, W_GROUP)
    z = cg_ref[...] * h_ref[...]
    z2 = z.reshape(bb * tt, W_GROUP)
    r1 = pltpu.roll(z2, 1, axis=0).reshape(shape)
    r2 = pltpu.roll(z2, 2, axis=0).reshape(shape)
    tpos = lax.broadcasted_iota(jnp.int32, shape, 1)
    c0 = carry_ref[:, 0:1, :]
    c1 = carry_ref[:, 1:2, :]
    zm1 = jnp.where(tpos == 0, c1, r1)
    zm2 = jnp.where(tpos == 0, c0, jnp.where(tpos == 1, c1, r2))
    y = w_ref[0:1, :] * zm2 + w_ref[1:2, :] * zm1 + w_ref[2:3, :] * z
    y_ref[...] = (bg_ref[...] * y).astype(BF16)
    new = cg_ref[:, tt - 2:tt, :] * h_ref[:, tt - 2:tt, :]
    carry_ref[...] = new
    st_ref[...] = new


def _conv(p3d, buf, w, *, bb, tt):
    bsz, length, _ = p3d.shape

    def col(j):
        return pl.BlockSpec((bb, tt, W_GROUP), lambda b, t: (b, t, j))

    state = pl.BlockSpec((bb, C_WIDTH - 1, W_GROUP), lambda b, t: (b, 0, 0))
    return pl.pallas_call(
        functools.partial(_conv_body, bb=bb, tt=tt),
        grid=(bsz // bb, length // tt),
        in_specs=[col(COL_C_B), col(COL_C_C), col(COL_C_H), state,
                  pl.BlockSpec(w.shape, lambda b, t: (0, 0))],
        out_specs=[pl.BlockSpec((bb, tt, W_GROUP), lambda b, t: (b, t, 0)), state],
        out_shape=[jax.ShapeDtypeStruct((bsz, length, W_GROUP), BF16),
                   jax.ShapeDtypeStruct(buf.shape, F32)],
        scratch_shapes=[pltpu.VMEM((bb, C_WIDTH - 1, W_GROUP), F32)],
        compiler_params=_cparams(("parallel", "arbitrary")),
        name="short_conv",
    )(p3d, p3d, p3d, buf, w)


def _group_dot(xb, ones_bd):
    wb = ones_bd.shape[0]
    return jnp.concatenate(
        [jnp.dot(xb[:, i * wb:(i + 1) * wb], ones_bd, preferred_element_type=F32)
         for i in range(W_GROUP // wb)], axis=1)


def _head_sum(x, ones_bd):
    return _group_dot(x.astype(BF16), ones_bd)


def _rwkvc_body(r_ref, k_ref, v_ref, lo_ref, sh_ref, s0_ref,
                mu_ref, mulo_ref, w0_ref, w2_ref, a0_ref, a2_ref, g2_ref, kk_ref, ka_ref, rk_ref,
                lnw_ref, lnb_ref, ones_ref, *rest, bb, tt, n_t, gsz, chunk):
    (y_ref, so_ref,
     s2_ref, carry_ref, carrylo_ref, pt_s, rt_s, qh_s, kh_s, qb_s, kb_s, v_s, ec_s, ys) = rest[-14:]
    tb = pl.program_id(1)
    n = bb * tt
    w3 = (bb, tt, W_GROUP)
    ones_bd = ones_ref[...]
    n_pair = K_HEADS // 2
    pair_w = 2 * K_HD
    zero_blk = jnp.zeros((K_HD, K_HD), F32)

    @pl.when(tb == 0)
    def _():
        for b in range(bb):
            for p in range(n_pair):
                top = jnp.concatenate([s0_ref[b, 2 * p], zero_blk], axis=1)
                bot = jnp.concatenate([zero_blk, s0_ref[b, 2 * p + 1]], axis=1)
                s2_ref[b, p] = jnp.concatenate([top, bot], axis=0)
        carry_ref[...] = sh_ref[:, :, 0:3 * W_GROUP]
        carrylo_ref[...] = sh_ref[:, :, 3 * W_GROUP:K_COLS]

    tpos = lax.broadcasted_iota(jnp.int32, w3, 1)
    tpos_lo = lax.broadcasted_iota(jnp.int32, (bb, tt, LORA_COLS), 1)

    def shifted(x, carry, mu, mask):
        prev = pltpu.roll(x.reshape(n, x.shape[-1]), 1, axis=0).reshape(x.shape)
        prev = jnp.where(mask == 0, carry, prev)
        return (x + (prev - x) * mu).reshape(n, x.shape[-1])

    r_in, k_in, v_in, lo_in = r_ref[...], k_ref[...], v_ref[...], lo_ref[...]
    r = shifted(r_in, carry_ref[:, :, 0:W_GROUP], mu_ref[:, 0:W_GROUP], tpos)
    k = shifted(k_in, carry_ref[:, :, W_GROUP:2 * W_GROUP], mu_ref[:, W_GROUP:2 * W_GROUP], tpos)
    v = shifted(v_in, carry_ref[:, :, 2 * W_GROUP:3 * W_GROUP], mu_ref[:, 2 * W_GROUP:3 * W_GROUP], tpos)
    lo = shifted(lo_in, carrylo_ref[...], mulo_ref[...], tpos_lo)
    carry_ref[:, :, 0:W_GROUP] = r_ref[:, tt - 1:tt, :]
    carry_ref[:, :, W_GROUP:2 * W_GROUP] = k_ref[:, tt - 1:tt, :]
    carry_ref[:, :, 2 * W_GROUP:3 * W_GROUP] = v_ref[:, tt - 1:tt, :]
    carrylo_ref[...] = lo_ref[:, tt - 1:tt, :]

    zw = w0_ref[...] + jnp.dot(jnp.tanh(lo).astype(BF16), w2_ref[...], preferred_element_type=F32)
    logw = -RWKV_DECAY_SCALE * jax.nn.sigmoid(zw)
    a = jax.nn.sigmoid(a0_ref[...] + jnp.dot(lo.astype(BF16), a2_ref[...], preferred_element_type=F32))
    gate = jnp.dot(jax.nn.sigmoid(lo).astype(BF16), g2_ref[...], preferred_element_type=F32)
    kk = k * kk_ref[...]
    kk = kk * lax.rsqrt(jnp.maximum(_head_sum(kk * kk, ones_bd), 1e-24))
    k2 = k * (1.0 + (a - 1.0) * ka_ref[...])
    bonus = _head_sum(r * k2 * rk_ref[...], ones_bd) * v
    q = -(kk * a)

    cpos = lax.broadcasted_iota(jnp.int32, (n, W_GROUP), 0) % chunk
    g = logw
    step = 1
    while step < chunk:
        g = g + jnp.where(cpos >= step, pltpu.roll(g, step, axis=0), 0.0)
        step *= 2
    g3 = g.reshape(n // chunk, chunk, W_GROUP)
    gtot = jnp.broadcast_to(g3[:, chunk - 1:chunk, :], g3.shape).reshape(n, W_GROUP)
    e_neg = jnp.exp(-g)
    e_rem = jnp.exp(gtot - g)
    pt_s[...] = (kk * jnp.exp(g - logw)).reshape(w3)
    rt_s[...] = (r * jnp.exp(g)).reshape(w3)
    qh_s[...] = (q * e_neg).reshape(w3)
    kh_s[...] = (k2 * e_neg).reshape(w3)
    qb_s[...] = (q * e_rem).reshape(w3)
    kb_s[...] = (k2 * e_rem).reshape(w3)
    v_s[...] = v.reshape(w3)
    ec_s[...] = jnp.exp(gtot).reshape(w3)

    c2, c4 = 2 * chunk, 4 * chunk
    even = lax.broadcasted_iota(jnp.int32, (chunk, pair_w), 1) < K_HD
    ri = lax.broadcasted_iota(jnp.int32, (c4, c4), 0)
    ci = lax.broadcasted_iota(jnp.int32, (c4, c4), 1)
    keep = ci % chunk < ri % chunk + ri // c2
    right = lax.broadcasted_iota(jnp.int32, (c2, c4), 1) >= c2
    eye = (lax.broadcasted_iota(jnp.int32, (c2, c2), 0)
           == lax.broadcasted_iota(jnp.int32, (c2, c2), 1)).astype(F32)
    nt = (((1,), (1,)), ((), ()))
    tn = (((0,), (0,)), ((), ()))
    n_double = chunk.bit_length() - 2

    def two(x):
        return jnp.concatenate([jnp.where(even, x, 0.0), jnp.where(even, 0.0, x)], axis=0)

    def mm(x, y):
        return jnp.dot(x.astype(BF16), y.astype(BF16), preferred_element_type=F32)

    def units(args):
        v2, pr, qk, qkb, s2, ec = zip(*args)
        idx = range(len(args))
        apr = [jnp.where(keep, lax.dot_general(pr[i], qk[i], nt, preferred_element_type=F32), 0.0) for i in idx]
        prs = [lax.dot_general(pr[i], s2[i].astype(BF16), nt, preferred_element_type=F32) for i in idx]
        apk = [mm(jnp.where(right, apr[i][0:c2], 0.0), jnp.concatenate([v2[i], v2[i]], axis=0)) for i in idx]
        power = [apr[i][0:c2, 0:c2] for i in idx]
        inv = [eye + power[i] for i in idx]
        for _ in range(n_double):
            power = [mm(power[i], power[i]) for i in idx]
            inv = [inv[i] + mm(inv[i], power[i]) for i in idx]
        u2 = [mm(inv[i], prs[i][0:c2] + apk[i]) for i in idx]
        uv = [jnp.concatenate([u2[i], v2[i]], axis=0).astype(BF16) for i in idx]
        y2 = [prs[i][c2:c4] + jnp.dot(apr[i][c2:c4].astype(BF16), uv[i], preferred_element_type=F32)
              for i in idx]
        s_new = [s2[i] * ec[i] + lax.dot_general(uv[i], qkb[i], tn, preferred_element_type=F32) for i in idx]
        return [(y2[i][0:chunk] + y2[i][chunk:c2], s_new[i]) for i in idx]

    def chunk_step(i, carry):
        c = i // (bb // gsz)
        b0 = (i % (bb // gsz)) * gsz
        r0 = pl.multiple_of(c * chunk, chunk)
        rows = pl.ds(r0, chunk)
        where = [(b0 + j, slice(p * pair_w, (p + 1) * pair_w), p) for j in range(gsz) for p in range(n_pair)]
        args = []
        for b, lanes, p in where:
            ld = lambda ref: two(ref[b, rows, lanes])
            args.append((ld(v_s),
                         jnp.concatenate([ld(pt_s), ld(rt_s)], axis=0).astype(BF16),
                         jnp.concatenate([ld(qh_s), ld(kh_s)], axis=0).astype(BF16),
                         jnp.concatenate([ld(qb_s), ld(kb_s)], axis=0).astype(BF16),
                         s2_ref[b, p], ec_s[b, pl.ds(r0, 1), lanes]))
        for (b, lanes, p), (y, s_new) in zip(where, units(args)):
            ys[b, rows, lanes] = y
            s2_ref[b, p] = s_new
        return carry

    lax.fori_loop(0, (tt // chunk) * (bb // gsz), chunk_step, 0)

    y = ys[...].reshape(n, W_GROUP)
    mean = _head_sum(y, ones_bd) * (1.0 / K_HD)
    yc = y - mean
    var = _head_sum(yc * yc, ones_bd) * (1.0 / K_HD)
    out = (yc * lax.rsqrt(var + GN_EPS) * lnw_ref[...] + lnb_ref[...] + bonus) * gate
    y_ref[...] = out.reshape(w3).astype(BF16)

    @pl.when(tb == n_t - 1)
    def _():
        for b in range(bb):
            for p in range(n_pair):
                so_ref[b, 2 * p] = s2_ref[b, p, 0:K_HD, 0:K_HD]
                so_ref[b, 2 * p + 1] = s2_ref[b, p, K_HD:pair_w, K_HD:pair_w]


def _rwkvc(p3d, shift, s0_all, layer, prm, stack, *, bb, tt, gsz, chunk):
    bsz, length, _ = p3d.shape
    n_t = length // tt

    def col(j):
        return pl.BlockSpec((bb, tt, W_GROUP), lambda b, t: (b, t, j))

    def whole(a):
        nd = a.ndim
        return pl.BlockSpec(a.shape, lambda b, t: (0,) * nd)

    params = [prm[nm] for nm in ("mu", "mu_lo", "w0", "w2", "a0", "a2", "g2", "k_k", "k_a", "r_k",
                                 "ln_w", "ln_b", "ones_bd")]
    state, state_shape, extra, extra_specs, aliases = _stack_slot(
        stack, s0_all.shape[1:], (bb, K_HEADS, K_HD, K_HD), n_inputs=6 + len(params))
    blk = pltpu.VMEM((bb, tt, W_GROUP), F32)
    return pl.pallas_call(
        functools.partial(_rwkvc_body, bb=bb, tt=tt, n_t=n_t, gsz=gsz, chunk=chunk),
        grid=(bsz // bb, n_t),
        in_specs=[col(COL_K_R), col(COL_K_K), col(COL_K_V),
                  pl.BlockSpec((bb, tt, LORA_COLS), lambda b, t: (b, t, COL_K_LORA)),
                  pl.BlockSpec((bb, 1, K_COLS), lambda b, t: (b, 0, 0)),
                  pl.BlockSpec((None, bb, K_HEADS, K_HD, K_HD), lambda b, t: (layer, b, 0, 0, 0))]
                 + [whole(a) for a in params] + extra_specs,
        out_specs=[pl.BlockSpec((bb, tt, W_GROUP), lambda b, t: (b, t, 0)), state],
        out_shape=[jax.ShapeDtypeStruct((bsz, length, W_GROUP), BF16), state_shape],
        input_output_aliases=aliases,
        scratch_shapes=[pltpu.VMEM((bb, K_HEADS // 2, 2 * K_HD, 2 * K_HD), F32),
                        pltpu.VMEM((bb, 1, 3 * W_GROUP), F32),
                        pltpu.VMEM((bb, 1, LORA_COLS), F32),
                        blk, blk, blk, blk, blk, blk, blk, blk, blk],
        compiler_params=_cparams(("parallel", "arbitrary")),
        name="rwkv7c",
    )(p3d, p3d, p3d, p3d, shift, s0_all, *params, *extra)


def _gate_mixing(w_s, b_s, seq):
    cl = min(A_CHUNK, seq)
    wm = jnp.tril(w_s[:, :cl, :cl])
    bias = b_s[:, :cl]
    rep = A_CHUNK // cl
    if rep > 1:
        eye = jnp.eye(rep, dtype=w_s.dtype)
        wm = jnp.einsum("ab,hts->hatbs", eye, wm).reshape(w_s.shape[0], A_CHUNK, A_CHUNK)
        bias = jnp.tile(bias, (1, rep))
    bias = jnp.repeat(bias.T, A_CHUNK, axis=1)
    return wm.astype(BF16), bias


def _pad_rows(w, row0):
    return jnp.zeros((LORA_COLS, W_GROUP), F32).at[row0:row0 + w.shape[0]].set(w).astype(BF16)


def _layer_params(l, ffn1_norm, ffn1_w_gate, ffn1_w_up, ffn1_w_down, mix_norm, w_in, w_out,
                  a_w_s, a_b_s, a_ln_g, a_ln_b, c_conv_w,
                  k_mu, k_w0, k_w2, k_a0, k_a2, k_g2, k_k_k, k_k_a, k_r_k, k_ln_w, k_ln_b,
                  ffn2_norm, ffn2_w_gate, ffn2_w_up, ffn2_w_down):
    row = lambda a: a[l].reshape(1, -1)
    head_of = np.arange(V7X_MXU_DIM) // K_HD
    ones_bd = jnp.asarray(head_of[:, None] == head_of[None, :], BF16)
    rwkv = dict(
        mu=k_mu[l][None, :3 * W_GROUP], mu_lo=k_mu[l][None, 3 * W_GROUP:],
        w0=row(k_w0), w2=_pad_rows(k_w2[l], 0),
        a0=row(k_a0), a2=_pad_rows(k_a2[l], W_LORA),
        g2=_pad_rows(k_g2[l], W_LORA + A_LORA),
        k_k=row(k_k_k), k_a=row(k_k_a), r_k=row(k_r_k), ln_w=row(k_ln_w), ln_b=row(k_ln_b),
        ones_bd=ones_bd)
    return dict(
        layer=l, ffn_layer=l,
        ffn1=(row(ffn1_norm), ffn1_w_gate, ffn1_w_up, ffn1_w_down),
        ffn2=(row(ffn2_norm), ffn2_w_gate, ffn2_w_up, ffn2_w_down),
        mix_norm=row(mix_norm), w_in=w_in, w_out=w_out,
        a_w_s=a_w_s[l], a_b_s=a_b_s[l], a_ln_g=row(a_ln_g), a_ln_b=row(a_ln_b),
        conv_w=c_conv_w[l], rwkv=rwkv)


def _stream_layer(x2d, bsz, length, pos0, ret_s0_all, ret_layer, conv_buf, rw_shift, rw_s0_all, p, fn, stacks, *,
                  final, cfg):
    layer = p["layer"]
    ffn_kw = dict(layer=p["ffn_layer"], tm=cfg["ffn_tm"], tf=cfg["tf"], cast=cfg["ffn_cast"])
    x1 = _ffn(x2d, *p["ffn1"], fn, final=False, **ffn_kw)
    casted = {}
    if cfg["ffn_cast"]:
        x1, casted["ffn1"] = x1
    proj = _proj(x1, p["mix_norm"], p["w_in"], layer=p["ffn_layer"], tm=cfg["proj_tm"], tn=cfg["tn"],
                 cast=cfg["ffn_cast"])
    if cfg["ffn_cast"]:
        proj, casted["w_in"] = proj
    p3d = proj.reshape(bsz, length, proj.shape[1])

    wm, bias = _gate_mixing(p["a_w_s"], p["a_b_s"], length)
    ya, v_rows = _gate(proj, wm, bias, p["a_ln_g"], p["a_ln_b"], tm=cfg["gate_tm"],
                       with_rows=cfg["with_rows"])
    yb, ret_s = _retention(p3d, ret_s0_all, ret_layer, pos0, stacks["ret"], bb=cfg["ret_bb"])
    yc, conv_new = _conv(p3d, conv_buf, p["conv_w"], bb=cfg["conv_bb"], tt=cfg["conv_tt"])
    yd, rw_s = _rwkvc(p3d, rw_shift[:, None, :], rw_s0_all, ret_layer, p["rwkv"], stacks["rwkv"],
                      bb=cfg["rwkv_bb"], tt=cfg["rwkv_tt"], gsz=cfg["rwkv_gsz"], chunk=cfg["rwkv_chunk"])
    shift_new = p3d[:, length - 1, K_COL0:K_COL0 + K_COLS]

    flat = lambda y: y.reshape(bsz * length, W_GROUP)
    x2 = _outproj(x1, ya, flat(yb), flat(yc), flat(yd), p["w_out"], layer=layer, tm=cfg["tm"])
    x3 = _ffn(x2, *p["ffn2"], fn, final=final, **ffn_kw)
    if cfg["ffn_cast"]:
        x3, casted["ffn2"] = x3
    return x3, ret_s, conv_new, shift_new, rw_s, v_rows, casted


def _stream_cfg(bsz, length, sample):
    m = bsz * length
    tm = min(512, m)
    if sample:
        return dict(tm=tm, ffn_tm=m, ffn_cast=True, proj_tm=m, tf=256, tn=640,
                    gate_tm=min(512, m), with_rows=True,
                    ret_bb=8, conv_bb=32, conv_tt=length, rwkv_bb=16, rwkv_tt=length, rwkv_gsz=8,
                    rwkv_chunk=min(RWKV_CHUNK, length))
    return dict(tm=tm, ffn_tm=tm, ffn_cast=False, proj_tm=min(1024, m), tf=512, tn=1280,
                gate_tm=min(512, m), with_rows=False,
                ret_bb=1, conv_bb=1, conv_tt=min(512, length), rwkv_bb=bsz, rwkv_tt=min(128, length),
                rwkv_gsz=4, rwkv_chunk=min(RWKV_CHUNK, length))


def kernel(x_prompt, x_sample, state_ret, state_conv, state_rwkv_shift, state_rwkv, ffn1_norm, ffn1_w_gate, ffn1_w_up, ffn1_w_down, mix_norm, w_in, w_out, a_w_s, a_b_s, a_ln_g, a_ln_b, c_conv_w, k_mu, k_w0, k_w2, k_a0, k_a2, k_g2, k_k_k, k_k_a, k_r_k, k_ln_w, k_ln_b, ffn2_norm, ffn2_w_gate, ffn2_w_up, ffn2_w_down, final_norm):
    bp, lp, d = x_prompt.shape
    bs, ls, _ = x_sample.shape
    depth = ffn1_norm.shape[0]
    cfg_p = _stream_cfg(bp, lp, sample=False)
    cfg_s = _stream_cfg(bs, ls, sample=True)
    fn = final_norm.reshape(1, d)

    zero_ret = jnp.zeros((1, bp, R_HEADS, R_HD, R_HD), F32)
    zero_conv = jnp.zeros((bp, C_WIDTH - 1, W_GROUP), F32)
    zero_shift = jnp.zeros((bp, K_COLS), F32)
    zero_rw = jnp.zeros((1, bp, K_HEADS, K_HD, K_HD), F32)

    w_out = w_out.astype(BF16)
    xp = x_prompt.reshape(bp * lp, d)
    xs = x_sample.reshape(bs * ls, d)
    outs = [[] for _ in range(5)]
    rp = rs = wp = ws = None
    for l in range(depth):
        p = _layer_params(l, ffn1_norm, ffn1_w_gate, ffn1_w_up, ffn1_w_down, mix_norm, w_in, w_out,
                          a_w_s, a_b_s, a_ln_g, a_ln_b, c_conv_w,
                          k_mu, k_w0, k_w2, k_a0, k_a2, k_g2, k_k_k, k_k_a, k_r_k, k_ln_w, k_ln_b,
                          ffn2_norm, ffn2_w_gate, ffn2_w_up, ffn2_w_down)
        final = l == depth - 1
        xs, rs, cs, ss, ws, vs, w16 = _stream_layer(xs, bs, ls, float(PAST_LEN), state_ret, l, state_conv[l],
                                                    state_rwkv_shift[l], state_rwkv, p, fn,
                                                    dict(ret=(rs, l, depth), rwkv=(ws, l, depth)),
                                                    final=final, cfg=cfg_s)
        p = dict(p, ffn_layer=0, ffn1=p["ffn1"][:1] + w16["ffn1"], ffn2=p["ffn2"][:1] + w16["ffn2"],
                 w_in=w16["w_in"])
        xp, rp, cp, sp, wp, _, _ = _stream_layer(xp, bp, lp, 0.0, zero_ret, 0, zero_conv, zero_shift, zero_rw,
                                                 p, fn, dict(ret=(rp, l, depth), rwkv=(wp, l, depth)),
                                                 final=final, cfg=cfg_p)
        for acc, val in zip(outs, (cp, cs, sp, ss, vs.reshape(bs, ls, W_GROUP))):
            acc.append(val)

    conv_p, conv_s, shift_p, shift_s, v_s = (jnp.stack(o) for o in outs)
    return (xp.reshape(bp, lp, d), xs.reshape(bs, ls, d), rp, rs, conv_p, conv_s, shift_p, shift_s, wp, ws, v_s)
```

```python
import functools

import numpy as np
import jax
import jax.numpy as jnp
from jax import lax
from jax.experimental import pallas as pl
from jax.experimental.pallas import tpu as pltpu

F32 = jnp.float32
BF16 = jnp.bfloat16

W_GROUP = 512
A_CHUNK = 128
R_HEADS = 4
R_HD = 128
R_CHUNK = 128
ROPE_BASE = 10000.0
C_WIDTH = 3
K_HD = 64
K_HEADS = 8
W_LORA = 64
A_LORA = 64
G_LORA = 128
LORA_COLS = W_LORA + A_LORA + G_LORA
K_COLS = 3 * W_GROUP + LORA_COLS
EPS = 1e-6
GN_EPS = 64e-5
PAST_LEN = 16384

COL_A_U, COL_A_V = 0, 1
COL_R_Q, COL_R_K, COL_R_V, COL_R_G = 2, 3, 4, 5
COL_C_B, COL_C_C, COL_C_H = 6, 7, 8
COL_K_R, COL_K_K, COL_K_V = 9, 10, 11
COL_K_LORA = (12 * W_GROUP) // LORA_COLS
K_COL0 = 9 * W_GROUP

V7X_VMEM_LIMIT_BYTES = 62 * 1024 * 1024
RWKV_DECAY_SCALE =float(np.exp(-0.5))
RWKV_CHUNK = 64
V7X_MXU_DIM = 256


def _cparams(sem, vmem=V7X_VMEM_LIMIT_BYTES):
    return pltpu.CompilerParams(dimension_semantics=sem, vmem_limit_bytes=vmem)


def _rms(x, w):
    return x * lax.rsqrt(jnp.mean(x * x, axis=-1, keepdims=True) + EPS) * w


def _ffn_body(x_ref, nw_ref, wg_ref, wu_ref, wd_ref, fn_ref, o_ref, *rest, n_i, n_f, tm, final, cast, x_in_hbm):
    i, j = pl.program_id(0), pl.program_id(1)
    if cast:
        casted, hn_ref = rest[:3], rest[3]
        for src, dst in zip((wg_ref, wu_ref, wd_ref), casted):
            dst[...] = src[...].astype(BF16)
        wg_ref, wu_ref, wd_ref = casted
    else:
        hn_ref = rest[0]

    def start_tile(x):
        hn_ref[...] = _rms(x, nw_ref[...]).astype(BF16)
        o_ref[...] = 2.0 * x

    if x_in_hbm:
        xbuf, sem = rest[-2:]

        def x_copy(tile):
            return pltpu.make_async_copy(x_ref.at[pl.ds(tile * tm, tm), :], xbuf, sem)

        @pl.when((i == 0) & (j == 0))
        def _():
            x_copy(0).start()

        @pl.when(j == 0)
        def _():
            x_copy(i).wait()
            start_tile(xbuf[...])

        @pl.when((j == 1) & (i + 1 < n_i))
        def _():
            x_copy(i + 1).start()
    else:
        @pl.when(j == 0)
        def _():
            start_tile(x_ref[...])

    h = hn_ref[...]
    tf = wg_ref.shape[1]
    acc = o_ref[...]
    pending = None
    for c0 in range(0, tf, V7X_MXU_DIM):
        c1 = min(c0 + V7X_MXU_DIM, tf)
        g = jnp.dot(h, wg_ref[:, c0:c1], preferred_element_type=F32)
        u = jnp.dot(h, wu_ref[:, c0:c1], preferred_element_type=F32)
        if pending is not None:
            acc = acc + jnp.dot(pending[0], wd_ref[pending[1]:pending[2], :], preferred_element_type=F32)
        pending = ((g * jax.nn.sigmoid(g) * u).astype(BF16), c0, c1)
    o_ref[...] = acc + jnp.dot(pending[0], wd_ref[pending[1]:pending[2], :], preferred_element_type=F32)

    @pl.when(j == n_f - 1)
    def _():
        y = 0.5 * o_ref[...]
        if final:
            y = _rms(y, fn_ref[...])
        o_ref[...] = y


def _ffn(x, nw, wg, wu, wd, fn, *, layer, final, tm, tf, cast=False):
    m, d = x.shape
    f = wg.shape[2]
    n_i, n_f = m // tm, f // tf
    x_in_hbm = n_i > 1
    assert not (cast and x_in_hbm) and n_f >= 2
    x_spec = pl.BlockSpec(memory_space=pl.ANY) if x_in_hbm else pl.BlockSpec((tm, d), lambda i, j: (i, 0))
    scratch = [pltpu.VMEM((tm, d), BF16)]
    if x_in_hbm:
        scratch += [pltpu.VMEM((tm, d), F32), pltpu.SemaphoreType.DMA(())]
    out_specs = [pl.BlockSpec((tm, d), lambda i, j: (i, 0))]
    out_shape = [jax.ShapeDtypeStruct((m, d), F32)]
    if cast:
        assert m == tm, "each weight block must be visited once"
        out_specs += [pl.BlockSpec((None, d, tf), lambda i, j: (0, 0, j)),
                      pl.BlockSpec((None, d, tf), lambda i, j: (0, 0, j)),
                      pl.BlockSpec((None, tf, d), lambda i, j: (0, j, 0))]
        out_shape += [jax.ShapeDtypeStruct((1, d, f), BF16), jax.ShapeDtypeStruct((1, d, f), BF16),
                      jax.ShapeDtypeStruct((1, f, d), BF16)]
    res = pl.pallas_call(
        functools.partial(_ffn_body, n_i=n_i, n_f=n_f, tm=tm, final=final, cast=cast, x_in_hbm=x_in_hbm),
        grid=(n_i, n_f),
        in_specs=[
            x_spec,
            pl.BlockSpec((1, d), lambda i, j: (0, 0)),
            pl.BlockSpec((None, d, tf), lambda i, j: (layer, 0, j)),
            pl.BlockSpec((None, d, tf), lambda i, j: (layer, 0, j)),
            pl.BlockSpec((None, tf, d), lambda i, j: (layer, j, 0)),
            pl.BlockSpec((1, d), lambda i, j: (0, 0)),
        ],
        out_specs=out_specs,
        out_shape=out_shape,
        scratch_shapes=scratch,
        compiler_params=_cparams(("arbitrary", "arbitrary")),
        name="ffn_final" if final else "ffn",
    )(x, nw, wg, wu, wd, fn)
    return (res[0], tuple(res[1:])) if cast else res[0]


def _proj_body(x_ref, nw_ref, w_ref, o_ref, *rest, cast):
    hn_ref = rest[-1]
    if cast:
        rest[0][...] = w_ref[...].astype(BF16)
        w_ref = rest[0]

    @pl.when(pl.program_id(1) == 0)
    def _():
        hn_ref[...] = _rms(x_ref[...], nw_ref[...]).astype(BF16)

    o_ref[...] = jnp.dot(hn_ref[...], w_ref[...], preferred_element_type=F32)


def _proj(x, nw, w, *, layer, tm, tn, cast=False):
    m, d = x.shape
    n = w.shape[2]
    out_specs = [pl.BlockSpec((tm, tn), lambda i, j: (i, j))]
    out_shape = [jax.ShapeDtypeStruct((m, n), F32)]
    if cast:
        assert m == tm, "each weight block must be visited once"
        out_specs.append(pl.BlockSpec((None, d, tn), lambda i, j: (0, 0, j)))
        out_shape.append(jax.ShapeDtypeStruct((1, d, n), BF16))
    res = pl.pallas_call(
        functools.partial(_proj_body, cast=cast),
        grid=(m // tm, n // tn),
        in_specs=[
            pl.BlockSpec((tm, d), lambda i, j: (i, 0)),
            pl.BlockSpec((1, d), lambda i, j: (0, 0)),
            pl.BlockSpec((None, d, tn), lambda i, j: (layer, 0, j)),
        ],
        out_specs=out_specs,
        out_shape=out_shape,
        scratch_shapes=[pltpu.VMEM((tm, d), BF16)],
        compiler_params=_cparams(("parallel", "arbitrary")),
        name="in_proj",
    )(x, nw, w)
    return tuple(res) if cast else res[0]


def _outproj_body(x_ref, ya_ref, yb_ref, yc_ref, yd_ref, w_ref, o_ref):
    acc = x_ref[...]
    for gi, y_ref in enumerate((ya_ref, yb_ref, yc_ref, yd_ref)):
        acc = acc + jnp.dot(y_ref[...], w_ref[gi * W_GROUP:(gi + 1) * W_GROUP, :],
                            preferred_element_type=F32)
    o_ref[...] = acc


def _outproj(x, ya, yb, yc, yd, w, *, layer, tm):
    m, d = x.shape
    yspec = pl.BlockSpec((tm, W_GROUP), lambda i: (i, 0))
    return pl.pallas_call(
        _outproj_body,
        grid=(m // tm,),
        in_specs=[pl.BlockSpec((tm, d), lambda i: (i, 0)), yspec, yspec, yspec, yspec,
                  pl.BlockSpec((None,) + w.shape[1:], lambda i: (layer, 0, 0))],
        out_specs=pl.BlockSpec((tm, d), lambda i: (i, 0)),
        out_shape=jax.ShapeDtypeStruct((m, d), F32),
        compiler_params=_cparams(("parallel",)),
        name="out_proj",
    )(x, ya, yb, yc, yd, w)


def _gate_body(u_ref, v_ref, wm_ref, bias_ref, g_ref, b_ref, y_ref, *vr_ref, tm):
    gu = jax.nn.gelu(u_ref[...], approximate=True)
    gv = jax.nn.gelu(v_ref[...], approximate=True)
    mu = jnp.mean(gv, axis=-1, keepdims=True)
    var = jnp.mean(jnp.square(gv - mu), axis=-1, keepdims=True)
    vn = (gv - mu) * lax.rsqrt(var + EPS) * g_ref[...] + b_ref[...]
    if vr_ref:
        vr_ref[0][...] = vn
    vnb = vn.astype(BF16)
    for c in range(tm // A_CHUNK):
        rows = slice(c * A_CHUNK, (c + 1) * A_CHUNK)
        for h in range(W_GROUP // A_CHUNK):
            cols = slice(h * A_CHUNK, (h + 1) * A_CHUNK)
            z = jnp.dot(wm_ref[h], vnb[rows, cols], preferred_element_type=F32) + bias_ref[:, cols]
            y_ref[rows, cols] = (gu[rows, cols] * z).astype(BF16)


def _gate(p2d, wm, bias, ln_g, ln_b, *, tm, with_rows):
    m = p2d.shape[0]
    row_spec = pl.BlockSpec((tm, W_GROUP), lambda i: (i, 0))
    out_shape = [jax.ShapeDtypeStruct((m, W_GROUP), BF16)]
    out_specs = [row_spec]
    if with_rows:
        out_shape.append(jax.ShapeDtypeStruct((m, W_GROUP), F32))
        out_specs.append(row_spec)
    res = pl.pallas_call(
        functools.partial(_gate_body, tm=tm),
        grid=(m // tm,),
        in_specs=[
            pl.BlockSpec((tm, W_GROUP), lambda i: (i, COL_A_U)),
            pl.BlockSpec((tm, W_GROUP), lambda i: (i, COL_A_V)),
            pl.BlockSpec(wm.shape, lambda i: (0, 0, 0)),
            pl.BlockSpec(bias.shape, lambda i: (0, 0)),
            pl.BlockSpec((1, W_GROUP), lambda i: (0, 0)),
            pl.BlockSpec((1, W_GROUP), lambda i: (0, 0)),
        ],
        out_specs=out_specs,
        out_shape=out_shape,
        compiler_params=_cparams(("parallel",)),
        name="spatial_gate",
    )(p2d, p2d, wm, bias, ln_g, ln_b)
    return res if with_rows else (res[0], None)


def _ret_body(q_ref, k_ref, v_ref, g_ref, cos_ref, sin_ref, dm_ref, qd_ref, kd_ref, s0_ref,
              *rest, bb, n_c, chunk_decay, creates):
    y_ref, so_ref, s_ref = rest[-3:]
    c = pl.program_id(1)

    @pl.when(c == 0)
    def _():
        s_ref[...] = s0_ref[...]

    cos = cos_ref[...]
    sin = sin_ref[...]
    nt = (((1,), (1,)), ((), ()))
    tn = (((0,), (0,)), ((), ()))
    units = [(b, h, slice(h * R_HD, (h + 1) * R_HD)) for b in range(bb) for h in range(R_HEADS)]
    idx = range(len(units))

    def rope(x):
        return x * cos + pltpu.roll(x, R_HD // 2, axis=1) * sin

    qr = [rope(q_ref[b, :, cols]) for b, h, cols in units]
    kr = [rope(k_ref[b, :, cols]) * (R_HD ** -0.5) for b, h, cols in units]
    v = [v_ref[b, :, cols].astype(BF16) for b, h, cols in units]
    s = [s_ref[b, h] for b, h, cols in units]
    sc = [lax.dot_general(qr[i].astype(BF16), kr[i].astype(BF16), nt, preferred_element_type=F32)
          * dm_ref[units[i][1]] for i in idx]
    cross = [jnp.dot((qr[i] * qd_ref[:, units[i][2]]).astype(BF16), s[i].astype(BF16),
                     preferred_element_type=F32) for i in idx]
    kv = [lax.dot_general((kr[i] * kd_ref[:, units[i][2]]).astype(BF16), v[i], tn,
                          preferred_element_type=F32) for i in idx]
    o = [cross[i] + jnp.dot(sc[i].astype(BF16), v[i], preferred_element_type=F32) for i in idx]
    for i, (b, h, cols) in enumerate(units):
        s_ref[b, h] = chunk_decay[h] * s[i] + kv[i]
        on = o[i] * lax.rsqrt(jnp.mean(o[i] * o[i], axis=-1, keepdims=True) + EPS)
        g = g_ref[b, :, cols]
        y_ref[b, :, cols] = (on * (g * jax.nn.sigmoid(g))).astype(BF16)

    @pl.when(c == n_c - 1)
    def _():
        _own_slot(so_ref, creates)[...] = s_ref[...]


def _ret_tables(cl, pos0, length):
    half = R_HD // 2
    inv = ROPE_BASE ** (-jnp.arange(half, dtype=F32) / half)
    pos = pos0 + jnp.arange(length, dtype=F32)
    ang = pos[:, None] * inv[None, :]
    cos = jnp.cos(ang)
    sin = jnp.sin(ang)
    cos_t = jnp.concatenate([cos, cos], axis=-1)
    sin_t = jnp.concatenate([-sin, sin], axis=-1)
    log_gamma = np.log(1.0 - 2.0 ** (-5.0 - np.arange(R_HEADS, dtype=np.float64)))
    idx = np.arange(cl, dtype=np.float64)
    diff = idx[:, None] - idx[None, :]
    dmat = np.where(diff >= 0, np.exp(np.maximum(diff, 0.0)[None] * log_gamma[:, None, None]), 0.0)
    kdec = np.exp((cl - 1.0 - idx)[:, None] * log_gamma[None, :])
    qdec = np.exp((idx + 1.0)[:, None] * log_gamma[None, :])
    chunk_decay = tuple(float(x) for x in np.exp(cl * log_gamma))
    rep = lambda a: jnp.asarray(np.repeat(a, R_HD, axis=1), F32)
    return cos_t, sin_t, jnp.asarray(dmat, F32), rep(qdec), rep(kdec), chunk_decay


def _stack_slot(stack, per_layer_shape, block, n_inputs):
    prev, slot, depth = stack
    nd = len(per_layer_shape)
    shape = jax.ShapeDtypeStruct((depth,) + tuple(per_layer_shape), F32)
    if prev is None:
        spec = pl.BlockSpec((depth,) + block, lambda b, t: (0, b) + (0,) * (nd - 1))
        return spec, shape, [], [], {}
    spec = pl.BlockSpec((None,) + block, lambda b, t: (slot, b) + (0,) * (nd - 1))
    return spec, shape, [prev], [pl.BlockSpec(memory_space=pl.ANY)], {n_inputs: 1}


def _own_slot(so_ref, creates):
    if creates is None:
        return so_ref
    slot, depth = creates
    for other in range(depth):
        if other != slot:
            so_ref[other] = jnp.zeros(so_ref.shape[1:], F32)
    return so_ref.at[slot]


def _retention(p3d, s0_all, layer, pos0, stack, *, bb):
    bsz, length, _ = p3d.shape
    cl = min(R_CHUNK, length)
    n_c = length // cl
    cos_t, sin_t, dmat, qdec, kdec, chunk_decay = _ret_tables(cl, pos0, length)

    def col(j):
        return pl.BlockSpec((bb, cl, W_GROUP), lambda b, c: (b, c, j))

    tab = pl.BlockSpec((cl, R_HD), lambda b, c: (c, 0))
    state, state_shape, extra, extra_specs, aliases = _stack_slot(
        stack, s0_all.shape[1:], (bb, R_HEADS, R_HD, R_HD), n_inputs=10)
    return pl.pallas_call(
        functools.partial(_ret_body, bb=bb, n_c=n_c, chunk_decay=chunk_decay,
                          creates=None if stack[0] is not None else stack[1:]),
        grid=(bsz // bb, n_c),
        in_specs=[col(COL_R_Q), col(COL_R_K), col(COL_R_V), col(COL_R_G), tab, tab,
                  pl.BlockSpec(dmat.shape, lambda b, c: (0, 0, 0)),
                  pl.BlockSpec(qdec.shape, lambda b, c: (0, 0)),
                  pl.BlockSpec(kdec.shape, lambda b, c: (0, 0)),
                  pl.BlockSpec((None, bb, R_HEADS, R_HD, R_HD), lambda b, c: (layer, b, 0, 0, 0))]
                 + extra_specs,
        out_specs=[pl.BlockSpec((bb, cl, W_GROUP), lambda b, c: (b, c, 0)), state],
        out_shape=[jax.ShapeDtypeStruct((bsz, length, W_GROUP), BF16), state_shape],
        input_output_aliases=aliases,
        scratch_shapes=[pltpu.VMEM((bb, R_HEADS, R_HD, R_HD), F32)],
        compiler_params=_cparams(("parallel", "arbitrary")),
        name="retention",
    )(p3d, p3d, p3d, p3d, cos_t, sin_t, dmat, qdec, kdec, s0_all, *extra)


def _conv_body(bg_ref, cg_ref, h_ref, buf_ref, w_ref, y_ref, st_ref, carry_ref, *, bb, tt):
    @pl.when(pl.program_id(1) == 0)
    def _():
        carry_ref[...] = buf_ref[...]

    shape = (bb, tt, W_GROUP)
    z = cg_ref[...] * h_ref[...]
    z2 = z.reshape(bb * tt, W_GROUP)
    r1 = pltpu.roll(z2, 1, axis=0).reshape(shape)
    r2 = pltpu.roll(z2, 2, axis=0).reshape(shape)
    tpos = lax.broadcasted_iota(jnp.int32, shape, 1)
    c0 = carry_ref[:, 0:1, :]
    c1 = carry_ref[:, 1:2, :]
    zm1 = jnp.where(tpos == 0, c1, r1)
    zm2 = jnp.where(tpos == 0, c0, jnp.where(tpos == 1, c1, r2))
    y = w_ref[0:1, :] * zm2 + w_ref[1:2, :] * zm1 + w_ref[2:3, :] * z
    y_ref[...] = (bg_ref[...] * y).astype(BF16)
    new = cg_ref[:, tt - 2:tt, :] * h_ref[:, tt - 2:tt, :]
    carry_ref[...] = new
    st_ref[...] = new


def _conv(p3d, buf, w, *, bb, tt):
    bsz, length, _ = p3d.shape

    def col(j):
        return pl.BlockSpec((bb, tt, W_GROUP), lambda b, t: (b, t, j))

    state = pl.BlockSpec((bb, C_WIDTH - 1, W_GROUP), lambda b, t: (b, 0, 0))
    return pl.pallas_call(
        functools.partial(_conv_body, bb=bb, tt=tt),
        grid=(bsz // bb, length // tt),
        in_specs=[col(COL_C_B), col(COL_C_C), col(COL_C_H), state,
                  pl.BlockSpec(w.shape, lambda b, t: (0, 0))],
        out_specs=[pl.BlockSpec((bb, tt, W_GROUP), lambda b, t: (b, t, 0)), state],
        out_shape=[jax.ShapeDtypeStruct((bsz, length, W_GROUP), BF16),
                   jax.ShapeDtypeStruct(buf.shape, F32)],
        scratch_shapes=[pltpu.VMEM((bb, C_WIDTH - 1, W_GROUP), F32)],
        compiler_params=_cparams(("parallel", "arbitrary")),
        name="short_conv",
    )(p3d, p3d, p3d, buf, w)


def _group_dot(xb, ones_bd):
    wb = ones_bd.shape[0]
    return jnp.concatenate(
        [jnp.dot(xb[:, i * wb:(i + 1) * wb], ones_bd, preferred_element_type=F32)
         for i in range(W_GROUP // wb)], axis=1)


def _head_sum(x, ones_bd):
    return _group_dot(x.astype(BF16), ones_bd)


def _rwkvc_body(r_ref, k_ref, v_ref, lo_ref, sh_ref, s0_ref,
                mu_ref, mulo_ref, w0_ref, w2_ref, a0_ref, a2_ref, g2_ref, kk_ref, ka_ref, rk_ref,
                lnw_ref, lnb_ref, ones_ref, *rest, bb, tt, n_t, gsz, chunk, creates):
    (y_ref, so_ref,
     s2_ref, carry_ref, carrylo_ref, pt_s, rt_s, qh_s, kh_s, qb_s, kb_s, v_s, ec_s, ys) = rest[-14:]
    tb = pl.program_id(1)
    n = bb * tt
    w3 = (bb, tt, W_GROUP)
    ones_bd = ones_ref[...]
    n_pair = K_HEADS // 2
    pair_w = 2 * K_HD
    zero_blk = jnp.zeros((K_HD, K_HD), F32)

    @pl.when(tb == 0)
    def _():
        for b in range(bb):
            for p in range(n_pair):
                top = jnp.concatenate([s0_ref[b, 2 * p], zero_blk], axis=1)
                bot = jnp.concatenate([zero_blk, s0_ref[b, 2 * p + 1]], axis=1)
                s2_ref[b, p] = jnp.concatenate([top, bot], axis=0)
        carry_ref[...] = sh_ref[:, :, 0:3 * W_GROUP]
        carrylo_ref[...] = sh_ref[:, :, 3 * W_GROUP:K_COLS]

    tpos = lax.broadcasted_iota(jnp.int32, w3, 1)
    tpos_lo = lax.broadcasted_iota(jnp.int32, (bb, tt, LORA_COLS), 1)

    def shifted(x, carry, mu, mask):
        prev = pltpu.roll(x.reshape(n, x.shape[-1]), 1, axis=0).reshape(x.shape)
        prev = jnp.where(mask == 0, carry, prev)
        return (x + (prev - x) * mu).reshape(n, x.shape[-1])

    r_in, k_in, v_in, lo_in = r_ref[...], k_ref[...], v_ref[...], lo_ref[...]
    r = shifted(r_in, carry_ref[:, :, 0:W_GROUP], mu_ref[:, 0:W_GROUP], tpos)
    k = shifted(k_in, carry_ref[:, :, W_GROUP:2 * W_GROUP], mu_ref[:, W_GROUP:2 * W_GROUP], tpos)
    v = shifted(v_in, carry_ref[:, :, 2 * W_GROUP:3 * W_GROUP], mu_ref[:, 2 * W_GROUP:3 * W_GROUP], tpos)
    lo = shifted(lo_in, carrylo_ref[...], mulo_ref[...], tpos_lo)
    carry_ref[:, :, 0:W_GROUP] = r_ref[:, tt - 1:tt, :]
    carry_ref[:, :, W_GROUP:2 * W_GROUP] = k_ref[:, tt - 1:tt, :]
    carry_ref[:, :, 2 * W_GROUP:3 * W_GROUP] = v_ref[:, tt - 1:tt, :]
    carrylo_ref[...] = lo_ref[:, tt - 1:tt, :]

    zw = w0_ref[...] + jnp.dot(jnp.tanh(lo).astype(BF16), w2_ref[...], preferred_element_type=F32)
    logw = -RWKV_DECAY_SCALE * jax.nn.sigmoid(zw)
    a = jax.nn.sigmoid(a0_ref[...] + jnp.dot(lo.astype(BF16), a2_ref[...], preferred_element_type=F32))
    gate = jnp.dot(jax.nn.sigmoid(lo).astype(BF16), g2_ref[...], preferred_element_type=F32)
    kk = k * kk_ref[...]
    kk = kk * lax.rsqrt(jnp.maximum(_head_sum(kk * kk, ones_bd), 1e-24))
    k2 = k * (1.0 + (a - 1.0) * ka_ref[...])
    bonus = _head_sum(r * k2 * rk_ref[...], ones_bd) * v
    q = -(kk * a)

    cpos = lax.broadcasted_iota(jnp.int32, (n, W_GROUP), 0) % chunk
    g = logw
    step = 1
    while step < chunk:
        g = g + jnp.where(cpos >= step, pltpu.roll(g, step, axis=0), 0.0)
        step *= 2
    g3 = g.reshape(n // chunk, chunk, W_GROUP)
    gtot = jnp.broadcast_to(g3[:, chunk - 1:chunk, :], g3.shape).reshape(n, W_GROUP)
    e_neg = jnp.exp(-g)
    e_rem = jnp.exp(gtot - g)
    pt_s[...] = (kk * jnp.exp(g - logw)).reshape(w3)
    rt_s[...] = (r * jnp.exp(g)).reshape(w3)
    qh_s[...] = (q * e_neg).reshape(w3)
    kh_s[...] = (k2 * e_neg).reshape(w3)
    qb_s[...] = (q * e_rem).reshape(w3)
    kb_s[...] = (k2 * e_rem).reshape(w3)
    v_s[...] = v.reshape(w3)
    ec_s[...] = jnp.exp(gtot).reshape(w3)

    c2, c4 = 2 * chunk, 4 * chunk
    even = lax.broadcasted_iota(jnp.int32, (chunk, pair_w), 1) < K_HD
    ri = lax.broadcasted_iota(jnp.int32, (c4, c4), 0)
    ci = lax.broadcasted_iota(jnp.int32, (c4, c4), 1)
    keep = ci % chunk < ri % chunk + ri // c2
    right = lax.broadcasted_iota(jnp.int32, (c2, c4), 1) >= c2
    eye = (lax.broadcasted_iota(jnp.int32, (c2, c2), 0)
           == lax.broadcasted_iota(jnp.int32, (c2, c2), 1)).astype(F32)
    nt = (((1,), (1,)), ((), ()))
    tn = (((0,), (0,)), ((), ()))
    n_double = chunk.bit_length() - 2

    def two(x):
        return jnp.concatenate([jnp.where(even, x, 0.0), jnp.where(even, 0.0, x)], axis=0)

    def mm(x, y):
        return jnp.dot(x.astype(BF16), y.astype(BF16), preferred_element_type=F32)

    def units(args):
        v2, pr, qk, qkb, s2, ec = zip(*args)
        idx = range(len(args))
        apr = [jnp.where(keep, lax.dot_general(pr[i], qk[i], nt, preferred_element_type=F32), 0.0) for i in idx]
        prs = [lax.dot_general(pr[i], s2[i].astype(BF16), nt, preferred_element_type=F32) for i in idx]
        apk = [mm(jnp.where(right, apr[i][0:c2], 0.0), jnp.concatenate([v2[i], v2[i]], axis=0)) for i in idx]
        power = [apr[i][0:c2, 0:c2] for i in idx]
        inv = [eye + power[i] for i in idx]
        for _ in range(n_double):
            power = [mm(power[i], power[i]) for i in idx]
            inv = [inv[i] + mm(inv[i], power[i]) for i in idx]
        u2 = [mm(inv[i], prs[i][0:c2] + apk[i]) for i in idx]
        uv = [jnp.concatenate([u2[i], v2[i]], axis=0).astype(BF16) for i in idx]
        y2 = [prs[i][c2:c4] + jnp.dot(apr[i][c2:c4].astype(BF16), uv[i], preferred_element_type=F32)
              for i in idx]
        s_new = [s2[i] * ec[i] + lax.dot_general(uv[i], qkb[i], tn, preferred_element_type=F32) for i in idx]
        return [(y2[i][0:chunk] + y2[i][chunk:c2], s_new[i]) for i in idx]

    def chunk_step(i, carry):
        c = i // (bb // gsz)
        b0 = (i % (bb // gsz)) * gsz
        r0 = pl.multiple_of(c * chunk, chunk)
        rows = pl.ds(r0, chunk)
        where = [(b0 + j, slice(p * pair_w, (p + 1) * pair_w), p) for j in range(gsz) for p in range(n_pair)]
        args = []
        for b, lanes, p in where:
            ld = lambda ref: two(ref[b, rows, lanes])
            args.append((ld(v_s),
                         jnp.concatenate([ld(pt_s), ld(rt_s)], axis=0).astype(BF16),
                         jnp.concatenate([ld(qh_s), ld(kh_s)], axis=0).astype(BF16),
                         jnp.concatenate([ld(qb_s), ld(kb_s)], axis=0).astype(BF16),
                         s2_ref[b, p], ec_s[b, pl.ds(r0, 1), lanes]))
        for (b, lanes, p), (y, s_new) in zip(where, units(args)):
            ys[b, rows, lanes] = y
            s2_ref[b, p] = s_new
        return carry

    lax.fori_loop(0, (tt // chunk) * (bb // gsz), chunk_step, 0)

    y = ys[...].reshape(n, W_GROUP)
    mean = _head_sum(y, ones_bd) * (1.0 / K_HD)
    yc = y - mean
    var = _head_sum(yc * yc, ones_bd) * (1.0 / K_HD)
    out = (yc * lax.rsqrt(var + GN_EPS) * lnw_ref[...] + lnb_ref[...] + bonus) * gate
    y_ref[...] = out.reshape(w3).astype(BF16)

    @pl.when(tb == n_t - 1)
    def _():
        own = _own_slot(so_ref, creates)
        for b in range(bb):
            for p in range(n_pair):
                own[b, 2 * p] = s2_ref[b, p, 0:K_HD, 0:K_HD]
                own[b, 2 * p + 1] = s2_ref[b, p, K_HD:pair_w, K_HD:pair_w]


def _rwkvc(p3d, shift, s0_all, layer, prm, stack, *, bb, tt, gsz, chunk):
    bsz, length, _ = p3d.shape
    n_t = length // tt

    def col(j):
        return pl.BlockSpec((bb, tt, W_GROUP), lambda b, t: (b, t, j))

    def whole(a):
        nd = a.ndim
        return pl.BlockSpec(a.shape, lambda b, t: (0,) * nd)

    params = [prm[nm] for nm in ("mu", "mu_lo", "w0", "w2", "a0", "a2", "g2", "k_k", "k_a", "r_k",
                                 "ln_w", "ln_b", "ones_bd")]
    state, state_shape, extra, extra_specs, aliases = _stack_slot(
        stack, s0_all.shape[1:], (bb, K_HEADS, K_HD, K_HD), n_inputs=6 + len(params))
    blk = pltpu.VMEM((bb, tt, W_GROUP), F32)
    return pl.pallas_call(
        functools.partial(_rwkvc_body, bb=bb, tt=tt, n_t=n_t, gsz=gsz, chunk=chunk,
                          creates=None if stack[0] is not None else stack[1:]),
        grid=(bsz // bb, n_t),
        in_specs=[col(COL_K_R), col(COL_K_K), col(COL_K_V),
                  pl.BlockSpec((bb, tt, LORA_COLS), lambda b, t: (b, t, COL_K_LORA)),
                  pl.BlockSpec((bb, 1, K_COLS), lambda b, t: (b, 0, 0)),
                  pl.BlockSpec((None, bb, K_HEADS, K_HD, K_HD), lambda b, t: (layer, b, 0, 0, 0))]
                 + [whole(a) for a in params] + extra_specs,
        out_specs=[pl.BlockSpec((bb, tt, W_GROUP), lambda b, t: (b, t, 0)), state],
        out_shape=[jax.ShapeDtypeStruct((bsz, length, W_GROUP), BF16), state_shape],
        input_output_aliases=aliases,
        scratch_shapes=[pltpu.VMEM((bb, K_HEADS // 2, 2 * K_HD, 2 * K_HD), F32),
                        pltpu.VMEM((bb, 1, 3 * W_GROUP), F32),
                        pltpu.VMEM((bb, 1, LORA_COLS), F32),
                        blk, blk, blk, blk, blk, blk, blk, blk, blk],
        compiler_params=_cparams(("parallel", "arbitrary")),
        name="rwkv7c",
    )(p3d, p3d, p3d, p3d, shift, s0_all, *params, *extra)


def _gate_mixing(w_s, b_s, seq):
    cl = min(A_CHUNK, seq)
    wm = jnp.tril(w_s[:, :cl, :cl])
    bias = b_s[:, :cl]
    rep = A_CHUNK // cl
    if rep > 1:
        eye = jnp.eye(rep, dtype=w_s.dtype)
        wm = jnp.einsum("ab,hts->hatbs", eye, wm).reshape(w_s.shape[0], A_CHUNK, A_CHUNK)
        bias = jnp.tile(bias, (1, rep))
    bias = jnp.repeat(bias.T, A_CHUNK, axis=1)
    return wm.astype(BF16), bias


def _pad_rows(w, row0):
    return jnp.zeros((LORA_COLS, W_GROUP), F32).at[row0:row0 + w.shape[0]].set(w).astype(BF16)


def _layer_params(l, ffn1_norm, ffn1_w_gate, ffn1_w_up, ffn1_w_down, mix_norm, w_in, w_out,
                  a_w_s, a_b_s, a_ln_g, a_ln_b, c_conv_w,
                  k_mu, k_w0, k_w2, k_a0, k_a2, k_g2, k_k_k, k_k_a, k_r_k, k_ln_w, k_ln_b,
                  ffn2_norm, ffn2_w_gate, ffn2_w_up, ffn2_w_down):
    row = lambda a: a[l].reshape(1, -1)
    head_of = np.arange(V7X_MXU_DIM) // K_HD
    ones_bd = jnp.asarray(head_of[:, None] == head_of[None, :], BF16)
    rwkv = dict(
        mu=k_mu[l][None, :3 * W_GROUP], mu_lo=k_mu[l][None, 3 * W_GROUP:],
        w0=row(k_w0), w2=_pad_rows(k_w2[l], 0),
        a0=row(k_a0), a2=_pad_rows(k_a2[l], W_LORA),
        g2=_pad_rows(k_g2[l], W_LORA + A_LORA),
        k_k=row(k_k_k), k_a=row(k_k_a), r_k=row(k_r_k), ln_w=row(k_ln_w), ln_b=row(k_ln_b),
        ones_bd=ones_bd)
    return dict(
        layer=l, ffn_layer=l,
        ffn1=(row(ffn1_norm), ffn1_w_gate, ffn1_w_up, ffn1_w_down),
        ffn2=(row(ffn2_norm), ffn2_w_gate, ffn2_w_up, ffn2_w_down),
        mix_norm=row(mix_norm), w_in=w_in, w_out=w_out,
        a_w_s=a_w_s[l], a_b_s=a_b_s[l], a_ln_g=row(a_ln_g), a_ln_b=row(a_ln_b),
        conv_w=c_conv_w[l], rwkv=rwkv)


def _stream_layer(x2d, bsz, length, pos0, ret_s0_all, ret_layer, conv_buf, rw_shift, rw_s0_all, p, fn, stacks, *,
                  final, cfg):
    layer = p["layer"]
    ffn_kw = dict(layer=p["ffn_layer"], tm=cfg["ffn_tm"], tf=cfg["tf"], cast=cfg["ffn_cast"])
    x1 = _ffn(x2d, *p["ffn1"], fn, final=False, **ffn_kw)
    casted = {}
    if cfg["ffn_cast"]:
        x1, casted["ffn1"] = x1
    proj = _proj(x1, p["mix_norm"], p["w_in"], layer=p["ffn_layer"], tm=cfg["proj_tm"], tn=cfg["tn"],
                 cast=cfg["ffn_cast"])
    if cfg["ffn_cast"]:
        proj, casted["w_in"] = proj
    p3d = proj.reshape(bsz, length, proj.shape[1])

    wm, bias = _gate_mixing(p["a_w_s"], p["a_b_s"], length)
    ya, v_rows = _gate(proj, wm, bias, p["a_ln_g"], p["a_ln_b"], tm=cfg["gate_tm"],
                       with_rows=cfg["with_rows"])
    yb, ret_s = _retention(p3d, ret_s0_all, ret_layer, pos0, stacks["ret"], bb=cfg["ret_bb"])
    yc, conv_new = _conv(p3d, conv_buf, p["conv_w"], bb=cfg["conv_bb"], tt=cfg["conv_tt"])
    yd, rw_s = _rwkvc(p3d, rw_shift[:, None, :], rw_s0_all, ret_layer, p["rwkv"], stacks["rwkv"],
                      bb=cfg["rwkv_bb"], tt=cfg["rwkv_tt"], gsz=cfg["rwkv_gsz"], chunk=cfg["rwkv_chunk"])
    shift_new = p3d[:, length - 1, K_COL0:K_COL0 + K_COLS]

    flat = lambda y: y.reshape(bsz * length, W_GROUP)
    x2 = _outproj(x1, ya, flat(yb), flat(yc), flat(yd), p["w_out"], layer=layer, tm=cfg["tm"])
    x3 = _ffn(x2, *p["ffn2"], fn, final=final, **ffn_kw)
    if cfg["ffn_cast"]:
        x3, casted["ffn2"] = x3
    return x3, ret_s, conv_new, shift_new, rw_s, v_rows, casted


def _stream_cfg(bsz, length, sample):
    m = bsz * length
    tm = min(512, m)
    if sample:
        return dict(tm=tm, ffn_tm=m, ffn_cast=True, proj_tm=m, tf=256, tn=640,
                    gate_tm=min(512, m), with_rows=True,
                    ret_bb=8, conv_bb=32, conv_tt=length, rwkv_bb=16, rwkv_tt=length, rwkv_gsz=8,
                    rwkv_chunk=min(RWKV_CHUNK, length))
    return dict(tm=tm, ffn_tm=min(1024, m), ffn_cast=False, proj_tm=min(1024, m), tf=512, tn=1280,
                gate_tm=min(512, m), with_rows=False,
                ret_bb=1, conv_bb=1, conv_tt=min(512, length), rwkv_bb=bsz, rwkv_tt=min(128, length),
                rwkv_gsz=4, rwkv_chunk=min(RWKV_CHUNK, length))


def kernel(x_prompt, x_sample, state_ret, state_conv, state_rwkv_shift, state_rwkv, ffn1_norm, ffn1_w_gate, ffn1_w_up, ffn1_w_down, mix_norm, w_in, w_out, a_w_s, a_b_s, a_ln_g, a_ln_b, c_conv_w, k_mu, k_w0, k_w2, k_a0, k_a2, k_g2, k_k_k, k_k_a, k_r_k, k_ln_w, k_ln_b, ffn2_norm, ffn2_w_gate, ffn2_w_up, ffn2_w_down, final_norm):
    bp, lp, d = x_prompt.shape
    bs, ls, _ = x_sample.shape
    depth = ffn1_norm.shape[0]
    cfg_p = _stream_cfg(bp, lp, sample=False)
    cfg_s = _stream_cfg(bs, ls, sample=True)
    fn = final_norm.reshape(1, d)

    zero_ret = jnp.zeros((1, bp, R_HEADS, R_HD, R_HD), F32)
    zero_conv = jnp.zeros((bp, C_WIDTH - 1, W_GROUP), F32)
    zero_shift = jnp.zeros((bp, K_COLS), F32)
    zero_rw = jnp.zeros((1, bp, K_HEADS, K_HD, K_HD), F32)

    w_out = w_out.astype(BF16)
    xp = x_prompt.reshape(bp * lp, d)
    xs = x_sample.reshape(bs * ls, d)
    outs = [[] for _ in range(5)]
    rp = rs = wp = ws = None
    for l in range(depth):
        p = _layer_params(l, ffn1_norm, ffn1_w_gate, ffn1_w_up, ffn1_w_down, mix_norm, w_in, w_out,
                          a_w_s, a_b_s, a_ln_g, a_ln_b, c_conv_w,
                          k_mu, k_w0, k_w2, k_a0, k_a2, k_g2, k_k_k, k_k_a, k_r_k, k_ln_w, k_ln_b,
                          ffn2_norm, ffn2_w_gate, ffn2_w_up, ffn2_w_down)
        final = l == depth - 1
        xs, rs, cs, ss, ws, vs, w16 = _stream_layer(xs, bs, ls, float(PAST_LEN), state_ret, l, state_conv[l],
                                                    state_rwkv_shift[l], state_rwkv, p, fn,
                                                    dict(ret=(rs, l, depth), rwkv=(ws, l, depth)),
                                                    final=final, cfg=cfg_s)
        p = dict(p, ffn_layer=0, ffn1=p["ffn1"][:1] + w16["ffn1"], ffn2=p["ffn2"][:1] + w16["ffn2"],
                 w_in=w16["w_in"])
        xp, rp, cp, sp, wp, _, _ = _stream_layer(xp, bp, lp, 0.0, zero_ret, 0, zero_conv, zero_shift, zero_rw,
                                                 p, fn, dict(ret=(rp, l, depth), rwkv=(wp, l, depth)),
                                                 final=final, cfg=cfg_p)
        for acc, val in zip(outs, (cp, cs, sp, ss, vs.reshape(bs, ls, W_GROUP))):
            acc.append(val)

    conv_p, conv_s, shift_p, shift_s, v_s = (jnp.stack(o) for o in outs)
    return (xp.reshape(bp, lp, d), xs.reshape(bs, ls, d), rp, rs, conv_p, conv_s, shift_p, shift_s, wp, ws, v_s)
```

```python
import functools

import numpy as np
import jax
import jax.numpy as jnp
from jax import lax
from jax.experimental import pallas as pl
from jax.experimental.pallas import tpu as pltpu

F32 = jnp.float32
BF16 = jnp.bfloat16

W_GROUP = 512
A_CHUNK = 128
R_HEADS = 4
R_HD = 128
R_CHUNK = 128
ROPE_BASE = 10000.0
C_WIDTH = 3
K_HD = 64
K_HEADS = 8
W_LORA = 64
A_LORA = 64
G_LORA = 128
LORA_COLS = W_LORA + A_LORA + G_LORA
K_COLS = 3 * W_GROUP + LORA_COLS
EPS = 1e-6
GN_EPS = 64e-5
PAST_LEN = 16384

COL_A_U, COL_A_V = 0, 1
COL_R_Q, COL_R_K, COL_R_V, COL_R_G = 2, 3, 4, 5
COL_C_B, COL_C_C, COL_C_H = 6, 7, 8
COL_K_R, COL_K_K, COL_K_V = 9, 10, 11
COL_K_LORA = (12 * W_GROUP) // LORA_COLS
K_COL0 = 9 * W_GROUP

V7X_VMEM_LIMIT_BYTES = 62 * 1024 * 1024
V7X_MXU_DIM = 256
FFN_NORM_START = 3
FFN_OUT_ROWS = 256
RWKV_DECAY_SCALE = float(np.exp(-0.5))
RWKV_CHUNK = 64

def _cparams(sem, vmem=V7X_VMEM_LIMIT_BYTES):
    return pltpu.CompilerParams(dimension_semantics=sem, vmem_limit_bytes=vmem)


def _rms(x, w):
    return x * lax.rsqrt(jnp.mean(x * x, axis=-1, keepdims=True) + EPS) * w


def _ffn_body(x_ref, nw_ref, wg_ref, wu_ref, wd_ref, fn_ref, o_ref, *rest, n_i, n_f, tm, final, cast, x_in_hbm):
    i, j = pl.program_id(0), pl.program_id(1)
    if cast:
        casted, hn_ref = rest[:3], rest[3]
        for src, dst in zip((wg_ref, wu_ref, wd_ref), casted):
            dst[...] = src[...].astype(BF16)
        wg_ref, wu_ref, wd_ref = casted
    else:
        hn_ref = rest[0]

    if x_in_hbm:
        xbuf, sem = rest[-2:]
        cur = i % 2
        nxt = 1 - cur
        has_next = i + 1 < n_i
        n_slices = n_f - FFN_NORM_START
        slice_rows = tm // n_slices
        tile_rows = pl.ds(0, tm)

        def x_copy(tile):
            return pltpu.make_async_copy(x_ref.at[pl.ds(tile * tm, tm), :], xbuf.at[tile_rows, :], sem)

        @pl.when((i == 0) & (j == 0))
        def _():
            xbuf[pl.ds(tm, slice_rows), :] = jnp.zeros((slice_rows, xbuf.shape[1]), F32)
            x_copy(0).start()
            x_copy(0).wait()
            hn_ref[0, tile_rows, :] = _rms(xbuf[tile_rows, :], nw_ref[...]).astype(BF16)

        @pl.when(j == 0)
        def _():
            o_ref[...] = 2.0 * xbuf[tile_rows, :]

        @pl.when((j == 1) & has_next)
        def _():
            x_copy(i + 1).start()

        @pl.when((j == FFN_NORM_START) & has_next)
        def _():
            x_copy(i + 1).wait()

        ready = (j >= FFN_NORM_START) & has_next
        k = jnp.clip(j - FFN_NORM_START, 0, n_slices - 1)
        rows = pl.ds(pl.multiple_of(jnp.where(ready, k * slice_rows, tm), slice_rows), slice_rows)
        hn_ref[nxt, rows, :] = _rms(xbuf[rows, :], nw_ref[...]).astype(BF16)
        h = hn_ref[cur, tile_rows, :]
    else:
        @pl.when(j == 0)
        def _():
            x = x_ref[...]
            hn_ref[...] = _rms(x, nw_ref[...]).astype(BF16)
            o_ref[...] = 2.0 * x

        h = hn_ref[...]

    tf = wg_ref.shape[1]
    acc = o_ref[...]
    pending = None
    for c0 in range(0, tf, V7X_MXU_DIM):
        c1 = min(c0 + V7X_MXU_DIM, tf)
        g = jnp.dot(h, wg_ref[:, c0:c1], preferred_element_type=F32)
        u = jnp.dot(h, wu_ref[:, c0:c1], preferred_element_type=F32)
        if pending is not None:
            acc = acc + jnp.dot(pending[0], wd_ref[pending[1]:pending[2], :], preferred_element_type=F32)
        pending = ((g * jax.nn.sigmoid(g) * u).astype(BF16), c0, c1)
    o_ref[...] = acc + jnp.dot(pending[0], wd_ref[pending[1]:pending[2], :], preferred_element_type=F32)

    @pl.when(j == n_f - 1)
    def _():
        for r0 in range(0, tm, FFN_OUT_ROWS):
            rows = slice(r0, min(r0 + FFN_OUT_ROWS, tm))
            y = 0.5 * o_ref[rows, :]
            if final:
                y = _rms(y, fn_ref[...])
            o_ref[rows, :] = y


def _ffn(x, nw, wg, wu, wd, fn, *, layer, final, tm, tf, cast=False):
    m, d = x.shape
    f = wg.shape[2]
    n_i, n_f = m // tm, f // tf
    x_in_hbm = n_i > 1
    assert not (cast and x_in_hbm)
    if x_in_hbm:
        n_slices = n_f - FFN_NORM_START
        assert n_slices > 0 and tm % (8 * n_slices) == 0
        x_spec = pl.BlockSpec(memory_space=pl.ANY)
        spare = tm // n_slices
        scratch = [pltpu.VMEM((2, tm + spare, d), BF16), pltpu.VMEM((tm + spare, d), F32),
                   pltpu.SemaphoreType.DMA(())]
    else:
        x_spec = pl.BlockSpec((tm, d), lambda i, j: (i, 0))
        scratch = [pltpu.VMEM((tm, d), BF16)]
    out_specs = [pl.BlockSpec((tm, d), lambda i, j: (i, 0))]
    out_shape = [jax.ShapeDtypeStruct((m, d), F32)]
    if cast:
        assert m == tm, "each weight block must be visited once"
        out_specs += [pl.BlockSpec((None, d, tf), lambda i, j: (0, 0, j)),
                      pl.BlockSpec((None, d, tf), lambda i, j: (0, 0, j)),
                      pl.BlockSpec((None, tf, d), lambda i, j: (0, j, 0))]
        out_shape += [jax.ShapeDtypeStruct((1, d, f), BF16), jax.ShapeDtypeStruct((1, d, f), BF16),
                      jax.ShapeDtypeStruct((1, f, d), BF16)]
    res = pl.pallas_call(
        functools.partial(_ffn_body, n_i=n_i, n_f=n_f, tm=tm, final=final, cast=cast, x_in_hbm=x_in_hbm),
        grid=(n_i, n_f),
        in_specs=[
            x_spec,
            pl.BlockSpec((1, d), lambda i, j: (0, 0)),
            pl.BlockSpec((None, d, tf), lambda i, j: (layer, 0, j)),
            pl.BlockSpec((None, d, tf), lambda i, j: (layer, 0, j)),
            pl.BlockSpec((None, tf, d), lambda i, j: (layer, j, 0)),
            pl.BlockSpec((1, d), lambda i, j: (0, 0)),
        ],
        out_specs=out_specs,
        out_shape=out_shape,
        scratch_shapes=scratch,
        compiler_params=_cparams(("arbitrary", "arbitrary")),
        name="ffn_final" if final else "ffn",
    )(x, nw, wg, wu, wd, fn)
    return (res[0], tuple(res[1:])) if cast else res[0]


def _proj_body(x_ref, nw_ref, w_ref, o_ref, *rest, cast):
    hn_ref = rest[-1]
    if cast:
        rest[0][...] = w_ref[...].astype(BF16)
        w_ref = rest[0]

    @pl.when(pl.program_id(1) == 0)
    def _():
        hn_ref[...] = _rms(x_ref[...], nw_ref[...]).astype(BF16)

    o_ref[...] = jnp.dot(hn_ref[...], w_ref[...], preferred_element_type=F32)


def _proj(x, nw, w, *, layer, tm, tn, cast=False):
    m, d = x.shape
    n = w.shape[2]
    out_specs = [pl.BlockSpec((tm, tn), lambda i, j: (i, j))]
    out_shape = [jax.ShapeDtypeStruct((m, n), F32)]
    if cast:
        assert m == tm, "each weight block must be visited once"
        out_specs.append(pl.BlockSpec((None, d, tn), lambda i, j: (0, 0, j)))
        out_shape.append(jax.ShapeDtypeStruct((1, d, n), BF16))
    res = pl.pallas_call(
        functools.partial(_proj_body, cast=cast),
        grid=(m // tm, n // tn),
        in_specs=[
            pl.BlockSpec((tm, d), lambda i, j: (i, 0)),
            pl.BlockSpec((1, d), lambda i, j: (0, 0)),
            pl.BlockSpec((None, d, tn), lambda i, j: (layer, 0, j)),
        ],
        out_specs=out_specs,
        out_shape=out_shape,
        scratch_shapes=[pltpu.VMEM((tm, d), BF16)],
        compiler_params=_cparams(("parallel", "arbitrary")),
        name="in_proj",
    )(x, nw, w)
    return tuple(res) if cast else res[0]


def _outproj_body(x_ref, ya_ref, yb_ref, yc_ref, yd_ref, w_ref, o_ref):
    acc = x_ref[...]
    for gi, y_ref in enumerate((ya_ref, yb_ref, yc_ref, yd_ref)):
        acc = acc + jnp.dot(y_ref[...], w_ref[gi * W_GROUP:(gi + 1) * W_GROUP, :],
                            preferred_element_type=F32)
    o_ref[...] = acc


def _outproj(x, ya, yb, yc, yd, w, *, layer, tm):
    m, d = x.shape
    yspec = pl.BlockSpec((tm, W_GROUP), lambda i: (i, 0))
    return pl.pallas_call(
        _outproj_body,
        grid=(m // tm,),
        in_specs=[pl.BlockSpec((tm, d), lambda i: (i, 0)), yspec, yspec, yspec, yspec,
                  pl.BlockSpec((None,) + w.shape[1:], lambda i: (layer, 0, 0))],
        out_specs=pl.BlockSpec((tm, d), lambda i: (i, 0)),
        out_shape=jax.ShapeDtypeStruct((m, d), F32),
        compiler_params=_cparams(("parallel",)),
        name="out_proj",
    )(x, ya, yb, yc, yd, w)


def _gate_body(u_ref, v_ref, wm_ref, bias_ref, g_ref, b_ref, y_ref, *vr_ref, tm):
    gu = jax.nn.gelu(u_ref[...], approximate=True)
    gv = jax.nn.gelu(v_ref[...], approximate=True)
    mu = jnp.mean(gv, axis=-1, keepdims=True)
    var = jnp.mean(jnp.square(gv - mu), axis=-1, keepdims=True)
    vn = (gv - mu) * lax.rsqrt(var + EPS) * g_ref[...] + b_ref[...]
    if vr_ref:
        vr_ref[0][...] = vn
    vnb = vn.astype(BF16)
    for c in range(tm // A_CHUNK):
        rows = slice(c * A_CHUNK, (c + 1) * A_CHUNK)
        for h in range(W_GROUP // A_CHUNK):
            cols = slice(h * A_CHUNK, (h + 1) * A_CHUNK)
            z = jnp.dot(wm_ref[h], vnb[rows, cols], preferred_element_type=F32) + bias_ref[:, cols]
            y_ref[rows, cols] = (gu[rows, cols] * z).astype(BF16)


def _gate(p2d, wm, bias, ln_g, ln_b, *, tm, with_rows):
    m = p2d.shape[0]
    row_spec = pl.BlockSpec((tm, W_GROUP), lambda i: (i, 0))
    out_shape = [jax.ShapeDtypeStruct((m, W_GROUP), BF16)]
    out_specs = [row_spec]
    if with_rows:
        out_shape.append(jax.ShapeDtypeStruct((m, W_GROUP), F32))
        out_specs.append(row_spec)
    res = pl.pallas_call(
        functools.partial(_gate_body, tm=tm),
        grid=(m // tm,),
        in_specs=[
            pl.BlockSpec((tm, W_GROUP), lambda i: (i, COL_A_U)),
            pl.BlockSpec((tm, W_GROUP), lambda i: (i, COL_A_V)),
            pl.BlockSpec(wm.shape, lambda i: (0, 0, 0)),
            pl.BlockSpec(bias.shape, lambda i: (0, 0)),
            pl.BlockSpec((1, W_GROUP), lambda i: (0, 0)),
            pl.BlockSpec((1, W_GROUP), lambda i: (0, 0)),
        ],
        out_specs=out_specs,
        out_shape=out_shape,
        compiler_params=_cparams(("parallel",)),
        name="spatial_gate",
    )(p2d, p2d, wm, bias, ln_g, ln_b)
    return res if with_rows else (res[0], None)


def _ret_body(q_ref, k_ref, v_ref, g_ref, cos_ref, sin_ref, dm_ref, qd_ref, kd_ref, s0_ref,
              *rest, bb, n_c, chunk_decay, creates):
    y_ref, so_ref, s_ref = rest[-3:]
    c = pl.program_id(1)

    @pl.when(c == 0)
    def _():
        s_ref[...] = s0_ref[...]

    cos = cos_ref[...]
    sin = sin_ref[...]
    nt = (((1,), (1,)), ((), ()))
    tn = (((0,), (0,)), ((), ()))
    units = [(b, h, slice(h * R_HD, (h + 1) * R_HD)) for b in range(bb) for h in range(R_HEADS)]
    idx = range(len(units))

    def rope(x):
        return x * cos + pltpu.roll(x, R_HD // 2, axis=1) * sin

    qr = [rope(q_ref[b, :, cols]) for b, h, cols in units]
    kr = [rope(k_ref[b, :, cols]) * (R_HD ** -0.5) for b, h, cols in units]
    v = [v_ref[b, :, cols].astype(BF16) for b, h, cols in units]
    s = [s_ref[b, h] for b, h, cols in units]
    sc = [lax.dot_general(qr[i].astype(BF16), kr[i].astype(BF16), nt, preferred_element_type=F32)
          * dm_ref[units[i][1]] for i in idx]
    cross = [jnp.dot((qr[i] * qd_ref[:, units[i][2]]).astype(BF16), s[i].astype(BF16),
                     preferred_element_type=F32) for i in idx]
    kv = [lax.dot_general((kr[i] * kd_ref[:, units[i][2]]).astype(BF16), v[i], tn,
                          preferred_element_type=F32) for i in idx]
    o = [cross[i] + jnp.dot(sc[i].astype(BF16), v[i], preferred_element_type=F32) for i in idx]
    for i, (b, h, cols) in enumerate(units):
        s_ref[b, h] = chunk_decay[h] * s[i] + kv[i]
        on = o[i] * lax.rsqrt(jnp.mean(o[i] * o[i], axis=-1, keepdims=True) + EPS)
        g = g_ref[b, :, cols]
        y_ref[b, :, cols] = (on * (g * jax.nn.sigmoid(g))).astype(BF16)

    @pl.when(c == n_c - 1)
    def _():
        _own_slot(so_ref, creates)[...] = s_ref[...]


def _ret_tables(cl, pos0, length):
    half = R_HD // 2
    inv = ROPE_BASE ** (-jnp.arange(half, dtype=F32) / half)
    pos = pos0 + jnp.arange(length, dtype=F32)
    ang = pos[:, None] * inv[None, :]
    cos = jnp.cos(ang)
    sin = jnp.sin(ang)
    cos_t = jnp.concatenate([cos, cos], axis=-1)
    sin_t = jnp.concatenate([-sin, sin], axis=-1)
    log_gamma = np.log(1.0 - 2.0 ** (-5.0 - np.arange(R_HEADS, dtype=np.float64)))
    idx = np.arange(cl, dtype=np.float64)
    diff = idx[:, None] - idx[None, :]
    dmat = np.where(diff >= 0, np.exp(np.maximum(diff, 0.0)[None] * log_gamma[:, None, None]), 0.0)
    kdec = np.exp((cl - 1.0 - idx)[:, None] * log_gamma[None, :])
    qdec = np.exp((idx + 1.0)[:, None] * log_gamma[None, :])
    chunk_decay = tuple(float(x) for x in np.exp(cl * log_gamma))
    rep = lambda a: jnp.asarray(np.repeat(a, R_HD, axis=1), F32)
    return cos_t, sin_t, jnp.asarray(dmat, F32), rep(qdec), rep(kdec), chunk_decay


def _stack_slot(stack, per_layer_shape, block, n_inputs):
    prev, slot, depth = stack
    nd = len(per_layer_shape)
    shape = jax.ShapeDtypeStruct((depth,) + tuple(per_layer_shape), F32)
    if prev is None:
        spec = pl.BlockSpec((depth,) + block, lambda b, t: (0, b) + (0,) * (nd - 1))
        return spec, shape, [], [], {}
    spec = pl.BlockSpec((None,) + block, lambda b, t: (slot, b) + (0,) * (nd - 1))
    return spec, shape, [prev], [pl.BlockSpec(memory_space=pl.ANY)], {n_inputs: 1}


def _own_slot(so_ref, creates):
    if creates is None:
        return so_ref
    slot, depth = creates
    for other in range(depth):
        if other != slot:
            so_ref[other] = jnp.zeros(so_ref.shape[1:], F32)
    return so_ref.at[slot]


def _retention(p3d, s0_all, layer, pos0, stack, *, bb):
    bsz, length, _ = p3d.shape
    cl = min(R_CHUNK, length)
    n_c = length // cl
    cos_t, sin_t, dmat, qdec, kdec, chunk_decay = _ret_tables(cl, pos0, length)

    def col(j):
        return pl.BlockSpec((bb, cl, W_GROUP), lambda b, c: (b, c, j))

    tab = pl.BlockSpec((cl, R_HD), lambda b, c: (c, 0))
    state, state_shape, extra, extra_specs, aliases = _stack_slot(
        stack, s0_all.shape[1:], (bb, R_HEADS, R_HD, R_HD), n_inputs=10)
    return pl.pallas_call(
        functools.partial(_ret_body, bb=bb, n_c=n_c, chunk_decay=chunk_decay,
                          creates=None if stack[0] is not None else stack[1:]),
        grid=(bsz // bb, n_c),
        in_specs=[col(COL_R_Q), col(COL_R_K), col(COL_R_V), col(COL_R_G), tab, tab,
                  pl.BlockSpec(dmat.shape, lambda b, c: (0, 0, 0)),
                  pl.BlockSpec(qdec.shape, lambda b, c: (0, 0)),
                  pl.BlockSpec(kdec.shape, lambda b, c: (0, 0)),
                  pl.BlockSpec((None, bb, R_HEADS, R_HD, R_HD), lambda b, c: (layer, b, 0, 0, 0))]
                 + extra_specs,
        out_specs=[pl.BlockSpec((bb, cl, W_GROUP), lambda b, c: (b, c, 0)), state],
        out_shape=[jax.ShapeDtypeStruct((bsz, length, W_GROUP), BF16), state_shape],
        input_output_aliases=aliases,
        scratch_shapes=[pltpu.VMEM((bb, R_HEADS, R_HD, R_HD), F32)],
        compiler_params=_cparams(("parallel", "arbitrary")),
        name="retention",
    )(p3d, p3d, p3d, p3d, cos_t, sin_t, dmat, qdec, kdec, s0_all, *extra)


def _conv_body(bg_ref, cg_ref, h_ref, buf_ref, w_ref, y_ref, st_ref, carry_ref, *, bb, tt):
    @pl.when(pl.program_id(1) == 0)
    def _():
        carry_ref[...] = buf_ref[...]

    shape = (bb, tt, W_GROUP)
    z = cg_ref[...] * h_ref[...]
    z2 = z.reshape(bb * tt, W_GROUP)
    r1 = pltpu.roll(z2, 1, axis=0).reshape(shape)
    r2 = pltpu.roll(z2, 2, axis=0).reshape(shape)
    tpos = lax.broadcasted_iota(jnp.int32, shape, 1)
    c0 = carry_ref[:, 0:1, :]
    c1 = carry_ref[:, 1:2, :]
    zm1 = jnp.where(tpos == 0, c1, r1)
    zm2 = jnp.where(tpos == 0, c0, jnp.where(tpos == 1, c1, r2))
    y = w_ref[0:1, :] * zm2 + w_ref[1:2, :] * zm1 + w_ref[2:3, :] * z
    y_ref[...] = (bg_ref[...] * y).astype(BF16)
    new = cg_ref[:, tt - 2:tt, :] * h_ref[:, tt - 2:tt, :]
    carry_ref[...] = new
    st_ref[...] = new


def _conv(p3d, buf, w, *, bb, tt):
    bsz, length, _ = p3d.shape

    def col(j):
        return pl.BlockSpec((bb, tt, W_GROUP), lambda b, t: (b, t, j))

    state = pl.BlockSpec((bb, C_WIDTH - 1, W_GROUP), lambda b, t: (b, 0, 0))
    return pl.pallas_call(
        functools.partial(_conv_body, bb=bb, tt=tt),
        grid=(bsz // bb, length // tt),
        in_specs=[col(COL_C_B), col(COL_C_C), col(COL_C_H), state,
                  pl.BlockSpec(w.shape, lambda b, t: (0, 0))],
        out_specs=[pl.BlockSpec((bb, tt, W_GROUP), lambda b, t: (b, t, 0)), state],
        out_shape=[jax.ShapeDtypeStruct((bsz, length, W_GROUP), BF16),
                   jax.ShapeDtypeStruct(buf.shape, F32)],
        scratch_shapes=[pltpu.VMEM((bb, C_WIDTH - 1, W_GROUP), F32)],
        compiler_params=_cparams(("parallel", "arbitrary")),
        name="short_conv",
    )(p3d, p3d, p3d, buf, w)


def _group_dot(xb, ones_bd):
    wb = ones_bd.shape[0]
    return jnp.concatenate(
        [jnp.dot(xb[:, i * wb:(i + 1) * wb], ones_bd, preferred_element_type=F32)
         for i in range(W_GROUP // wb)], axis=1)


def _head_sum(x, ones_bd):
    return _group_dot(x.astype(BF16), ones_bd)


def _rwkvc_body(r_ref, k_ref, v_ref, lo_ref, sh_ref, s0_ref,
                mu_ref, mulo_ref, w0_ref, w2_ref, a0_ref, a2_ref, g2_ref, kk_ref, ka_ref, rk_ref,
                lnw_ref, lnb_ref, ones_ref, *rest, bb, tt, n_t, gsz, chunk, creates):
    (y_ref, so_ref,
     s2_ref, carry_ref, carrylo_ref, pt_s, rt_s, qh_s, kh_s, qb_s, kb_s, v_s, ec_s, ys) = rest[-14:]
    tb = pl.program_id(1)
    n = bb * tt
    w3 = (bb, tt, W_GROUP)
    ones_bd = ones_ref[...]
    n_pair = K_HEADS // 2
    pair_w = 2 * K_HD
    zero_blk = jnp.zeros((K_HD, K_HD), F32)

    @pl.when(tb == 0)
    def _():
        for b in range(bb):
            for p in range(n_pair):
                top = jnp.concatenate([s0_ref[b, 2 * p], zero_blk], axis=1)
                bot = jnp.concatenate([zero_blk, s0_ref[b, 2 * p + 1]], axis=1)
                s2_ref[b, p] = jnp.concatenate([top, bot], axis=0)
        carry_ref[...] = sh_ref[:, :, 0:3 * W_GROUP]
        carrylo_ref[...] = sh_ref[:, :, 3 * W_GROUP:K_COLS]

    tpos = lax.broadcasted_iota(jnp.int32, w3, 1)
    tpos_lo = lax.broadcasted_iota(jnp.int32, (bb, tt, LORA_COLS), 1)

    def shifted(x, carry, mu, mask):
        prev = pltpu.roll(x.reshape(n, x.shape[-1]), 1, axis=0).reshape(x.shape)
        prev = jnp.where(mask == 0, carry, prev)
        return (x + (prev - x) * mu).reshape(n, x.shape[-1])

    r_in, k_in, v_in, lo_in = r_ref[...], k_ref[...], v_ref[...], lo_ref[...]
    r = shifted(r_in, carry_ref[:, :, 0:W_GROUP], mu_ref[:, 0:W_GROUP], tpos)
    k = shifted(k_in, carry_ref[:, :, W_GROUP:2 * W_GROUP], mu_ref[:, W_GROUP:2 * W_GROUP], tpos)
    v = shifted(v_in, carry_ref[:, :, 2 * W_GROUP:3 * W_GROUP], mu_ref[:, 2 * W_GROUP:3 * W_GROUP], tpos)
    lo = shifted(lo_in, carrylo_ref[...], mulo_ref[...], tpos_lo)
    carry_ref[:, :, 0:W_GROUP] = r_ref[:, tt - 1:tt, :]
    carry_ref[:, :, W_GROUP:2 * W_GROUP] = k_ref[:, tt - 1:tt, :]
    carry_ref[:, :, 2 * W_GROUP:3 * W_GROUP] = v_ref[:, tt - 1:tt, :]
    carrylo_ref[...] = lo_ref[:, tt - 1:tt, :]

    zw = w0_ref[...] + jnp.dot(jnp.tanh(lo).astype(BF16), w2_ref[...], preferred_element_type=F32)
    logw = -RWKV_DECAY_SCALE * jax.nn.sigmoid(zw)
    a = jax.nn.sigmoid(a0_ref[...] + jnp.dot(lo.astype(BF16), a2_ref[...], preferred_element_type=F32))
    gate = jnp.dot(jax.nn.sigmoid(lo).astype(BF16), g2_ref[...], preferred_element_type=F32)
    kk = k * kk_ref[...]
    kk = kk * lax.rsqrt(jnp.maximum(_head_sum(kk * kk, ones_bd), 1e-24))
    k2 = k * (1.0 + (a - 1.0) * ka_ref[...])
    bonus = _head_sum(r * k2 * rk_ref[...], ones_bd) * v
    q = -(kk * a)

    cpos = lax.broadcasted_iota(jnp.int32, (n, W_GROUP), 0) % chunk
    g = logw
    step = 1
    while step < chunk:
        g = g + jnp.where(cpos >= step, pltpu.roll(g, step, axis=0), 0.0)
        step *= 2
    g3 = g.reshape(n // chunk, chunk, W_GROUP)
    gtot = jnp.broadcast_to(g3[:, chunk - 1:chunk, :], g3.shape).reshape(n, W_GROUP)
    e_neg = jnp.exp(-g)
    e_rem = jnp.exp(gtot - g)
    pt_s[...] = (kk * jnp.exp(g - logw)).reshape(w3)
    rt_s[...] = (r * jnp.exp(g)).reshape(w3)
    qh_s[...] = (q * e_neg).reshape(w3)
    kh_s[...] = (k2 * e_neg).reshape(w3)
    qb_s[...] = (q * e_rem).reshape(w3)
    kb_s[...] = (k2 * e_rem).reshape(w3)
    v_s[...] = v.reshape(w3)
    ec_s[...] = jnp.exp(gtot).reshape(w3)

    c2, c4 = 2 * chunk, 4 * chunk
    even = lax.broadcasted_iota(jnp.int32, (chunk, pair_w), 1) < K_HD
    ri = lax.broadcasted_iota(jnp.int32, (c4, c4), 0)
    ci = lax.broadcasted_iota(jnp.int32, (c4, c4), 1)
    keep = ci % chunk < ri % chunk + ri // c2
    right = lax.broadcasted_iota(jnp.int32, (c2, c4), 1) >= c2
    eye = (lax.broadcasted_iota(jnp.int32, (c2, c2), 0)
           == lax.broadcasted_iota(jnp.int32, (c2, c2), 1)).astype(F32)
    nt = (((1,), (1,)), ((), ()))
    tn = (((0,), (0,)), ((), ()))
    n_double = chunk.bit_length() - 2

    def two(x):
        return jnp.concatenate([jnp.where(even, x, 0.0), jnp.where(even, 0.0, x)], axis=0)

    def mm(x, y):
        return jnp.dot(x.astype(BF16), y.astype(BF16), preferred_element_type=F32)

    def units(args):
        v2, pr, qk, qkb, s2, ec = zip(*args)
        idx = range(len(args))
        apr = [jnp.where(keep, lax.dot_general(pr[i], qk[i], nt, preferred_element_type=F32), 0.0) for i in idx]
        prs = [lax.dot_general(pr[i], s2[i].astype(BF16), nt, preferred_element_type=F32) for i in idx]
        apk = [mm(jnp.where(right, apr[i][0:c2], 0.0), jnp.concatenate([v2[i], v2[i]], axis=0)) for i in idx]
        power = [apr[i][0:c2, 0:c2] for i in idx]
        inv = [eye + power[i] for i in idx]
        for _ in range(n_double):
            power = [mm(power[i], power[i]) for i in idx]
            inv = [inv[i] + mm(inv[i], power[i]) for i in idx]
        u2 = [mm(inv[i], prs[i][0:c2] + apk[i]) for i in idx]
        uv = [jnp.concatenate([u2[i], v2[i]], axis=0).astype(BF16) for i in idx]
        y2 = [prs[i][c2:c4] + jnp.dot(apr[i][c2:c4].astype(BF16), uv[i], preferred_element_type=F32)
              for i in idx]
        s_new = [s2[i] * ec[i] + lax.dot_general(uv[i], qkb[i], tn, preferred_element_type=F32) for i in idx]
        return [(y2[i][0:chunk] + y2[i][chunk:c2], s_new[i]) for i in idx]

    def chunk_step(i, carry):
        c = i // (bb // gsz)
        b0 = (i % (bb // gsz)) * gsz
        r0 = pl.multiple_of(c * chunk, chunk)
        rows = pl.ds(r0, chunk)
        where = [(b0 + j, slice(p * pair_w, (p + 1) * pair_w), p) for j in range(gsz) for p in range(n_pair)]
        args = []
        for b, lanes, p in where:
            ld = lambda ref: two(ref[b, rows, lanes])
            args.append((ld(v_s),
                         jnp.concatenate([ld(pt_s), ld(rt_s)], axis=0).astype(BF16),
                         jnp.concatenate([ld(qh_s), ld(kh_s)], axis=0).astype(BF16),
                         jnp.concatenate([ld(qb_s), ld(kb_s)], axis=0).astype(BF16),
                         s2_ref[b, p], ec_s[b, pl.ds(r0, 1), lanes]))
        for (b, lanes, p), (y, s_new) in zip(where, units(args)):
            ys[b, rows, lanes] = y
            s2_ref[b, p] = s_new
        return carry

    lax.fori_loop(0, (tt // chunk) * (bb // gsz), chunk_step, 0)

    y = ys[...].reshape(n, W_GROUP)
    mean = _head_sum(y, ones_bd) * (1.0 / K_HD)
    yc = y - mean
    var = _head_sum(yc * yc, ones_bd) * (1.0 / K_HD)
    out = (yc * lax.rsqrt(var + GN_EPS) * lnw_ref[...] + lnb_ref[...] + bonus) * gate
    y_ref[...] = out.reshape(w3).astype(BF16)

    @pl.when(tb == n_t - 1)
    def _():
        own = _own_slot(so_ref, creates)
        for b in range(bb):
            for p in range(n_pair):
                own[b, 2 * p] = s2_ref[b, p, 0:K_HD, 0:K_HD]
                own[b, 2 * p + 1] = s2_ref[b, p, K_HD:pair_w, K_HD:pair_w]


def _rwkvc(p3d, shift, s0_all, layer, prm, stack, *, bb, tt, gsz, chunk):
    bsz, length, _ = p3d.shape
    n_t = length // tt

    def col(j):
        return pl.BlockSpec((bb, tt, W_GROUP), lambda b, t: (b, t, j))

    def whole(a):
        nd = a.ndim
        return pl.BlockSpec(a.shape, lambda b, t: (0,) * nd)

    params = [prm[nm] for nm in ("mu", "mu_lo", "w0", "w2", "a0", "a2", "g2", "k_k", "k_a", "r_k",
                                 "ln_w", "ln_b", "ones_bd")]
    state, state_shape, extra, extra_specs, aliases = _stack_slot(
        stack, s0_all.shape[1:], (bb, K_HEADS, K_HD, K_HD), n_inputs=6 + len(params))
    blk = pltpu.VMEM((bb, tt, W_GROUP), F32)
    return pl.pallas_call(
        functools.partial(_rwkvc_body, bb=bb, tt=tt, n_t=n_t, gsz=gsz, chunk=chunk,
                          creates=None if stack[0] is not None else stack[1:]),
        grid=(bsz // bb, n_t),
        in_specs=[col(COL_K_R), col(COL_K_K), col(COL_K_V),
                  pl.BlockSpec((bb, tt, LORA_COLS), lambda b, t: (b, t, COL_K_LORA)),
                  pl.BlockSpec((bb, 1, K_COLS), lambda b, t: (b, 0, 0)),
                  pl.BlockSpec((None, bb, K_HEADS, K_HD, K_HD), lambda b, t: (layer, b, 0, 0, 0))]
                 + [whole(a) for a in params] + extra_specs,
        out_specs=[pl.BlockSpec((bb, tt, W_GROUP), lambda b, t: (b, t, 0)), state],
        out_shape=[jax.ShapeDtypeStruct((bsz, length, W_GROUP), BF16), state_shape],
        input_output_aliases=aliases,
        scratch_shapes=[pltpu.VMEM((bb, K_HEADS // 2, 2 * K_HD, 2 * K_HD), F32),
                        pltpu.VMEM((bb, 1, 3 * W_GROUP), F32),
                        pltpu.VMEM((bb, 1, LORA_COLS), F32),
                        blk, blk, blk, blk, blk, blk, blk, blk, blk],
        compiler_params=_cparams(("parallel", "arbitrary")),
        name="rwkv7c",
    )(p3d, p3d, p3d, p3d, shift, s0_all, *params, *extra)


def _gate_mixing(w_s, b_s, seq):
    cl = min(A_CHUNK, seq)
    wm = jnp.tril(w_s[:, :cl, :cl])
    bias = b_s[:, :cl]
    rep = A_CHUNK // cl
    if rep > 1:
        eye = jnp.eye(rep, dtype=w_s.dtype)
        wm = jnp.einsum("ab,hts->hatbs", eye, wm).reshape(w_s.shape[0], A_CHUNK, A_CHUNK)
        bias = jnp.tile(bias, (1, rep))
    bias = jnp.repeat(bias.T, A_CHUNK, axis=1)
    return wm.astype(BF16), bias


def _pad_rows(w, row0):
    return jnp.zeros((LORA_COLS, W_GROUP), F32).at[row0:row0 + w.shape[0]].set(w).astype(BF16)


def _layer_params(l, ffn1_norm, ffn1_w_gate, ffn1_w_up, ffn1_w_down, mix_norm, w_in, w_out,
                  a_w_s, a_b_s, a_ln_g, a_ln_b, c_conv_w,
                  k_mu, k_w0, k_w2, k_a0, k_a2, k_g2, k_k_k, k_k_a, k_r_k, k_ln_w, k_ln_b,
                  ffn2_norm, ffn2_w_gate, ffn2_w_up, ffn2_w_down):
    row = lambda a: a[l].reshape(1, -1)
    head_of = np.arange(V7X_MXU_DIM) // K_HD
    ones_bd = jnp.asarray(head_of[:, None] == head_of[None, :], BF16)
    rwkv = dict(
        mu=k_mu[l][None, :3 * W_GROUP], mu_lo=k_mu[l][None, 3 * W_GROUP:],
        w0=row(k_w0), w2=_pad_rows(k_w2[l], 0),
        a0=row(k_a0), a2=_pad_rows(k_a2[l], W_LORA),
        g2=_pad_rows(k_g2[l], W_LORA + A_LORA),
        k_k=row(k_k_k), k_a=row(k_k_a), r_k=row(k_r_k), ln_w=row(k_ln_w), ln_b=row(k_ln_b),
        ones_bd=ones_bd)
    return dict(
        layer=l, ffn_layer=l,
        ffn1=(row(ffn1_norm), ffn1_w_gate, ffn1_w_up, ffn1_w_down),
        ffn2=(row(ffn2_norm), ffn2_w_gate, ffn2_w_up, ffn2_w_down),
        mix_norm=row(mix_norm), w_in=w_in, w_out=w_out,
        a_w_s=a_w_s[l], a_b_s=a_b_s[l], a_ln_g=row(a_ln_g), a_ln_b=row(a_ln_b),
        conv_w=c_conv_w[l], rwkv=rwkv)


def _stream_layer(x2d, bsz, length, pos0, ret_s0_all, ret_layer, conv_buf, rw_shift, rw_s0_all, p, fn, stacks, *,
                  final, cfg):
    layer = p["layer"]
    ffn_kw = dict(layer=p["ffn_layer"], tm=cfg["ffn_tm"], tf=cfg["tf"], cast=cfg["ffn_cast"])
    x1 = _ffn(x2d, *p["ffn1"], fn, final=False, **ffn_kw)
    casted = {}
    if cfg["ffn_cast"]:
        x1, casted["ffn1"] = x1
    proj = _proj(x1, p["mix_norm"], p["w_in"], layer=p["ffn_layer"], tm=cfg["proj_tm"], tn=cfg["tn"],
                 cast=cfg["ffn_cast"])
    if cfg["ffn_cast"]:
        proj, casted["w_in"] = proj
    p3d = proj.reshape(bsz, length, proj.shape[1])

    wm, bias = _gate_mixing(p["a_w_s"], p["a_b_s"], length)
    ya, v_rows = _gate(proj, wm, bias, p["a_ln_g"], p["a_ln_b"], tm=cfg["gate_tm"],
                       with_rows=cfg["with_rows"])
    yb, ret_s = _retention(p3d, ret_s0_all, ret_layer, pos0, stacks["ret"], bb=cfg["ret_bb"])
    yc, conv_new = _conv(p3d, conv_buf, p["conv_w"], bb=cfg["conv_bb"], tt=cfg["conv_tt"])
    yd, rw_s = _rwkvc(p3d, rw_shift[:, None, :], rw_s0_all, ret_layer, p["rwkv"], stacks["rwkv"],
                      bb=cfg["rwkv_bb"], tt=cfg["rwkv_tt"], gsz=cfg["rwkv_gsz"], chunk=cfg["rwkv_chunk"])
    shift_new = p3d[:, length - 1, K_COL0:K_COL0 + K_COLS]

    flat = lambda y: y.reshape(bsz * length, W_GROUP)
    x2 = _outproj(x1, ya, flat(yb), flat(yc), flat(yd), p["w_out"], layer=layer, tm=cfg["tm"])
    x3 = _ffn(x2, *p["ffn2"], fn, final=final, **ffn_kw)
    if cfg["ffn_cast"]:
        x3, casted["ffn2"] = x3
    return x3, ret_s, conv_new, shift_new, rw_s, v_rows, casted


def _stream_cfg(bsz, length, sample):
    m = bsz * length
    tm = min(512, m)
    if sample:
        return dict(tm=tm, ffn_tm=m, ffn_cast=True, proj_tm=m, tf=256, tn=640,
                    gate_tm=min(512, m), with_rows=True,
                    ret_bb=8, conv_bb=32, conv_tt=length, rwkv_bb=16, rwkv_tt=length, rwkv_gsz=8,
                    rwkv_chunk=min(RWKV_CHUNK, length))
    return dict(tm=tm, ffn_tm=min(1024, m), ffn_cast=False, proj_tm=min(1024, m), tf=512, tn=1280,
                gate_tm=min(512, m), with_rows=False,
                ret_bb=1, conv_bb=1, conv_tt=min(512, length), rwkv_bb=bsz, rwkv_tt=min(128, length),
                rwkv_gsz=4, rwkv_chunk=min(RWKV_CHUNK, length))


def kernel(x_prompt, x_sample, state_ret, state_conv, state_rwkv_shift, state_rwkv, ffn1_norm, ffn1_w_gate, ffn1_w_up, ffn1_w_down, mix_norm, w_in, w_out, a_w_s, a_b_s, a_ln_g, a_ln_b, c_conv_w, k_mu, k_w0, k_w2, k_a0, k_a2, k_g2, k_k_k, k_k_a, k_r_k, k_ln_w, k_ln_b, ffn2_norm, ffn2_w_gate, ffn2_w_up, ffn2_w_down, final_norm):
    bp, lp, d = x_prompt.shape
    bs, ls, _ = x_sample.shape
    depth = ffn1_norm.shape[0]
    cfg_p = _stream_cfg(bp, lp, sample=False)
    cfg_s = _stream_cfg(bs, ls, sample=True)
    fn = final_norm.reshape(1, d)

    zero_ret = jnp.zeros((1, bp, R_HEADS, R_HD, R_HD), F32)
    zero_conv = jnp.zeros((bp, C_WIDTH - 1, W_GROUP), F32)
    zero_shift = jnp.zeros((bp, K_COLS), F32)
    zero_rw = jnp.zeros((1, bp, K_HEADS, K_HD, K_HD), F32)

    w_out = w_out.astype(BF16)
    xp = x_prompt.reshape(bp * lp, d)
    xs = x_sample.reshape(bs * ls, d)
    outs = [[] for _ in range(5)]
    rp = rs = wp = ws = None
    for l in range(depth):
        p = _layer_params(l, ffn1_norm, ffn1_w_gate, ffn1_w_up, ffn1_w_down, mix_norm, w_in, w_out,
                          a_w_s, a_b_s, a_ln_g, a_ln_b, c_conv_w,
                          k_mu, k_w0, k_w2, k_a0, k_a2, k_g2, k_k_k, k_k_a, k_r_k, k_ln_w, k_ln_b,
                          ffn2_norm, ffn2_w_gate, ffn2_w_up, ffn2_w_down)
        final = l == depth - 1
        xs, rs, cs, ss, ws, vs, w16 = _stream_layer(xs, bs, ls, float(PAST_LEN), state_ret, l, state_conv[l],
                                                    state_rwkv_shift[l], state_rwkv, p, fn,
                                                    dict(ret=(rs, l, depth), rwkv=(ws, l, depth)),
                                                    final=final, cfg=cfg_s)
        p = dict(p, ffn_layer=0, ffn1=p["ffn1"][:1] + w16["ffn1"], ffn2=p["ffn2"][:1] + w16["ffn2"],
                 w_in=w16["w_in"])
        xp, rp, cp, sp, wp, _, _ = _stream_layer(xp, bp, lp, 0.0, zero_ret, 0, zero_conv, zero_shift, zero_rw,
                                                 p, fn, dict(ret=(rp, l, depth), rwkv=(wp, l, depth)),
                                                 final=final, cfg=cfg_p)
        for acc, val in zip(outs, (cp, cs, sp, ss, vs.reshape(bs, ls, W_GROUP))):
            acc.append(val)

    conv_p, conv_s, shift_p, shift_s, v_s = (jnp.stack(o) for o in outs)
    return (xp.reshape(bp, lp, d), xs.reshape(bs, ls, d), rp, rs, conv_p, conv_s, shift_p, shift_s, wp, ws, v_s)
```

```python
import functools

import numpy as np
import jax
import jax.numpy as jnp
from jax import lax
from jax.experimental import pallas as pl
from jax.experimental.pallas import tpu as pltpu

F32 = jnp.float32
BF16 = jnp.bfloat16

W_GROUP = 512
A_CHUNK = 128
R_HEADS = 4
R_HD = 128
R_CHUNK = 128
ROPE_BASE = 10000.0
C_WIDTH = 3
K_HD = 64
K_HEADS = 8
W_LORA = 64
A_LORA = 64
G_LORA = 128
LORA_COLS = W_LORA + A_LORA + G_LORA
K_COLS = 3 * W_GROUP + LORA_COLS
EPS = 1e-6
GN_EPS = 64e-5
PAST_LEN = 16384

COL_A_U, COL_A_V = 0, 1
COL_R_Q, COL_R_K, COL_R_V, COL_R_G = 2, 3, 4, 5
COL_C_B, COL_C_C, COL_C_H = 6, 7, 8
COL_K_R, COL_K_K, COL_K_V = 9, 10, 11
COL_K_LORA = (12 * W_GROUP) // LORA_COLS
K_COL0 = 9 * W_GROUP

V7X_VMEM_LIMIT_BYTES = 62 * 1024 * 1024
V7X_MXU_DIM = 256
RWKV_DECAY_SCALE = float(np.exp(-0.5))
RWKV_CHUNK = 64


def _cparams(sem, vmem=V7X_VMEM_LIMIT_BYTES):
    return pltpu.CompilerParams(dimension_semantics=sem, vmem_limit_bytes=vmem)


def _rms(x, w):
    return x * lax.rsqrt(jnp.mean(x * x, axis=-1, keepdims=True) + EPS) * w


def _ffn_body(x_ref, nw_ref, wg_ref, wu_ref, wd_ref, fn_ref, o_ref, *rest, n_i, n_f, tm, final, cast, x_in_hbm):
    i, j = pl.program_id(0), pl.program_id(1)
    if cast:
        casted, hn_ref = rest[:3], rest[3]
        for src, dst in zip((wg_ref, wu_ref, wd_ref), casted):
            dst[...] = src[...].astype(BF16)
        wg_ref, wu_ref, wd_ref = casted
    else:
        hn_ref = rest[0]

    def start_tile(src_ref):
        x = src_ref[...]
        hn_ref[...] = _rms(x, nw_ref[...]).astype(BF16)
        o_ref[...] = 2.0 * x

    if x_in_hbm:
        xbuf, sem = rest[-2:]

        def x_copy(tile):
            return pltpu.make_async_copy(x_ref.at[pl.ds(tile * tm, tm), :], xbuf, sem)

        @pl.when((i == 0) & (j == 0))
        def _():
            x_copy(0).start()

        @pl.when(j == 0)
        def _():
            x_copy(i).wait()
            start_tile(xbuf)

        @pl.when((j == 1) & (i + 1 < n_i))
        def _():
            x_copy(i + 1).start()
    else:
        @pl.when(j == 0)
        def _():
            start_tile(x_ref)

    h = hn_ref[...]
    tf = wg_ref.shape[1]
    acc = o_ref[...]
    pending = None
    for c0 in range(0, tf, V7X_MXU_DIM):
        c1 = min(c0 + V7X_MXU_DIM, tf)
        g = jnp.dot(h, wg_ref[:, c0:c1], preferred_element_type=F32)
        u = jnp.dot(h, wu_ref[:, c0:c1], preferred_element_type=F32)
        if pending is not None:
            acc = acc + jnp.dot(pending[0], wd_ref[pending[1]:pending[2], :], preferred_element_type=F32)
        pending = ((g * jax.nn.sigmoid(g) * u).astype(BF16), c0, c1)
    o_ref[...] = acc + jnp.dot(pending[0], wd_ref[pending[1]:pending[2], :], preferred_element_type=F32)

    @pl.when(j == n_f - 1)
    def _():
        y = 0.5 * o_ref[...]
        if final:
            y = _rms(y, fn_ref[...])
        o_ref[...] = y


def _ffn(x, nw, wg, wu, wd, fn, *, layer, final, tm, tf, cast=False):
    m, d = x.shape
    f = wg.shape[2]
    n_i, n_f = m // tm, f // tf
    x_in_hbm = n_i > 1
    assert not (cast and x_in_hbm) and n_f >= 2
    x_spec = pl.BlockSpec(memory_space=pl.ANY) if x_in_hbm else pl.BlockSpec((tm, d), lambda i, j: (i, 0))
    scratch = [pltpu.VMEM((tm, d), BF16)]
    if x_in_hbm:
        scratch += [pltpu.VMEM((tm, d), F32), pltpu.SemaphoreType.DMA(())]
    out_specs = [pl.BlockSpec((tm, d), lambda i, j: (i, 0))]
    out_shape = [jax.ShapeDtypeStruct((m, d), F32)]
    if cast:
        assert m == tm, "each weight block must be visited once"
        out_specs += [pl.BlockSpec((None, d, tf), lambda i, j: (0, 0, j)),
                      pl.BlockSpec((None, d, tf), lambda i, j: (0, 0, j)),
                      pl.BlockSpec((None, tf, d), lambda i, j: (0, j, 0))]
        out_shape += [jax.ShapeDtypeStruct((1, d, f), BF16), jax.ShapeDtypeStruct((1, d, f), BF16),
                      jax.ShapeDtypeStruct((1, f, d), BF16)]
    res = pl.pallas_call(
        functools.partial(_ffn_body, n_i=n_i, n_f=n_f, tm=tm, final=final, cast=cast, x_in_hbm=x_in_hbm),
        grid=(n_i, n_f),
        in_specs=[
            x_spec,
            pl.BlockSpec((1, d), lambda i, j: (0, 0)),
            pl.BlockSpec((None, d, tf), lambda i, j: (layer, 0, j)),
            pl.BlockSpec((None, d, tf), lambda i, j: (layer, 0, j)),
            pl.BlockSpec((None, tf, d), lambda i, j: (layer, j, 0)),
            pl.BlockSpec((1, d), lambda i, j: (0, 0)),
        ],
        out_specs=out_specs,
        out_shape=out_shape,
        scratch_shapes=scratch,
        compiler_params=_cparams(("arbitrary", "arbitrary")),
        name="ffn_final" if final else "ffn",
    )(x, nw, wg, wu, wd, fn)
    return (res[0], tuple(res[1:])) if cast else res[0]


def _proj_body(x_ref, nw_ref, w_ref, o_ref, *rest, cast):
    hn_ref = rest[-1]
    if cast:
        rest[0][...] = w_ref[...].astype(BF16)
        w_ref = rest[0]

    @pl.when(pl.program_id(1) == 0)
    def _():
        hn_ref[...] = _rms(x_ref[...], nw_ref[...]).astype(BF16)

    o_ref[...] = jnp.dot(hn_ref[...], w_ref[...], preferred_element_type=F32)


def _proj(x, nw, w, *, layer, tm, tn, cast=False):
    m, d = x.shape
    n = w.shape[2]
    out_specs = [pl.BlockSpec((tm, tn), lambda i, j: (i, j))]
    out_shape = [jax.ShapeDtypeStruct((m, n), F32)]
    if cast:
        assert m == tm, "each weight block must be visited once"
        out_specs.append(pl.BlockSpec((None, d, tn), lambda i, j: (0, 0, j)))
        out_shape.append(jax.ShapeDtypeStruct((1, d, n), BF16))
    res = pl.pallas_call(
        functools.partial(_proj_body, cast=cast),
        grid=(m // tm, n // tn),
        in_specs=[
            pl.BlockSpec((tm, d), lambda i, j: (i, 0)),
            pl.BlockSpec((1, d), lambda i, j: (0, 0)),
            pl.BlockSpec((None, d, tn), lambda i, j: (layer, 0, j)),
        ],
        out_specs=out_specs,
        out_shape=out_shape,
        scratch_shapes=[pltpu.VMEM((tm, d), BF16)],
        compiler_params=_cparams(("parallel", "arbitrary")),
        name="in_proj",
    )(x, nw, w)
    return tuple(res) if cast else res[0]


def _outproj_body(x_ref, ya_ref, yb_ref, yc_ref, yd_ref, w_ref, o_ref):
    acc = x_ref[...]
    for gi, y_ref in enumerate((ya_ref, yb_ref, yc_ref, yd_ref)):
        acc = acc + jnp.dot(y_ref[...], w_ref[gi * W_GROUP:(gi + 1) * W_GROUP, :],
                            preferred_element_type=F32)
    o_ref[...] = acc


def _outproj(x, ya, yb, yc, yd, w, *, layer, tm):
    m, d = x.shape
    yspec = pl.BlockSpec((tm, W_GROUP), lambda i: (i, 0))
    return pl.pallas_call(
        _outproj_body,
        grid=(m // tm,),
        in_specs=[pl.BlockSpec((tm, d), lambda i: (i, 0)), yspec, yspec, yspec, yspec,
                  pl.BlockSpec((None,) + w.shape[1:], lambda i: (layer, 0, 0))],
        out_specs=pl.BlockSpec((tm, d), lambda i: (i, 0)),
        out_shape=jax.ShapeDtypeStruct((m, d), F32),
        compiler_params=_cparams(("parallel",)),
        name="out_proj",
    )(x, ya, yb, yc, yd, w)


def _gate_body(u_ref, v_ref, wm_ref, bias_ref, g_ref, b_ref, y_ref, *vr_ref, tm):
    gu = jax.nn.gelu(u_ref[...], approximate=True)
    gv = jax.nn.gelu(v_ref[...], approximate=True)
    mu = jnp.mean(gv, axis=-1, keepdims=True)
    var = jnp.mean(jnp.square(gv - mu), axis=-1, keepdims=True)
    vn = (gv - mu) * lax.rsqrt(var + EPS) * g_ref[...] + b_ref[...]
    if vr_ref:
        vr_ref[0][...] = vn
    vnb = vn.astype(BF16)
    for c in range(tm // A_CHUNK):
        rows = slice(c * A_CHUNK, (c + 1) * A_CHUNK)
        for h in range(W_GROUP // A_CHUNK):
            cols = slice(h * A_CHUNK, (h + 1) * A_CHUNK)
            z = jnp.dot(wm_ref[h], vnb[rows, cols], preferred_element_type=F32) + bias_ref[:, cols]
            y_ref[rows, cols] = (gu[rows, cols] * z).astype(BF16)


def _gate(p2d, wm, bias, ln_g, ln_b, *, tm, with_rows):
    m = p2d.shape[0]
    row_spec = pl.BlockSpec((tm, W_GROUP), lambda i: (i, 0))
    out_shape = [jax.ShapeDtypeStruct((m, W_GROUP), BF16)]
    out_specs = [row_spec]
    if with_rows:
        out_shape.append(jax.ShapeDtypeStruct((m, W_GROUP), F32))
        out_specs.append(row_spec)
    res = pl.pallas_call(
        functools.partial(_gate_body, tm=tm),
        grid=(m // tm,),
        in_specs=[
            pl.BlockSpec((tm, W_GROUP), lambda i: (i, COL_A_U)),
            pl.BlockSpec((tm, W_GROUP), lambda i: (i, COL_A_V)),
            pl.BlockSpec(wm.shape, lambda i: (0, 0, 0)),
            pl.BlockSpec(bias.shape, lambda i: (0, 0)),
            pl.BlockSpec((1, W_GROUP), lambda i: (0, 0)),
            pl.BlockSpec((1, W_GROUP), lambda i: (0, 0)),
        ],
        out_specs=out_specs,
        out_shape=out_shape,
        compiler_params=_cparams(("parallel",)),
        name="spatial_gate",
    )(p2d, p2d, wm, bias, ln_g, ln_b)
    return res if with_rows else (res[0], None)


def _ret_body(q_ref, k_ref, v_ref, g_ref, cos_ref, sin_ref, dm_ref, qd_ref, kd_ref, s0_ref,
              *rest, bb, n_c, chunk_decay, creates):
    y_ref, so_ref, s_ref = rest[-3:]
    c = pl.program_id(1)

    @pl.when(c == 0)
    def _():
        s_ref[...] = s0_ref[...]

    cos = cos_ref[...]
    sin = sin_ref[...]
    nt = (((1,), (1,)), ((), ()))
    tn = (((0,), (0,)), ((), ()))
    units = [(b, h, slice(h * R_HD, (h + 1) * R_HD)) for b in range(bb) for h in range(R_HEADS)]
    idx = range(len(units))

    def rope(x):
        return x * cos + pltpu.roll(x, R_HD // 2, axis=1) * sin

    qr = [rope(q_ref[b, :, cols]) for b, h, cols in units]
    kr = [rope(k_ref[b, :, cols]) * (R_HD ** -0.5) for b, h, cols in units]
    v = [v_ref[b, :, cols].astype(BF16) for b, h, cols in units]
    s = [s_ref[b, h] for b, h, cols in units]
    sc = [lax.dot_general(qr[i].astype(BF16), kr[i].astype(BF16), nt, preferred_element_type=F32)
          * dm_ref[units[i][1]] for i in idx]
    cross = [jnp.dot((qr[i] * qd_ref[:, units[i][2]]).astype(BF16), s[i].astype(BF16),
                     preferred_element_type=F32) for i in idx]
    kv = [lax.dot_general((kr[i] * kd_ref[:, units[i][2]]).astype(BF16), v[i], tn,
                          preferred_element_type=F32) for i in idx]
    o = [cross[i] + jnp.dot(sc[i].astype(BF16), v[i], preferred_element_type=F32) for i in idx]
    for i, (b, h, cols) in enumerate(units):
        s_ref[b, h] = chunk_decay[h] * s[i] + kv[i]
        on = o[i] * lax.rsqrt(jnp.mean(o[i] * o[i], axis=-1, keepdims=True) + EPS)
        g = g_ref[b, :, cols]
        y_ref[b, :, cols] = (on * (g * jax.nn.sigmoid(g))).astype(BF16)

    @pl.when(c == n_c - 1)
    def _():
        _own_slot(so_ref, creates)[...] = s_ref[...]


def _ret_tables(cl, pos0, length):
    half = R_HD // 2
    inv = ROPE_BASE ** (-jnp.arange(half, dtype=F32) / half)
    pos = pos0 + jnp.arange(length, dtype=F32)
    ang = pos[:, None] * inv[None, :]
    cos = jnp.cos(ang)
    sin = jnp.sin(ang)
    cos_t = jnp.concatenate([cos, cos], axis=-1)
    sin_t = jnp.concatenate([-sin, sin], axis=-1)
    log_gamma = np.log(1.0 - 2.0 ** (-5.0 - np.arange(R_HEADS, dtype=np.float64)))
    idx = np.arange(cl, dtype=np.float64)
    diff = idx[:, None] - idx[None, :]
    dmat = np.where(diff >= 0, np.exp(np.maximum(diff, 0.0)[None] * log_gamma[:, None, None]), 0.0)
    kdec = np.exp((cl - 1.0 - idx)[:, None] * log_gamma[None, :])
    qdec = np.exp((idx + 1.0)[:, None] * log_gamma[None, :])
    chunk_decay = tuple(float(x) for x in np.exp(cl * log_gamma))
    rep = lambda a: jnp.asarray(np.repeat(a, R_HD, axis=1), F32)
    return cos_t, sin_t, jnp.asarray(dmat, F32), rep(qdec), rep(kdec), chunk_decay


def _stack_slot(stack, per_layer_shape, block, n_inputs):
    prev, slot, depth = stack
    nd = len(per_layer_shape)
    shape = jax.ShapeDtypeStruct((depth,) + tuple(per_layer_shape), F32)
    if prev is None:
        spec = pl.BlockSpec((depth,) + block, lambda b, t: (0, b) + (0,) * (nd - 1))
        return spec, shape, [], [], {}
    spec = pl.BlockSpec((None,) + block, lambda b, t: (slot, b) + (0,) * (nd - 1))
    return spec, shape, [prev], [pl.BlockSpec(memory_space=pl.ANY)], {n_inputs: 1}


def _own_slot(so_ref, creates):
    if creates is None:
        return so_ref
    slot, depth = creates
    for other in range(depth):
        if other != slot:
            so_ref[other] = jnp.zeros(so_ref.shape[1:], F32)
    return so_ref.at[slot]


def _retention(p3d, s0_all, layer, pos0, stack, *, bb):
    bsz, length, _ = p3d.shape
    cl = min(R_CHUNK, length)
    n_c = length // cl
    cos_t, sin_t, dmat, qdec, kdec, chunk_decay = _ret_tables(cl, pos0, length)

    def col(j):
        return pl.BlockSpec((bb, cl, W_GROUP), lambda b, c: (b, c, j))

    tab = pl.BlockSpec((cl, R_HD), lambda b, c: (c, 0))
    state, state_shape, extra, extra_specs, aliases = _stack_slot(
        stack, s0_all.shape[1:], (bb, R_HEADS, R_HD, R_HD), n_inputs=10)
    return pl.pallas_call(
        functools.partial(_ret_body, bb=bb, n_c=n_c, chunk_decay=chunk_decay,
                          creates=None if stack[0] is not None else stack[1:]),
        grid=(bsz // bb, n_c),
        in_specs=[col(COL_R_Q), col(COL_R_K), col(COL_R_V), col(COL_R_G), tab, tab,
                  pl.BlockSpec(dmat.shape, lambda b, c: (0, 0, 0)),
                  pl.BlockSpec(qdec.shape, lambda b, c: (0, 0)),
                  pl.BlockSpec(kdec.shape, lambda b, c: (0, 0)),
                  pl.BlockSpec((None, bb, R_HEADS, R_HD, R_HD), lambda b, c: (layer, b, 0, 0, 0))]
                 + extra_specs,
        out_specs=[pl.BlockSpec((bb, cl, W_GROUP), lambda b, c: (b, c, 0)), state],
        out_shape=[jax.ShapeDtypeStruct((bsz, length, W_GROUP), BF16), state_shape],
        input_output_aliases=aliases,
        scratch_shapes=[pltpu.VMEM((bb, R_HEADS, R_HD, R_HD), F32)],
        compiler_params=_cparams(("parallel", "arbitrary")),
        name="retention",
    )(p3d, p3d, p3d, p3d, cos_t, sin_t, dmat, qdec, kdec, s0_all, *extra)


def _conv_body(bg_ref, cg_ref, h_ref, buf_ref, w_ref, y_ref, st_ref, carry_ref, *, bb, tt):
    @pl.when(pl.program_id(1) == 0)
    def _():
        carry_ref[...] = buf_ref[...]

    shape = (bb, tt, W_GROUP)
    z = cg_ref[...] * h_ref[...]
    z2 = z.reshape(bb * tt, W_GROUP)
    r1 = pltpu.roll(z2, 1, axis=0).reshape(shape)
    r2 = pltpu.roll(z2, 2, axis=0).reshape(shape)
    tpos = lax.broadcasted_iota(jnp.int32, shape, 1)
    c0 = carry_ref[:, 0:1, :]
    c1 = carry_ref[:, 1:2, :]
    zm1 = jnp.where(tpos == 0, c1, r1)
    zm2 = jnp.where(tpos == 0, c0, jnp.where(tpos == 1, c1, r2))
    y = w_ref[0:1, :] * zm2 + w_ref[1:2, :] * zm1 + w_ref[2:3, :] * z
    y_ref[...] = (bg_ref[...] * y).astype(BF16)
    new = cg_ref[:, tt - 2:tt, :] * h_ref[:, tt - 2:tt, :]
    carry_ref[...] = new
    st_ref[...] = new


def _conv(p3d, buf, w, *, bb, tt):
    bsz, length, _ = p3d.shape

    def col(j):
        return pl.BlockSpec((bb, tt, W_GROUP), lambda b, t: (b, t, j))

    state = pl.BlockSpec((bb, C_WIDTH - 1, W_GROUP), lambda b, t: (b, 0, 0))
    return pl.pallas_call(
        functools.partial(_conv_body, bb=bb, tt=tt),
        grid=(bsz // bb, length // tt),
        in_specs=[col(COL_C_B), col(COL_C_C), col(COL_C_H), state,
                  pl.BlockSpec(w.shape, lambda b, t: (0, 0))],
        out_specs=[pl.BlockSpec((bb, tt, W_GROUP), lambda b, t: (b, t, 0)), state],
        out_shape=[jax.ShapeDtypeStruct((bsz, length, W_GROUP), BF16),
                   jax.ShapeDtypeStruct(buf.shape, F32)],
        scratch_shapes=[pltpu.VMEM((bb, C_WIDTH - 1, W_GROUP), F32)],
        compiler_params=_cparams(("parallel", "arbitrary")),
        name="short_conv",
    )(p3d, p3d, p3d, buf, w)


def _group_dot(xb, ones_bd):
    wb = ones_bd.shape[0]
    return jnp.concatenate(
        [jnp.dot(xb[:, i * wb:(i + 1) * wb], ones_bd, preferred_element_type=F32)
         for i in range(W_GROUP // wb)], axis=1)


def _head_sum(x, ones_bd):
    return _group_dot(x.astype(BF16), ones_bd)


def _rwkvc_body(r_ref, k_ref, v_ref, lo_ref, sh_ref, s0_ref,
                mu_ref, mulo_ref, w0_ref, w2_ref, a0_ref, a2_ref, g2_ref, kk_ref, ka_ref, rk_ref,
                lnw_ref, lnb_ref, ones_ref, *rest, bb, tt, n_t, gsz, chunk, creates):
    (y_ref, so_ref,
     s2_ref, carry_ref, carrylo_ref, pt_s, rt_s, qh_s, kh_s, qb_s, kb_s, v_s, ec_s, ys) = rest[-14:]
    tb = pl.program_id(1)
    n = bb * tt
    w3 = (bb, tt, W_GROUP)
    ones_bd = ones_ref[...]
    n_pair = K_HEADS // 2
    pair_w = 2 * K_HD
    zero_blk = jnp.zeros((K_HD, K_HD), F32)

    @pl.when(tb == 0)
    def _():
        for b in range(bb):
            for p in range(n_pair):
                top = jnp.concatenate([s0_ref[b, 2 * p], zero_blk], axis=1)
                bot = jnp.concatenate([zero_blk, s0_ref[b, 2 * p + 1]], axis=1)
                s2_ref[b, p] = jnp.concatenate([top, bot], axis=0)
        carry_ref[...] = sh_ref[:, :, 0:3 * W_GROUP]
        carrylo_ref[...] = sh_ref[:, :, 3 * W_GROUP:K_COLS]

    tpos = lax.broadcasted_iota(jnp.int32, w3, 1)
    tpos_lo = lax.broadcasted_iota(jnp.int32, (bb, tt, LORA_COLS), 1)

    def shifted(x, carry, mu, mask):
        prev = pltpu.roll(x.reshape(n, x.shape[-1]), 1, axis=0).reshape(x.shape)
        prev = jnp.where(mask == 0, carry, prev)
        return (x + (prev - x) * mu).reshape(n, x.shape[-1])

    r_in, k_in, v_in, lo_in = r_ref[...], k_ref[...], v_ref[...], lo_ref[...]
    r = shifted(r_in, carry_ref[:, :, 0:W_GROUP], mu_ref[:, 0:W_GROUP], tpos)
    k = shifted(k_in, carry_ref[:, :, W_GROUP:2 * W_GROUP], mu_ref[:, W_GROUP:2 * W_GROUP], tpos)
    v = shifted(v_in, carry_ref[:, :, 2 * W_GROUP:3 * W_GROUP], mu_ref[:, 2 * W_GROUP:3 * W_GROUP], tpos)
    lo = shifted(lo_in, carrylo_ref[...], mulo_ref[...], tpos_lo)
    carry_ref[:, :, 0:W_GROUP] = r_ref[:, tt - 1:tt, :]
    carry_ref[:, :, W_GROUP:2 * W_GROUP] = k_ref[:, tt - 1:tt, :]
    carry_ref[:, :, 2 * W_GROUP:3 * W_GROUP] = v_ref[:, tt - 1:tt, :]
    carrylo_ref[...] = lo_ref[:, tt - 1:tt, :]

    zw = w0_ref[...] + jnp.dot(jnp.tanh(lo).astype(BF16), w2_ref[...], preferred_element_type=F32)
    logw = -RWKV_DECAY_SCALE * jax.nn.sigmoid(zw)
    a = jax.nn.sigmoid(a0_ref[...] + jnp.dot(lo.astype(BF16), a2_ref[...], preferred_element_type=F32))
    gate = jnp.dot(jax.nn.sigmoid(lo).astype(BF16), g2_ref[...], preferred_element_type=F32)
    kk = k * kk_ref[...]
    kk = kk * lax.rsqrt(jnp.maximum(_head_sum(kk * kk, ones_bd), 1e-24))
    k2 = k * (1.0 + (a - 1.0) * ka_ref[...])
    bonus = _head_sum(r * k2 * rk_ref[...], ones_bd) * v
    q = -(kk * a)

    cpos = lax.broadcasted_iota(jnp.int32, (n, W_GROUP), 0) % chunk
    g = logw
    step = 1
    while step < chunk:
        g = g + jnp.where(cpos >= step, pltpu.roll(g, step, axis=0), 0.0)
        step *= 2
    g3 = g.reshape(n // chunk, chunk, W_GROUP)
    gtot = jnp.broadcast_to(g3[:, chunk - 1:chunk, :], g3.shape).reshape(n, W_GROUP)
    e_neg = jnp.exp(-g)
    e_rem = jnp.exp(gtot - g)
    pt_s[...] = (kk * jnp.exp(g - logw)).reshape(w3)
    rt_s[...] = (r * jnp.exp(g)).reshape(w3)
    qh_s[...] = (q * e_neg).reshape(w3)
    kh_s[...] = (k2 * e_neg).reshape(w3)
    qb_s[...] = (q * e_rem).reshape(w3)
    kb_s[...] = (k2 * e_rem).reshape(w3)
    v_s[...] = v.reshape(w3)
    ec_s[...] = jnp.exp(gtot).reshape(w3)

    c2, c4 = 2 * chunk, 4 * chunk
    even = lax.broadcasted_iota(jnp.int32, (chunk, pair_w), 1) < K_HD
    ri = lax.broadcasted_iota(jnp.int32, (c4, c4), 0)
    ci = lax.broadcasted_iota(jnp.int32, (c4, c4), 1)
    keep = ci % chunk < ri % chunk + ri // c2
    right = lax.broadcasted_iota(jnp.int32, (c2, c4), 1) >= c2
    eye = (lax.broadcasted_iota(jnp.int32, (c2, c2), 0)
           == lax.broadcasted_iota(jnp.int32, (c2, c2), 1)).astype(F32)
    nt = (((1,), (1,)), ((), ()))
    tn = (((0,), (0,)), ((), ()))
    n_double = chunk.bit_length() - 2

    def two(x):
        return jnp.concatenate([jnp.where(even, x, 0.0), jnp.where(even, 0.0, x)], axis=0)

    def mm(x, y):
        return jnp.dot(x.astype(BF16), y.astype(BF16), preferred_element_type=F32)

    def units(args):
        v2, pr, qk, qkb, s2, ec = zip(*args)
        idx = range(len(args))
        apr = [jnp.where(keep, lax.dot_general(pr[i], qk[i], nt, preferred_element_type=F32), 0.0) for i in idx]
        prs = [lax.dot_general(pr[i], s2[i].astype(BF16), nt, preferred_element_type=F32) for i in idx]
        apk = [mm(jnp.where(right, apr[i][0:c2], 0.0), jnp.concatenate([v2[i], v2[i]], axis=0)) for i in idx]
        power = [apr[i][0:c2, 0:c2] for i in idx]
        inv = [eye + power[i] for i in idx]
        for _ in range(n_double):
            power = [mm(power[i], power[i]) for i in idx]
            inv = [inv[i] + mm(inv[i], power[i]) for i in idx]
        u2 = [mm(inv[i], prs[i][0:c2] + apk[i]) for i in idx]
        uv = [jnp.concatenate([u2[i], v2[i]], axis=0).astype(BF16) for i in idx]
        y2 = [prs[i][c2:c4] + jnp.dot(apr[i][c2:c4].astype(BF16), uv[i], preferred_element_type=F32)
              for i in idx]
        s_new = [s2[i] * ec[i] + lax.dot_general(uv[i], qkb[i], tn, preferred_element_type=F32) for i in idx]
        return [(y2[i][0:chunk] + y2[i][chunk:c2], s_new[i]) for i in idx]

    def chunk_step(i, carry):
        c = i // (bb // gsz)
        b0 = (i % (bb // gsz)) * gsz
        r0 = pl.multiple_of(c * chunk, chunk)
        rows = pl.ds(r0, chunk)
        where = [(b0 + j, slice(p * pair_w, (p + 1) * pair_w), p) for j in range(gsz) for p in range(n_pair)]
        args = []
        for b, lanes, p in where:
            ld = lambda ref: two(ref[b, rows, lanes])
            args.append((ld(v_s),
                         jnp.concatenate([ld(pt_s), ld(rt_s)], axis=0).astype(BF16),
                         jnp.concatenate([ld(qh_s), ld(kh_s)], axis=0).astype(BF16),
                         jnp.concatenate([ld(qb_s), ld(kb_s)], axis=0).astype(BF16),
                         s2_ref[b, p], ec_s[b, pl.ds(r0, 1), lanes]))
        for (b, lanes, p), (y, s_new) in zip(where, units(args)):
            ys[b, rows, lanes] = y
            s2_ref[b, p] = s_new
        return carry

    lax.fori_loop(0, (tt // chunk) * (bb // gsz), chunk_step, 0)

    y = ys[...].reshape(n, W_GROUP)
    mean = _head_sum(y, ones_bd) * (1.0 / K_HD)
    yc = y - mean
    var = _head_sum(yc * yc, ones_bd) * (1.0 / K_HD)
    out = (yc * lax.rsqrt(var + GN_EPS) * lnw_ref[...] + lnb_ref[...] + bonus) * gate
    y_ref[...] = out.reshape(w3).astype(BF16)

    @pl.when(tb == n_t - 1)
    def _():
        own = _own_slot(so_ref, creates)
        for b in range(bb):
            for p in range(n_pair):
                own[b, 2 * p] = s2_ref[b, p, 0:K_HD, 0:K_HD]
                own[b, 2 * p + 1] = s2_ref[b, p, K_HD:pair_w, K_HD:pair_w]


def _rwkvc(p3d, shift, s0_all, layer, prm, stack, *, bb, tt, gsz, chunk):
    bsz, length, _ = p3d.shape
    n_t = length // tt

    def col(j):
        return pl.BlockSpec((bb, tt, W_GROUP), lambda b, t: (b, t, j))

    def whole(a):
        nd = a.ndim
        return pl.BlockSpec(a.shape, lambda b, t: (0,) * nd)

    params = [prm[nm] for nm in ("mu", "mu_lo", "w0", "w2", "a0", "a2", "g2", "k_k", "k_a", "r_k",
                                 "ln_w", "ln_b", "ones_bd")]
    state, state_shape, extra, extra_specs, aliases = _stack_slot(
        stack, s0_all.shape[1:], (bb, K_HEADS, K_HD, K_HD), n_inputs=6 + len(params))
    blk = pltpu.VMEM((bb, tt, W_GROUP), F32)
    return pl.pallas_call(
        functools.partial(_rwkvc_body, bb=bb, tt=tt, n_t=n_t, gsz=gsz, chunk=chunk,
                          creates=None if stack[0] is not None else stack[1:]),
        grid=(bsz // bb, n_t),
        in_specs=[col(COL_K_R), col(COL_K_K), col(COL_K_V),
                  pl.BlockSpec((bb, tt, LORA_COLS), lambda b, t: (b, t, COL_K_LORA)),
                  pl.BlockSpec((bb, 1, K_COLS), lambda b, t: (b, 0, 0)),
                  pl.BlockSpec((None, bb, K_HEADS, K_HD, K_HD), lambda b, t: (layer, b, 0, 0, 0))]
                 + [whole(a) for a in params] + extra_specs,
        out_specs=[pl.BlockSpec((bb, tt, W_GROUP), lambda b, t: (b, t, 0)), state],
        out_shape=[jax.ShapeDtypeStruct((bsz, length, W_GROUP), BF16), state_shape],
        input_output_aliases=aliases,
        scratch_shapes=[pltpu.VMEM((bb, K_HEADS // 2, 2 * K_HD, 2 * K_HD), F32),
                        pltpu.VMEM((bb, 1, 3 * W_GROUP), F32),
                        pltpu.VMEM((bb, 1, LORA_COLS), F32),
                        blk, blk, blk, blk, blk, blk, blk, blk, blk],
        compiler_params=_cparams(("parallel", "arbitrary")),
        name="rwkv7c",
    )(p3d, p3d, p3d, p3d, shift, s0_all, *params, *extra)


def _gate_mixing(w_s, b_s, seq):
    cl = min(A_CHUNK, seq)
    wm = jnp.tril(w_s[:, :cl, :cl])
    bias = b_s[:, :cl]
    rep = A_CHUNK // cl
    if rep > 1:
        eye = jnp.eye(rep, dtype=w_s.dtype)
        wm = jnp.einsum("ab,hts->hatbs", eye, wm).reshape(w_s.shape[0], A_CHUNK, A_CHUNK)
        bias = jnp.tile(bias, (1, rep))
    bias = jnp.repeat(bias.T, A_CHUNK, axis=1)
    return wm.astype(BF16), bias


def _pad_rows(w, row0):
    return jnp.zeros((LORA_COLS, W_GROUP), F32).at[row0:row0 + w.shape[0]].set(w).astype(BF16)


def _layer_params(l, ffn1_norm, ffn1_w_gate, ffn1_w_up, ffn1_w_down, mix_norm, w_in, w_out,
                  a_w_s, a_b_s, a_ln_g, a_ln_b, c_conv_w,
                  k_mu, k_w0, k_w2, k_a0, k_a2, k_g2, k_k_k, k_k_a, k_r_k, k_ln_w, k_ln_b,
                  ffn2_norm, ffn2_w_gate, ffn2_w_up, ffn2_w_down):
    row = lambda a: a[l].reshape(1, -1)
    head_of = np.arange(V7X_MXU_DIM) // K_HD
    ones_bd = jnp.asarray(head_of[:, None] == head_of[None, :], BF16)
    rwkv = dict(
        mu=k_mu[l][None, :3 * W_GROUP], mu_lo=k_mu[l][None, 3 * W_GROUP:],
        w0=row(k_w0), w2=_pad_rows(k_w2[l], 0),
        a0=row(k_a0), a2=_pad_rows(k_a2[l], W_LORA),
        g2=_pad_rows(k_g2[l], W_LORA + A_LORA),
        k_k=row(k_k_k), k_a=row(k_k_a), r_k=row(k_r_k), ln_w=row(k_ln_w), ln_b=row(k_ln_b),
        ones_bd=ones_bd)
    return dict(
        layer=l, ffn_layer=l,
        ffn1=(row(ffn1_norm), ffn1_w_gate, ffn1_w_up, ffn1_w_down),
        ffn2=(row(ffn2_norm), ffn2_w_gate, ffn2_w_up, ffn2_w_down),
        mix_norm=row(mix_norm), w_in=w_in, w_out=w_out,
        a_w_s=a_w_s[l], a_b_s=a_b_s[l], a_ln_g=row(a_ln_g), a_ln_b=row(a_ln_b),
        conv_w=c_conv_w[l], rwkv=rwkv)


def _stream_layer(x2d, bsz, length, pos0, ret_s0_all, ret_layer, conv_buf, rw_shift, rw_s0_all, p, fn, stacks, *,
                  final, cfg):
    layer = p["layer"]
    ffn_kw = dict(layer=p["ffn_layer"], tm=cfg["ffn_tm"], tf=cfg["tf"], cast=cfg["ffn_cast"])
    x1 = _ffn(x2d, *p["ffn1"], fn, final=False, **ffn_kw)
    casted = {}
    if cfg["ffn_cast"]:
        x1, casted["ffn1"] = x1
    proj = _proj(x1, p["mix_norm"], p["w_in"], layer=p["ffn_layer"], tm=cfg["proj_tm"], tn=cfg["tn"],
                 cast=cfg["ffn_cast"])
    if cfg["ffn_cast"]:
        proj, casted["w_in"] = proj
    p3d = proj.reshape(bsz, length, proj.shape[1])

    wm, bias = _gate_mixing(p["a_w_s"], p["a_b_s"], length)
    ya, v_rows = _gate(proj, wm, bias, p["a_ln_g"], p["a_ln_b"], tm=cfg["gate_tm"],
                       with_rows=cfg["with_rows"])
    yb, ret_s = _retention(p3d, ret_s0_all, ret_layer, pos0, stacks["ret"], bb=cfg["ret_bb"])
    yc, conv_new = _conv(p3d, conv_buf, p["conv_w"], bb=cfg["conv_bb"], tt=cfg["conv_tt"])
    yd, rw_s = _rwkvc(p3d, rw_shift[:, None, :], rw_s0_all, ret_layer, p["rwkv"], stacks["rwkv"],
                      bb=cfg["rwkv_bb"], tt=cfg["rwkv_tt"], gsz=cfg["rwkv_gsz"], chunk=cfg["rwkv_chunk"])
    shift_new = p3d[:, length - 1, K_COL0:K_COL0 + K_COLS]

    flat = lambda y: y.reshape(bsz * length, W_GROUP)
    x2 = _outproj(x1, ya, flat(yb), flat(yc), flat(yd), p["w_out"], layer=layer, tm=cfg["tm"])
    x3 = _ffn(x2, *p["ffn2"], fn, final=final, **ffn_kw)
    if cfg["ffn_cast"]:
        x3, casted["ffn2"] = x3
    return x3, ret_s, conv_new, shift_new, rw_s, v_rows, casted


def _stream_cfg(bsz, length, sample):
    m = bsz * length
    tm = min(512, m)
    if sample:
        return dict(tm=tm, ffn_tm=m, ffn_cast=True, proj_tm=m, tf=256, tn=640,
                    gate_tm=min(512, m), with_rows=True,
                    ret_bb=8, conv_bb=32, conv_tt=length, rwkv_bb=16, rwkv_tt=length, rwkv_gsz=8,
                    rwkv_chunk=min(RWKV_CHUNK, length))
    return dict(tm=tm, ffn_tm=min(1024, m), ffn_cast=False, proj_tm=min(1024, m), tf=512, tn=1280,
                gate_tm=min(512, m), with_rows=False,
                ret_bb=1, conv_bb=1, conv_tt=min(512, length), rwkv_bb=bsz, rwkv_tt=min(128, length),
                rwkv_gsz=4, rwkv_chunk=min(RWKV_CHUNK, length))


def kernel(x_prompt, x_sample, state_ret, state_conv, state_rwkv_shift, state_rwkv, ffn1_norm, ffn1_w_gate, ffn1_w_up, ffn1_w_down, mix_norm, w_in, w_out, a_w_s, a_b_s, a_ln_g, a_ln_b, c_conv_w, k_mu, k_w0, k_w2, k_a0, k_a2, k_g2, k_k_k, k_k_a, k_r_k, k_ln_w, k_ln_b, ffn2_norm, ffn2_w_gate, ffn2_w_up, ffn2_w_down, final_norm):
    bp, lp, d = x_prompt.shape
    bs, ls, _ = x_sample.shape
    depth = ffn1_norm.shape[0]
    cfg_p = _stream_cfg(bp, lp, sample=False)
    cfg_s = _stream_cfg(bs, ls, sample=True)
    fn = final_norm.reshape(1, d)

    zero_ret = jnp.zeros((1, bp, R_HEADS, R_HD, R_HD), F32)
    zero_conv = jnp.zeros((bp, C_WIDTH - 1, W_GROUP), F32)
    zero_shift = jnp.zeros((bp, K_COLS), F32)
    zero_rw = jnp.zeros((1, bp, K_HEADS, K_HD, K_HD), F32)

    w_out = w_out.astype(BF16)
    xp = x_prompt.reshape(bp * lp, d)
    xs = x_sample.reshape(bs * ls, d)
    outs = [[] for _ in range(5)]
    rp = rs = wp = ws = None
    for l in range(depth):
        p = _layer_params(l, ffn1_norm, ffn1_w_gate, ffn1_w_up, ffn1_w_down, mix_norm, w_in, w_out,
                          a_w_s, a_b_s, a_ln_g, a_ln_b, c_conv_w,
                          k_mu, k_w0, k_w2, k_a0, k_a2, k_g2, k_k_k, k_k_a, k_r_k, k_ln_w, k_ln_b,
                          ffn2_norm, ffn2_w_gate, ffn2_w_up, ffn2_w_down)
        final = l == depth - 1
        xs, rs, cs, ss, ws, vs, w16 = _stream_layer(xs, bs, ls, float(PAST_LEN), state_ret, l, state_conv[l],
                                                    state_rwkv_shift[l], state_rwkv, p, fn,
                                                    dict(ret=(rs, l, depth), rwkv=(ws, l, depth)),
                                                    final=final, cfg=cfg_s)
        p = dict(p, ffn_layer=0, ffn1=p["ffn1"][:1] + w16["ffn1"], ffn2=p["ffn2"][:1] + w16["ffn2"],
                 w_in=w16["w_in"])
        xp, rp, cp, sp, wp, _, _ = _stream_layer(xp, bp, lp, 0.0, zero_ret, 0, zero_conv, zero_shift, zero_rw,
                                                 p, fn, dict(ret=(rp, l, depth), rwkv=(wp, l, depth)),
                                                 final=final, cfg=cfg_p)
        for acc, val in zip(outs, (cp, cs, sp, ss, vs.reshape(bs, ls, W_GROUP))):
            acc.append(val)

    conv_p, conv_s, shift_p, shift_s, v_s = (jnp.stack(o) for o in outs)
    return (xp.reshape(bp, lp, d), xs.reshape(bs, ls, d), rp, rs, conv_p, conv_s, shift_p, shift_s, wp, ws, v_s)
```

```python
import functools

import numpy as np
import jax
import jax.numpy as jnp
from jax import lax
from jax.experimental import pallas as pl
from jax.experimental.pallas import tpu as pltpu

F32 = jnp.float32
BF16 = jnp.bfloat16

W_GROUP = 512
A_CHUNK = 128
R_HEADS = 4
R_HD = 128
R_CHUNK = 128
ROPE_BASE = 10000.0
C_WIDTH = 3
K_HD = 64
K_HEADS = 8
W_LORA = 64
A_LORA = 64
G_LORA = 128
LORA_COLS = W_LORA + A_LORA + G_LORA
K_COLS = 3 * W_GROUP + LORA_COLS
EPS = 1e-6
GN_EPS = 64e-5
PAST_LEN = 16384

COL_A_U, COL_A_V = 0, 1
COL_R_Q, COL_R_K, COL_R_V, COL_R_G = 2, 3, 4, 5
COL_C_B, COL_C_C, COL_C_H = 6, 7, 8
COL_K_R, COL_K_K, COL_K_V = 9, 10, 11
COL_K_LORA = (12 * W_GROUP) // LORA_COLS
K_COL0 = 9 * W_GROUP

V7X_VMEM_LIMIT_BYTES = 62 * 1024 * 1024
V7X_MXU_DIM = 256
RWKV_DECAY_SCALE = float(np.exp(-0.5))
RWKV_CHUNK = 64


def _cparams(sem, vmem=V7X_VMEM_LIMIT_BYTES):
    return pltpu.CompilerParams(dimension_semantics=sem, vmem_limit_bytes=vmem)


def _rms(x, w):
    return x * lax.rsqrt(jnp.mean(x * x, axis=-1, keepdims=True) + EPS) * w


def _ffn_body(x_ref, nw_ref, wg_ref, wu_ref, wd_ref, fn_ref, o_ref, *rest, n_i, n_f, tm, final, cast, x_in_hbm):
    i, j = pl.program_id(0), pl.program_id(1)
    if cast:
        casted, hn_ref = rest[:3], rest[3]
        for src, dst in zip((wg_ref, wu_ref, wd_ref), casted):
            dst[...] = src[...].astype(BF16)
        wg_ref, wu_ref, wd_ref = casted
    else:
        hn_ref = rest[0]

    def start_tile(src_ref):
        x = src_ref[...]
        hn_ref[...] = _rms(x, nw_ref[...]).astype(BF16)
        o_ref[...] = 2.0 * x

    if x_in_hbm:
        xbuf, sem = rest[-2:]

        def x_copy(tile):
            return pltpu.make_async_copy(x_ref.at[pl.ds(tile * tm, tm), :], xbuf, sem)

        @pl.when((i == 0) & (j == 0))
        def _():
            x_copy(0).start()

        @pl.when(j == 0)
        def _():
            x_copy(i).wait()
            start_tile(xbuf)

        @pl.when((j == 1) & (i + 1 < n_i))
        def _():
            x_copy(i + 1).start()
    else:
        @pl.when(j == 0)
        def _():
            start_tile(x_ref)

    h = hn_ref[...]
    tf = wg_ref.shape[1]
    acc = o_ref[...]
    pending = None
    for c0 in range(0, tf, V7X_MXU_DIM):
        c1 = min(c0 + V7X_MXU_DIM, tf)
        g = jnp.dot(h, wg_ref[:, c0:c1], preferred_element_type=F32)
        u = jnp.dot(h, wu_ref[:, c0:c1], preferred_element_type=F32)
        if pending is not None:
            acc = acc + jnp.dot(pending[0], wd_ref[pending[1]:pending[2], :], preferred_element_type=F32)
        pending = ((g * jax.nn.sigmoid(g) * u).astype(BF16), c0, c1)
    o_ref[...] = acc + jnp.dot(pending[0], wd_ref[pending[1]:pending[2], :], preferred_element_type=F32)

    @pl.when(j == n_f - 1)
    def _():
        y = 0.5 * o_ref[...]
        if final:
            y = _rms(y, fn_ref[...])
        o_ref[...] = y


def _ffn(x, nw, wg, wu, wd, fn, *, layer, final, tm, tf, cast=False):
    m, d = x.shape
    f = wg.shape[2]
    n_i, n_f = m // tm, f // tf
    x_in_hbm = n_i > 1
    assert not (cast and x_in_hbm) and n_f >= 2
    x_spec = pl.BlockSpec(memory_space=pl.ANY) if x_in_hbm else pl.BlockSpec((tm, d), lambda i, j: (i, 0))
    scratch = [pltpu.VMEM((tm, d), BF16)]
    if x_in_hbm:
        scratch += [pltpu.VMEM((tm, d), F32), pltpu.SemaphoreType.DMA(())]
    out_specs = [pl.BlockSpec((tm, d), lambda i, j: (i, 0))]
    out_shape = [jax.ShapeDtypeStruct((m, d), F32)]
    if cast:
        assert m == tm, "each weight block must be visited once"
        out_specs += [pl.BlockSpec((None, d, tf), lambda i, j: (0, 0, j)),
                      pl.BlockSpec((None, d, tf), lambda i, j: (0, 0, j)),
                      pl.BlockSpec((None, tf, d), lambda i, j: (0, j, 0))]
        out_shape += [jax.ShapeDtypeStruct((1, d, f), BF16), jax.ShapeDtypeStruct((1, d, f), BF16),
                      jax.ShapeDtypeStruct((1, f, d), BF16)]
    res = pl.pallas_call(
        functools.partial(_ffn_body, n_i=n_i, n_f=n_f, tm=tm, final=final, cast=cast, x_in_hbm=x_in_hbm),
        grid=(n_i, n_f),
        in_specs=[
            x_spec,
            pl.BlockSpec((1, d), lambda i, j: (0, 0)),
            pl.BlockSpec((None, d, tf), lambda i, j: (layer, 0, j)),
            pl.BlockSpec((None, d, tf), lambda i, j: (layer, 0, j)),
            pl.BlockSpec((None, tf, d), lambda i, j: (layer, j, 0)),
            pl.BlockSpec((1, d), lambda i, j: (0, 0)),
        ],
        out_specs=out_specs,
        out_shape=out_shape,
        scratch_shapes=scratch,
        compiler_params=_cparams(("arbitrary", "arbitrary")),
        name="ffn_final" if final else "ffn",
    )(x, nw, wg, wu, wd, fn)
    return (res[0], tuple(res[1:])) if cast else res[0]


def _proj_body(x_ref, nw_ref, w_ref, o_ref, *rest, cast):
    hn_ref = rest[-1]
    if cast:
        rest[0][...] = w_ref[...].astype(BF16)
        w_ref = rest[0]

    @pl.when(pl.program_id(1) == 0)
    def _():
        hn_ref[...] = _rms(x_ref[...], nw_ref[...]).astype(BF16)

    o_ref[...] = jnp.dot(hn_ref[...], w_ref[...], preferred_element_type=F32)


def _proj(x, nw, w, *, layer, tm, tn, cast=False):
    m, d = x.shape
    n = w.shape[2]
    out_specs = [pl.BlockSpec((tm, tn), lambda i, j: (i, j))]
    out_shape = [jax.ShapeDtypeStruct((m, n), F32)]
    if cast:
        assert m == tm, "each weight block must be visited once"
        out_specs.append(pl.BlockSpec((None, d, tn), lambda i, j: (0, 0, j)))
        out_shape.append(jax.ShapeDtypeStruct((1, d, n), BF16))
    res = pl.pallas_call(
        functools.partial(_proj_body, cast=cast),
        grid=(m // tm, n // tn),
        in_specs=[
            pl.BlockSpec((tm, d), lambda i, j: (i, 0)),
            pl.BlockSpec((1, d), lambda i, j: (0, 0)),
            pl.BlockSpec((None, d, tn), lambda i, j: (layer, 0, j)),
        ],
        out_specs=out_specs,
        out_shape=out_shape,
        scratch_shapes=[pltpu.VMEM((tm, d), BF16)],
        compiler_params=_cparams(("parallel", "arbitrary")),
        name="in_proj",
    )(x, nw, w)
    return tuple(res) if cast else res[0]


def _outproj_body(x_ref, ya_ref, yb_ref, yc_ref, yd_ref, w_ref, o_ref):
    acc = x_ref[...]
    for gi, y_ref in enumerate((ya_ref, yb_ref, yc_ref, yd_ref)):
        acc = acc + jnp.dot(y_ref[...], w_ref[gi * W_GROUP:(gi + 1) * W_GROUP, :],
                            preferred_element_type=F32)
    o_ref[...] = acc


def _outproj(x, ya, yb, yc, yd, w, *, layer, tm):
    m, d = x.shape
    yspec = pl.BlockSpec((tm, W_GROUP), lambda i: (i, 0))
    return pl.pallas_call(
        _outproj_body,
        grid=(m // tm,),
        in_specs=[pl.BlockSpec((tm, d), lambda i: (i, 0)), yspec, yspec, yspec, yspec,
                  pl.BlockSpec((None,) + w.shape[1:], lambda i: (layer, 0, 0))],
        out_specs=pl.BlockSpec((tm, d), lambda i: (i, 0)),
        out_shape=jax.ShapeDtypeStruct((m, d), F32),
        compiler_params=_cparams(("parallel",)),
        name="out_proj",
    )(x, ya, yb, yc, yd, w)


def _gate_body(u_ref, v_ref, wm_ref, bias_ref, g_ref, b_ref, y_ref, *vr_ref, tm):
    gu = jax.nn.gelu(u_ref[...], approximate=True)
    gv = jax.nn.gelu(v_ref[...], approximate=True)
    mu = jnp.mean(gv, axis=-1, keepdims=True)
    var = jnp.mean(jnp.square(gv - mu), axis=-1, keepdims=True)
    vn = (gv - mu) * lax.rsqrt(var + EPS) * g_ref[...] + b_ref[...]
    if vr_ref:
        vr_ref[0][...] = vn
    vnb = vn.astype(BF16)
    for c in range(tm // A_CHUNK):
        rows = slice(c * A_CHUNK, (c + 1) * A_CHUNK)
        for h in range(W_GROUP // A_CHUNK):
            cols = slice(h * A_CHUNK, (h + 1) * A_CHUNK)
            z = jnp.dot(wm_ref[h], vnb[rows, cols], preferred_element_type=F32) + bias_ref[:, cols]
            y_ref[rows, cols] = (gu[rows, cols] * z).astype(BF16)


def _gate(p2d, wm, bias, ln_g, ln_b, *, tm, with_rows):
    m = p2d.shape[0]
    row_spec = pl.BlockSpec((tm, W_GROUP), lambda i: (i, 0))
    out_shape = [jax.ShapeDtypeStruct((m, W_GROUP), BF16)]
    out_specs = [row_spec]
    if with_rows:
        out_shape.append(jax.ShapeDtypeStruct((m, W_GROUP), F32))
        out_specs.append(row_spec)
    res = pl.pallas_call(
        functools.partial(_gate_body, tm=tm),
        grid=(m // tm,),
        in_specs=[
            pl.BlockSpec((tm, W_GROUP), lambda i: (i, COL_A_U)),
            pl.BlockSpec((tm, W_GROUP), lambda i: (i, COL_A_V)),
            pl.BlockSpec(wm.shape, lambda i: (0, 0, 0)),
            pl.BlockSpec(bias.shape, lambda i: (0, 0)),
            pl.BlockSpec((1, W_GROUP), lambda i: (0, 0)),
            pl.BlockSpec((1, W_GROUP), lambda i: (0, 0)),
        ],
        out_specs=out_specs,
        out_shape=out_shape,
        compiler_params=_cparams(("parallel",)),
        name="spatial_gate",
    )(p2d, p2d, wm, bias, ln_g, ln_b)
    return res if with_rows else (res[0], None)


def _ret_body(q_ref, k_ref, v_ref, g_ref, cos_ref, sin_ref, dm_ref, qd_ref, kd_ref, s0_ref,
              *rest, bb, n_c, chunk_decay, creates):
    y_ref, so_ref, s_ref = rest[-3:]
    c = pl.program_id(1)

    @pl.when(c == 0)
    def _():
        s_ref[...] = s0_ref[...]

    cos = cos_ref[...]
    sin = sin_ref[...]
    nt = (((1,), (1,)), ((), ()))
    tn = (((0,), (0,)), ((), ()))
    units = [(b, h, slice(h * R_HD, (h + 1) * R_HD)) for b in range(bb) for h in range(R_HEADS)]
    idx = range(len(units))

    def rope(x):
        return x * cos + pltpu.roll(x, R_HD // 2, axis=1) * sin

    qr = [rope(q_ref[b, :, cols]) for b, h, cols in units]
    kr = [rope(k_ref[b, :, cols]) * (R_HD ** -0.5) for b, h, cols in units]
    v = [v_ref[b, :, cols].astype(BF16) for b, h, cols in units]
    s = [s_ref[b, h] for b, h, cols in units]
    sc = [lax.dot_general(qr[i].astype(BF16), kr[i].astype(BF16), nt, preferred_element_type=F32)
          * dm_ref[units[i][1]] for i in idx]
    cross = [jnp.dot((qr[i] * qd_ref[:, units[i][2]]).astype(BF16), s[i].astype(BF16),
                     preferred_element_type=F32) for i in idx]
    kv = [lax.dot_general((kr[i] * kd_ref[:, units[i][2]]).astype(BF16), v[i], tn,
                          preferred_element_type=F32) for i in idx]
    o = [cross[i] + jnp.dot(sc[i].astype(BF16), v[i], preferred_element_type=F32) for i in idx]
    for i, (b, h, cols) in enumerate(units):
        s_ref[b, h] = chunk_decay[h] * s[i] + kv[i]
        on = o[i] * lax.rsqrt(jnp.mean(o[i] * o[i], axis=-1, keepdims=True) + EPS)
        g = g_ref[b, :, cols]
        y_ref[b, :, cols] = (on * (g * jax.nn.sigmoid(g))).astype(BF16)

    @pl.when(c == n_c - 1)
    def _():
        _own_slot(so_ref, creates)[...] = s_ref[...]


def _ret_tables(cl, pos0, length):
    half = R_HD // 2
    inv = ROPE_BASE ** (-jnp.arange(half, dtype=F32) / half)
    pos = pos0 + jnp.arange(length, dtype=F32)
    ang = pos[:, None] * inv[None, :]
    cos = jnp.cos(ang)
    sin = jnp.sin(ang)
    cos_t = jnp.concatenate([cos, cos], axis=-1)
    sin_t = jnp.concatenate([-sin, sin], axis=-1)
    log_gamma = np.log(1.0 - 2.0 ** (-5.0 - np.arange(R_HEADS, dtype=np.float64)))
    idx = np.arange(cl, dtype=np.float64)
    diff = idx[:, None] - idx[None, :]
    dmat = np.where(diff >= 0, np.exp(np.maximum(diff, 0.0)[None] * log_gamma[:, None, None]), 0.0)
    kdec = np.exp((cl - 1.0 - idx)[:, None] * log_gamma[None, :])
    qdec = np.exp((idx + 1.0)[:, None] * log_gamma[None, :])
    chunk_decay = tuple(float(x) for x in np.exp(cl * log_gamma))
    rep = lambda a: jnp.asarray(np.repeat(a, R_HD, axis=1), F32)
    return cos_t, sin_t, jnp.asarray(dmat, F32), rep(qdec), rep(kdec), chunk_decay


def _stack_slot(stack, per_layer_shape, block, n_inputs):
    prev, slot, depth = stack
    nd = len(per_layer_shape)
    shape = jax.ShapeDtypeStruct((depth,) + tuple(per_layer_shape), F32)
    if prev is None:
        spec = pl.BlockSpec((depth,) + block, lambda b, t: (0, b) + (0,) * (nd - 1))
        return spec, shape, [], [], {}
    spec = pl.BlockSpec((None,) + block, lambda b, t: (slot, b) + (0,) * (nd - 1))
    return spec, shape, [prev], [pl.BlockSpec(memory_space=pl.ANY)], {n_inputs: 1}


def _own_slot(so_ref, creates):
    if creates is None:
        return so_ref
    slot, depth = creates
    for other in range(depth):
        if other != slot:
            so_ref[other] = jnp.zeros(so_ref.shape[1:], F32)
    return so_ref.at[slot]


def _retention(p3d, s0_all, layer, pos0, stack, *, bb):
    bsz, length, _ = p3d.shape
    cl = min(R_CHUNK, length)
    n_c = length // cl
    cos_t, sin_t, dmat, qdec, kdec, chunk_decay = _ret_tables(cl, pos0, length)

    def col(j):
        return pl.BlockSpec((bb, cl, W_GROUP), lambda b, c: (b, c, j))

    tab = pl.BlockSpec((cl, R_HD), lambda b, c: (c, 0))
    state, state_shape, extra, extra_specs, aliases = _stack_slot(
        stack, s0_all.shape[1:], (bb, R_HEADS, R_HD, R_HD), n_inputs=10)
    return pl.pallas_call(
        functools.partial(_ret_body, bb=bb, n_c=n_c, chunk_decay=chunk_decay,
                          creates=None if stack[0] is not None else stack[1:]),
        grid=(bsz // bb, n_c),
        in_specs=[col(COL_R_Q), col(COL_R_K), col(COL_R_V), col(COL_R_G), tab, tab,
                  pl.BlockSpec(dmat.shape, lambda b, c: (0, 0, 0)),
                  pl.BlockSpec(qdec.shape, lambda b, c: (0, 0)),
                  pl.BlockSpec(kdec.shape, lambda b, c: (0, 0)),
                  pl.BlockSpec((None, bb, R_HEADS, R_HD, R_HD), lambda b, c: (layer, b, 0, 0, 0))]
                 + extra_specs,
        out_specs=[pl.BlockSpec((bb, cl, W_GROUP), lambda b, c: (b, c, 0)), state],
        out_shape=[jax.ShapeDtypeStruct((bsz, length, W_GROUP), BF16), state_shape],
        input_output_aliases=aliases,
        scratch_shapes=[pltpu.VMEM((bb, R_HEADS, R_HD, R_HD), F32)],
        compiler_params=_cparams(("parallel", "arbitrary")),
        name="retention",
    )(p3d, p3d, p3d, p3d, cos_t, sin_t, dmat, qdec, kdec, s0_all, *extra)


def _conv_body(bg_ref, cg_ref, h_ref, buf_ref, w_ref, y_ref, st_ref, carry_ref, *, bb, tt):
    @pl.when(pl.program_id(1) == 0)
    def _():
        carry_ref[...] = buf_ref[...]

    shape = (bb, tt, W_GROUP)
    z = cg_ref[...] * h_ref[...]
    z2 = z.reshape(bb * tt, W_GROUP)
    r1 = pltpu.roll(z2, 1, axis=0).reshape(shape)
    r2 = pltpu.roll(z2, 2, axis=0).reshape(shape)
    tpos = lax.broadcasted_iota(jnp.int32, shape, 1)
    c0 = carry_ref[:, 0:1, :]
    c1 = carry_ref[:, 1:2, :]
    zm1 = jnp.where(tpos == 0, c1, r1)
    zm2 = jnp.where(tpos == 0, c0, jnp.where(tpos == 1, c1, r2))
    y = w_ref[0:1, :] * zm2 + w_ref[1:2, :] * zm1 + w_ref[2:3, :] * z
    y_ref[...] = (bg_ref[...] * y).astype(BF16)
    new = cg_ref[:, tt - 2:tt, :] * h_ref[:, tt - 2:tt, :]
    carry_ref[...] = new
    st_ref[...] = new


def _conv(p3d, buf, w, *, bb, tt):
    bsz, length, _ = p3d.shape

    def col(j):
        return pl.BlockSpec((bb, tt, W_GROUP), lambda b, t: (b, t, j))

    state = pl.BlockSpec((bb, C_WIDTH - 1, W_GROUP), lambda b, t: (b, 0, 0))
    return pl.pallas_call(
        functools.partial(_conv_body, bb=bb, tt=tt),
        grid=(bsz // bb, length // tt),
        in_specs=[col(COL_C_B), col(COL_C_C), col(COL_C_H), state,
                  pl.BlockSpec(w.shape, lambda b, t: (0, 0))],
        out_specs=[pl.BlockSpec((bb, tt, W_GROUP), lambda b, t: (b, t, 0)), state],
        out_shape=[jax.ShapeDtypeStruct((bsz, length, W_GROUP), BF16),
                   jax.ShapeDtypeStruct(buf.shape, F32)],
        scratch_shapes=[pltpu.VMEM((bb, C_WIDTH - 1, W_GROUP), F32)],
        compiler_params=_cparams(("parallel", "arbitrary")),
        name="short_conv",
    )(p3d, p3d, p3d, buf, w)


def _group_dot(xb, ones_bd):
    wb = ones_bd.shape[0]
    return jnp.concatenate(
        [jnp.dot(xb[:, i * wb:(i + 1) * wb], ones_bd, preferred_element_type=F32)
         for i in range(W_GROUP // wb)], axis=1)


def _head_sum(x, ones_bd):
    return _group_dot(x.astype(BF16), ones_bd)


def _rwkvc_body(r_ref, k_ref, v_ref, lo_ref, sh_ref, s0_ref,
                mu_ref, mulo_ref, w0_ref, w2_ref, a0_ref, a2_ref, g2_ref, kk_ref, ka_ref, rk_ref,
                lnw_ref, lnb_ref, ones_ref, *rest, bb, tt, n_t, gsz, chunk, creates):
    (y_ref, so_ref,
     s2_ref, carry_ref, carrylo_ref, pt_s, rt_s, qh_s, kh_s, qb_s, kb_s, v_s, ec_s, ys) = rest[-14:]
    tb = pl.program_id(1)
    n = bb * tt
    w3 = (bb, tt, W_GROUP)
    ones_bd = ones_ref[...]
    n_pair = K_HEADS // 2
    pair_w = 2 * K_HD
    zero_blk = jnp.zeros((K_HD, K_HD), F32)

    @pl.when(tb == 0)
    def _():
        for b in range(bb):
            for p in range(n_pair):
                top = jnp.concatenate([s0_ref[b, 2 * p], zero_blk], axis=1)
                bot = jnp.concatenate([zero_blk, s0_ref[b, 2 * p + 1]], axis=1)
                s2_ref[b, p] = jnp.concatenate([top, bot], axis=0)
        carry_ref[...] = sh_ref[:, :, 0:3 * W_GROUP]
        carrylo_ref[...] = sh_ref[:, :, 3 * W_GROUP:K_COLS]

    tpos = lax.broadcasted_iota(jnp.int32, w3, 1)
    tpos_lo = lax.broadcasted_iota(jnp.int32, (bb, tt, LORA_COLS), 1)

    def shifted(x, carry, mu, mask):
        prev = pltpu.roll(x.reshape(n, x.shape[-1]), 1, axis=0).reshape(x.shape)
        prev = jnp.where(mask == 0, carry, prev)
        return (x + (prev - x) * mu).reshape(n, x.shape[-1])

    r_in, k_in, v_in, lo_in = r_ref[...], k_ref[...], v_ref[...], lo_ref[...]
    r = shifted(r_in, carry_ref[:, :, 0:W_GROUP], mu_ref[:, 0:W_GROUP], tpos)
    k = shifted(k_in, carry_ref[:, :, W_GROUP:2 * W_GROUP], mu_ref[:, W_GROUP:2 * W_GROUP], tpos)
    v = shifted(v_in, carry_ref[:, :, 2 * W_GROUP:3 * W_GROUP], mu_ref[:, 2 * W_GROUP:3 * W_GROUP], tpos)
    lo = shifted(lo_in, carrylo_ref[...], mulo_ref[...], tpos_lo)
    carry_ref[:, :, 0:W_GROUP] = r_ref[:, tt - 1:tt, :]
    carry_ref[:, :, W_GROUP:2 * W_GROUP] = k_ref[:, tt - 1:tt, :]
    carry_ref[:, :, 2 * W_GROUP:3 * W_GROUP] = v_ref[:, tt - 1:tt, :]
    carrylo_ref[...] = lo_ref[:, tt - 1:tt, :]

    zw = w0_ref[...] + jnp.dot(jnp.tanh(lo).astype(BF16), w2_ref[...], preferred_element_type=F32)
    logw = -RWKV_DECAY_SCALE * jax.nn.sigmoid(zw)
    a = jax.nn.sigmoid(a0_ref[...] + jnp.dot(lo.astype(BF16), a2_ref[...], preferred_element_type=F32))
    gate = jnp.dot(jax.nn.sigmoid(lo).astype(BF16), g2_ref[...], preferred_element_type=F32)
    kk = k * kk_ref[...]
    kk = kk * lax.rsqrt(jnp.maximum(_head_sum(kk * kk, ones_bd), 1e-24))
    k2 = k * (1.0 + (a - 1.0) * ka_ref[...])
    bonus = _head_sum(r * k2 * rk_ref[...], ones_bd) * v
    q = -(kk * a)

    cpos = lax.broadcasted_iota(jnp.int32, (n, W_GROUP), 0) % chunk
    g = logw
    step = 1
    while step < chunk:
        g = g + jnp.where(cpos >= step, pltpu.roll(g, step, axis=0), 0.0)
        step *= 2
    g3 = g.reshape(n // chunk, chunk, W_GROUP)
    gtot = jnp.broadcast_to(g3[:, chunk - 1:chunk, :], g3.shape).reshape(n, W_GROUP)
    e_neg = jnp.exp(-g)
    e_rem = jnp.exp(gtot - g)
    pt_s[...] = (kk * jnp.exp(g - logw)).reshape(w3)
    rt_s[...] = (r * jnp.exp(g)).reshape(w3)
    qh_s[...] = (q * e_neg).reshape(w3)
    kh_s[...] = (k2 * e_neg).reshape(w3)
    qb_s[...] = (q * e_rem).reshape(w3)
    kb_s[...] = (k2 * e_rem).reshape(w3)
    v_s[...] = v.reshape(w3)
    ec_s[...] = jnp.exp(gtot).reshape(w3)

    c2, c4 = 2 * chunk, 4 * chunk
    even = lax.broadcasted_iota(jnp.int32, (chunk, pair_w), 1) < K_HD
    ri = lax.broadcasted_iota(jnp.int32, (c4, c4), 0)
    ci = lax.broadcasted_iota(jnp.int32, (c4, c4), 1)
    keep = ci % chunk < ri % chunk + ri // c2
    right = lax.broadcasted_iota(jnp.int32, (c2, c4), 1) >= c2
    eye = (lax.broadcasted_iota(jnp.int32, (c2, c2), 0)
           == lax.broadcasted_iota(jnp.int32, (c2, c2), 1)).astype(F32)
    nt = (((1,), (1,)), ((), ()))
    tn = (((0,), (0,)), ((), ()))
    n_double = chunk.bit_length() - 2

    def two(x):
        return jnp.concatenate([jnp.where(even, x, 0.0), jnp.where(even, 0.0, x)], axis=0)

    def mm(x, y):
        return jnp.dot(x.astype(BF16), y.astype(BF16), preferred_element_type=F32)

    def units(args):
        v2, pr, qk, qkb, s2, ec = zip(*args)
        idx = range(len(args))
        apr = [jnp.where(keep, lax.dot_general(pr[i], qk[i], nt, preferred_element_type=F32), 0.0) for i in idx]
        prs = [lax.dot_general(pr[i], s2[i].astype(BF16), nt, preferred_element_type=F32) for i in idx]
        apk = [mm(jnp.where(right, apr[i][0:c2], 0.0), jnp.concatenate([v2[i], v2[i]], axis=0)) for i in idx]
        power = [apr[i][0:c2, 0:c2] for i in idx]
        inv = [eye + power[i] for i in idx]
        for _ in range(n_double):
            power = [mm(power[i], power[i]) for i in idx]
            inv = [inv[i] + mm(inv[i], power[i]) for i in idx]
        u2 = [mm(inv[i], prs[i][0:c2] + apk[i]) for i in idx]
        uv = [jnp.concatenate([u2[i], v2[i]], axis=0).astype(BF16) for i in idx]
        y2 = [prs[i][c2:c4] + jnp.dot(apr[i][c2:c4].astype(BF16), uv[i], preferred_element_type=F32)
              for i in idx]
        s_new = [s2[i] * ec[i] + lax.dot_general(uv[i], qkb[i], tn, preferred_element_type=F32) for i in idx]
        return [(y2[i][0:chunk] + y2[i][chunk:c2], s_new[i]) for i in idx]

    def chunk_step(i, carry):
        c = i // (bb // gsz)
        b0 = (i % (bb // gsz)) * gsz
        r0 = pl.multiple_of(c * chunk, chunk)
        rows = pl.ds(r0, chunk)
        where = [(b0 + j, slice(p * pair_w, (p + 1) * pair_w), p) for j in range(gsz) for p in range(n_pair)]
        args = []
        for b, lanes, p in where:
            ld = lambda ref: two(ref[b, rows, lanes])
            args.append((ld(v_s),
                         jnp.concatenate([ld(pt_s), ld(rt_s)], axis=0).astype(BF16),
                         jnp.concatenate([ld(qh_s), ld(kh_s)], axis=0).astype(BF16),
                         jnp.concatenate([ld(qb_s), ld(kb_s)], axis=0).astype(BF16),
                         s2_ref[b, p], ec_s[b, pl.ds(r0, 1), lanes]))
        for (b, lanes, p), (y, s_new) in zip(where, units(args)):
            ys[b, rows, lanes] = y
            s2_ref[b, p] = s_new
        return carry

    lax.fori_loop(0, (tt // chunk) * (bb // gsz), chunk_step, 0)

    y = ys[...].reshape(n, W_GROUP)
    mean = _head_sum(y, ones_bd) * (1.0 / K_HD)
    yc = y - mean
    var = _head_sum(yc * yc, ones_bd) * (1.0 / K_HD)
    out = (yc * lax.rsqrt(var + GN_EPS) * lnw_ref[...] + lnb_ref[...] + bonus) * gate
    y_ref[...] = out.reshape(w3).astype(BF16)

    @pl.when(tb == n_t - 1)
    def _():
        own = _own_slot(so_ref, creates)
        for b in range(bb):
            for p in range(n_pair):
                own[b, 2 * p] = s2_ref[b, p, 0:K_HD, 0:K_HD]
                own[b, 2 * p + 1] = s2_ref[b, p, K_HD:pair_w, K_HD:pair_w]


def _rwkvc(p3d, shift, s0_all, layer, prm, stack, *, bb, tt, gsz, chunk):
    bsz, length, _ = p3d.shape
    n_t = length // tt

    def col(j):
        return pl.BlockSpec((bb, tt, W_GROUP), lambda b, t: (b, t, j))

    def whole(a):
        nd = a.ndim
        return pl.BlockSpec(a.shape, lambda b, t: (0,) * nd)

    params = [prm[nm] for nm in ("mu", "mu_lo", "w0", "w2", "a0", "a2", "g2", "k_k", "k_a", "r_k",
                                 "ln_w", "ln_b", "ones_bd")]
    state, state_shape, extra, extra_specs, aliases = _stack_slot(
        stack, s0_all.shape[1:], (bb, K_HEADS, K_HD, K_HD), n_inputs=6 + len(params))
    blk = pltpu.VMEM((bb, tt, W_GROUP), F32)
    return pl.pallas_call(
        functools.partial(_rwkvc_body, bb=bb, tt=tt, n_t=n_t, gsz=gsz, chunk=chunk,
                          creates=None if stack[0] is not None else stack[1:]),
        grid=(bsz // bb, n_t),
        in_specs=[col(COL_K_R), col(COL_K_K), col(COL_K_V),
                  pl.BlockSpec((bb, tt, LORA_COLS), lambda b, t: (b, t, COL_K_LORA)),
                  pl.BlockSpec((bb, 1, K_COLS), lambda b, t: (b, 0, 0)),
                  pl.BlockSpec((None, bb, K_HEADS, K_HD, K_HD), lambda b, t: (layer, b, 0, 0, 0))]
                 + [whole(a) for a in params] + extra_specs,
        out_specs=[pl.BlockSpec((bb, tt, W_GROUP), lambda b, t: (b, t, 0)), state],
        out_shape=[jax.ShapeDtypeStruct((bsz, length, W_GROUP), BF16), state_shape],
        input_output_aliases=aliases,
        scratch_shapes=[pltpu.VMEM((bb, K_HEADS // 2, 2 * K_HD, 2 * K_HD), F32),
                        pltpu.VMEM((bb, 1, 3 * W_GROUP), F32),
                        pltpu.VMEM((bb, 1, LORA_COLS), F32),
                        blk, blk, blk, blk, blk, blk, blk, blk, blk],
        compiler_params=_cparams(("parallel", "arbitrary")),
        name="rwkv7c",
    )(p3d, p3d, p3d, p3d, shift, s0_all, *params, *extra)


def _gate_mixing(w_s, b_s, seq):
    cl = min(A_CHUNK, seq)
    wm = jnp.tril(w_s[:, :cl, :cl])
    bias = b_s[:, :cl]
    rep = A_CHUNK // cl
    if rep > 1:
        eye = jnp.eye(rep, dtype=w_s.dtype)
        wm = jnp.einsum("ab,hts->hatbs", eye, wm).reshape(w_s.shape[0], A_CHUNK, A_CHUNK)
        bias = jnp.tile(bias, (1, rep))
    bias = jnp.repeat(bias.T, A_CHUNK, axis=1)
    return wm.astype(BF16), bias


def _pad_rows(w, row0):
    return jnp.zeros((LORA_COLS, W_GROUP), F32).at[row0:row0 + w.shape[0]].set(w).astype(BF16)


def _layer_params(l, ffn1_norm, ffn1_w_gate, ffn1_w_up, ffn1_w_down, mix_norm, w_in, w_out,
                  a_w_s, a_b_s, a_ln_g, a_ln_b, c_conv_w,
                  k_mu, k_w0, k_w2, k_a0, k_a2, k_g2, k_k_k, k_k_a, k_r_k, k_ln_w, k_ln_b,
                  ffn2_norm, ffn2_w_gate, ffn2_w_up, ffn2_w_down):
    row = lambda a: a[l].reshape(1, -1)
    head_of = np.arange(V7X_MXU_DIM) // K_HD
    ones_bd = jnp.asarray(head_of[:, None] == head_of[None, :], BF16)
    rwkv = dict(
        mu=k_mu[l][None, :3 * W_GROUP], mu_lo=k_mu[l][None, 3 * W_GROUP:],
        w0=row(k_w0), w2=_pad_rows(k_w2[l], 0),
        a0=row(k_a0), a2=_pad_rows(k_a2[l], W_LORA),
        g2=_pad_rows(k_g2[l], W_LORA + A_LORA),
        k_k=row(k_k_k), k_a=row(k_k_a), r_k=row(k_r_k), ln_w=row(k_ln_w), ln_b=row(k_ln_b),
        ones_bd=ones_bd)
    return dict(
        layer=l, ffn_layer=l,
        ffn1=(row(ffn1_norm), ffn1_w_gate, ffn1_w_up, ffn1_w_down),
        ffn2=(row(ffn2_norm), ffn2_w_gate, ffn2_w_up, ffn2_w_down),
        mix_norm=row(mix_norm), w_in=w_in, w_out=w_out,
        a_w_s=a_w_s[l], a_b_s=a_b_s[l], a_ln_g=row(a_ln_g), a_ln_b=row(a_ln_b),
        conv_w=c_conv_w[l], rwkv=rwkv)


def _stream_layer(x2d, bsz, length, pos0, ret_s0_all, ret_layer, conv_buf, rw_shift, rw_s0_all, p, fn, stacks, *,
                  final, cfg):
    layer = p["layer"]
    ffn_kw = dict(layer=p["ffn_layer"], tm=cfg["ffn_tm"], tf=cfg["tf"], cast=cfg["ffn_cast"])
    x1 = _ffn(x2d, *p["ffn1"], fn, final=False, **ffn_kw)
    casted = {}
    if cfg["ffn_cast"]:
        x1, casted["ffn1"] = x1
    proj = _proj(x1, p["mix_norm"], p["w_in"], layer=p["ffn_layer"], tm=cfg["proj_tm"], tn=cfg["tn"],
                 cast=cfg["ffn_cast"])
    if cfg["ffn_cast"]:
        proj, casted["w_in"] = proj
    p3d = proj.reshape(bsz, length, proj.shape[1])

    wm, bias = _gate_mixing(p["a_w_s"], p["a_b_s"], length)
    ya, v_rows = _gate(proj, wm, bias, p["a_ln_g"], p["a_ln_b"], tm=cfg["gate_tm"],
                       with_rows=cfg["with_rows"])
    yb, ret_s = _retention(p3d, ret_s0_all, ret_layer, pos0, stacks["ret"], bb=cfg["ret_bb"])
    yc, conv_new = _conv(p3d, conv_buf, p["conv_w"], bb=cfg["conv_bb"], tt=cfg["conv_tt"])
    yd, rw_s = _rwkvc(p3d, rw_shift[:, None, :], rw_s0_all, ret_layer, p["rwkv"], stacks["rwkv"],
                      bb=cfg["rwkv_bb"], tt=cfg["rwkv_tt"], gsz=cfg["rwkv_gsz"], chunk=cfg["rwkv_chunk"])
    shift_new = p3d[:, length - 1, K_COL0:K_COL0 + K_COLS]

    flat = lambda y: y.reshape(bsz * length, W_GROUP)
    x2 = _outproj(x1, ya, flat(yb), flat(yc), flat(yd), p["w_out"], layer=layer, tm=cfg["tm"])
    x3 = _ffn(x2, *p["ffn2"], fn, final=final, **ffn_kw)
    if cfg["ffn_cast"]:
        x3, casted["ffn2"] = x3
    return x3, ret_s, conv_new, shift_new, rw_s, v_rows, casted


def _stream_cfg(bsz, length, sample):
    m = bsz * length
    tm = min(512, m)
    if sample:
        return dict(tm=tm, ffn_tm=m, ffn_cast=True, proj_tm=m, tf=256, tn=640,
                    gate_tm=min(512, m), with_rows=True,
                    ret_bb=8, conv_bb=32, conv_tt=length, rwkv_bb=16, rwkv_tt=length, rwkv_gsz=8,
                    rwkv_chunk=min(RWKV_CHUNK, length))
    return dict(tm=tm, ffn_tm=min(1024, m), ffn_cast=False, proj_tm=min(1024, m), tf=512, tn=1280,
                gate_tm=min(512, m), with_rows=False,
                ret_bb=bsz, conv_bb=1, conv_tt=min(512, length), rwkv_bb=bsz, rwkv_tt=min(128, length),
                rwkv_gsz=4, rwkv_chunk=min(RWKV_CHUNK, length))


def kernel(x_prompt, x_sample, state_ret, state_conv, state_rwkv_shift, state_rwkv, ffn1_norm, ffn1_w_gate, ffn1_w_up, ffn1_w_down, mix_norm, w_in, w_out, a_w_s, a_b_s, a_ln_g, a_ln_b, c_conv_w, k_mu, k_w0, k_w2, k_a0, k_a2, k_g2, k_k_k, k_k_a, k_r_k, k_ln_w, k_ln_b, ffn2_norm, ffn2_w_gate, ffn2_w_up, ffn2_w_down, final_norm):
    bp, lp, d = x_prompt.shape
    bs, ls, _ = x_sample.shape
    depth = ffn1_norm.shape[0]
    cfg_p = _stream_cfg(bp, lp, sample=False)
    cfg_s = _stream_cfg(bs, ls, sample=True)
    fn = final_norm.reshape(1, d)

    zero_ret = jnp.zeros((1, bp, R_HEADS, R_HD, R_HD), F32)
    zero_conv = jnp.zeros((bp, C_WIDTH - 1, W_GROUP), F32)
    zero_shift = jnp.zeros((bp, K_COLS), F32)
    zero_rw = jnp.zeros((1, bp, K_HEADS, K_HD, K_HD), F32)

    w_out = w_out.astype(BF16)
    xp = x_prompt.reshape(bp * lp, d)
    xs = x_sample.reshape(bs * ls, d)
    outs = [[] for _ in range(5)]
    rp = rs = wp = ws = None
    for l in range(depth):
        p = _layer_params(l, ffn1_norm, ffn1_w_gate, ffn1_w_up, ffn1_w_down, mix_norm, w_in, w_out,
                          a_w_s, a_b_s, a_ln_g, a_ln_b, c_conv_w,
                          k_mu, k_w0, k_w2, k_a0, k_a2, k_g2, k_k_k, k_k_a, k_r_k, k_ln_w, k_ln_b,
                          ffn2_norm, ffn2_w_gate, ffn2_w_up, ffn2_w_down)
        final = l == depth - 1
        xs, rs, cs, ss, ws, vs, w16 = _stream_layer(xs, bs, ls, float(PAST_LEN), state_ret, l, state_conv[l],
                                                    state_rwkv_shift[l], state_rwkv, p, fn,
                                                    dict(ret=(rs, l, depth), rwkv=(ws, l, depth)),
                                                    final=final, cfg=cfg_s)
        p = dict(p, ffn_layer=0, ffn1=p["ffn1"][:1] + w16["ffn1"], ffn2=p["ffn2"][:1] + w16["ffn2"],
                 w_in=w16["w_in"])
        xp, rp, cp, sp, wp, _, _ = _stream_layer(xp, bp, lp, 0.0, zero_ret, 0, zero_conv, zero_shift, zero_rw,
                                                 p, fn, dict(ret=(rp, l, depth), rwkv=(wp, l, depth)),
                                                 final=final, cfg=cfg_p)
        for acc, val in zip(outs, (cp, cs, sp, ss, vs.reshape(bs, ls, W_GROUP))):
            acc.append(val)

    conv_p, conv_s, shift_p, shift_s, v_s = (jnp.stack(o) for o in outs)
    return (xp.reshape(bp, lp, d), xs.reshape(bs, ls, d), rp, rs, conv_p, conv_s, shift_p, shift_s, wp, ws, v_s)
```
